```python
import math
import jax, jax.numpy as jnp
from jax import lax
import numpy as np

D_MODEL = 1024
BATCH = 2
SEQ = 8192
DEPTH = 2

D_FF = 2816
EPS = 1e-6
HG_HEADS = 4
HG_DK = 128
HG_DV = 128
D_HG = HG_HEADS * HG_DK
HG_CHUNK = 64
SSD_HEADS = 8
SSD_HEADDIM = 64
D_SSD = SSD_HEADS * SSD_HEADDIM
SSD_GROUPS = 2
SSD_STATE = 128
SSD_CONV = 4
SSD_CHUNK = 64
D_XBC = D_SSD + 2 * SSD_GROUPS * SSD_STATE
S5_GROUPS = 32
S5_GROUP_CH = 16
D_S5 = S5_GROUPS * S5_GROUP_CH
S5_STATE = 64
D_MIX = D_HG + D_SSD + D_S5
D_IN = 4 * D_HG + D_SSD + D_XBC + SSD_HEADS + D_S5

kernel_name = 'hybrid_hgrn2_ssd_s5_macaron'


def _rmsnorm(x, g):
    x32 = x.astype(jnp.float32)
    y = x32 * lax.rsqrt(jnp.mean(x32 * x32, axis=-1, keepdims=True) + EPS)
    return (y * g.astype(jnp.float32)).astype(x.dtype)


def _swiglu(x, w_gate, w_up, w_down):
    return (jax.nn.silu(x @ w_gate) * (x @ w_up)) @ w_down


def _hgrn2(q_raw, f_raw, i_raw, g_raw, lb, gnorm_w):
    out_dtype = q_raw.dtype
    bsz, seq, _ = q_raw.shape
    nc = seq // HG_CHUNK
    f32 = jnp.float32
    lb = lb.astype(f32)
    q = jax.nn.silu(q_raw.astype(f32))
    log_f = jnp.logaddexp(jnp.log(lb), jnp.log1p(-lb) + jax.nn.log_sigmoid(f_raw.astype(f32)))
    k = -jnp.expm1(log_f)
    v = i_raw.astype(f32)

    def chunks(t, d):
        return t.reshape(bsz, nc, HG_CHUNK, HG_HEADS, d).transpose(1, 0, 3, 2, 4)

    qc, kc, vc = chunks(q, HG_DK), chunks(k, HG_DK), chunks(v, HG_DV)
    bc = jnp.cumsum(chunks(log_f, HG_DK), axis=3)
    causal = jnp.tril(jnp.ones((HG_CHUNK, HG_CHUNK), dtype=bool))[:, :, None]

    def step(S, xs):
        q_, k_, v_, b_ = xs
        o_inter = jnp.einsum('bhtk,bhkv->bhtv', q_ * jnp.exp(b_), S)
        rel = b_[:, :, :, None, :] - b_[:, :, None, :, :]
        decay = jnp.exp(jnp.where(causal, rel, -jnp.inf))
        scores = jnp.einsum('bhtk,bhsk,bhtsk->bhts', q_, k_, decay)
        o_intra = jnp.einsum('bhts,bhsv->bhtv', scores, v_)
        b_last = b_[:, :, -1:, :]
        S_new = jnp.exp(b_last[:, :, 0, :])[..., None] * S + jnp.einsum(
            'bhsk,bhsv->bhkv', k_ * jnp.exp(b_last - b_), v_)
        return S_new, o_inter + o_intra

    S0 = jnp.zeros((bsz, HG_HEADS, HG_DK, HG_DV), f32)
    _, o = lax.scan(step, S0, (qc, kc, vc, bc))
    o = o.transpose(1, 0, 3, 2, 4).reshape(bsz, seq, HG_HEADS, HG_DV)
    o = o * lax.rsqrt(jnp.mean(o * o, axis=-1, keepdims=True) + EPS) * gnorm_w.astype(f32)
    o = o.reshape(bsz, seq, D_HG) * jax.nn.silu(g_raw.astype(f32))
    return o.astype(out_dtype)


def _ssd(z, xbc, dt_raw, conv_w, conv_b, dt_bias, a_log, d_skip, norm_w):
    out_dtype = z.dtype
    bsz, seq, _ = z.shape
    nc = seq // SSD_CHUNK
    hg = SSD_HEADS // SSD_GROUPS
    f32 = jnp.float32
    xbc = lax.conv_general_dilated(
        xbc.astype(f32), conv_w.astype(f32)[:, None, :], window_strides=(1,),
        padding=[(SSD_CONV - 1, 0)], dimension_numbers=('NWC', 'WIO', 'NWC'),
        feature_group_count=D_XBC)
    xbc = jax.nn.silu(xbc + conv_b.astype(f32))
    x = xbc[..., :D_SSD]
    b_in = xbc[..., D_SSD:D_SSD + SSD_GROUPS * SSD_STATE]
    c_in = xbc[..., D_SSD + SSD_GROUPS * SSD_STATE:]
    dt = jax.nn.softplus(dt_raw.astype(f32) + dt_bias.astype(f32))
    a = -jnp.exp(a_log.astype(f32)).reshape(SSD_GROUPS, hg)

    xh = x.reshape(bsz, nc, SSD_CHUNK, SSD_GROUPS, hg, SSD_HEADDIM)
    bm = b_in.reshape(bsz, nc, SSD_CHUNK, SSD_GROUPS, SSD_STATE)
    cm = c_in.reshape(bsz, nc, SSD_CHUNK, SSD_GROUPS, SSD_STATE)
    dtc = dt.reshape(bsz, nc, SSD_CHUNK, SSD_GROUPS, hg)
    a_dt = (dtc * a).transpose(0, 3, 4, 1, 2)
    a_cs = jnp.cumsum(a_dt, axis=-1)
    x_dt = xh * dtc[..., None]
    causal = jnp.tril(jnp.ones((SSD_CHUNK, SSD_CHUNK), dtype=bool))
    l_dec = jnp.exp(jnp.where(causal, a_cs[..., :, None] - a_cs[..., None, :], -jnp.inf))
    scores = jnp.einsum('bclgn,bcsgn->bgcls', cm, bm)
    y_diag = jnp.einsum('bgcls,bghcls,bcsghp->bclghp', scores, l_dec, x_dt)
    decay_states = jnp.exp(a_cs[..., -1:] - a_cs)
    states = jnp.einsum('bcsgn,bghcs,bcsghp->cbghpn', bm, decay_states, x_dt)
    chunk_decay = jnp.exp(a_cs[..., -1]).transpose(3, 0, 1, 2)

    def step(h, xs):
        dec, st = xs
        return dec[..., None, None] * h + st, h

    h0 = jnp.zeros((bsz, SSD_GROUPS, hg, SSD_HEADDIM, SSD_STATE), f32)
    _, prev = lax.scan(step, h0, (chunk_decay, states))
    y_off = jnp.einsum('bclgn,cbghpn,bghcl->bclghp', cm, prev, jnp.exp(a_cs))
    y = (y_diag + y_off).reshape(bsz, seq, D_SSD) + x * jnp.repeat(d_skip.astype(f32), SSD_HEADDIM)
    y = (y * jax.nn.silu(z.astype(f32))).reshape(bsz, seq, SSD_GROUPS, D_SSD // SSD_GROUPS)
    y = y * lax.rsqrt(jnp.mean(y * y, axis=-1, keepdims=True) + EPS)
    y = y.reshape(bsz, seq, D_SSD) * norm_w.astype(f32)
    return y.astype(out_dtype)


def _s5_combine(e1, e2):
    a1r, a1i, b1r, b1i = e1
    a2r, a2i, b2r, b2i = e2
    return (a2r * a1r - a2i * a1i,
            a2r * a1i + a2i * a1r,
            a2r * b1r - a2i * b1i + b2r,
            a2r * b1i + a2i * b1r + b2i)


def _s5(u, a_re, a_im, b_re, b_im, c_re, c_im, d_skip, log_dt, glu_w, glu_b):
    out_dtype = u.dtype
    bsz, seq, _ = u.shape
    f32 = jnp.float32
    u32 = u.astype(f32)
    ug = u32.reshape(bsz, seq, S5_GROUPS, S5_GROUP_CH)
    ar, ai = a_re.astype(f32), a_im.astype(f32)
    delta = jnp.exp(log_dt.astype(f32))[:, None]
    mag = jnp.exp(ar * delta)
    ab_re, ab_im = mag * jnp.cos(ai * delta), mag * jnp.sin(ai * delta)
    den = ar * ar + ai * ai
    nr, ni = ab_re - 1.0, ab_im
    fr = (nr * ar + ni * ai) / den
    fi = (ni * ar - nr * ai) / den
    br, bi = b_re.astype(f32), b_im.astype(f32)
    bb_re = fr[..., None] * br - fi[..., None] * bi
    bb_im = fr[..., None] * bi + fi[..., None] * br
    bu_re = jnp.einsum('blgc,gpc->blgp', ug, bb_re)
    bu_im = jnp.einsum('blgc,gpc->blgp', ug, bb_im)
    elems = (jnp.broadcast_to(ab_re, bu_re.shape), jnp.broadcast_to(ab_im, bu_im.shape), bu_re, bu_im)
    _, _, xr, xi = lax.associative_scan(_s5_combine, elems, axis=1)
    y = jnp.einsum('blgp,gcp->blgc', xr, c_re.astype(f32)) - jnp.einsum('blgp,gcp->blgc', xi, c_im.astype(f32))
    y = y.reshape(bsz, seq, D_S5) + d_skip.astype(f32) * u32
    y = jax.nn.gelu(y)
    y = y * jax.nn.sigmoid(y @ glu_w.astype(f32) + glu_b.astype(f32))
    return y.astype(out_dtype)


def _mixer(h, w_in, w_out, lb, hg_gnorm, ssd_conv_w, ssd_conv_b, ssd_dt_bias, ssd_a_log, ssd_d,
           ssd_norm, s5_a_re, s5_a_im, s5_b_re, s5_b_im, s5_c_re, s5_c_im, s5_d, s5_log_dt,
           s5_glu_w, s5_glu_b):
    sizes = (D_HG, D_HG, D_HG, D_HG, D_SSD, D_XBC, SSD_HEADS, D_S5)
    idx = [int(i) for i in np.cumsum(sizes)[:-1]]
    q, f, i, g, z, xbc, dt, u = jnp.split(h @ w_in, idx, axis=-1)
    o_a = _hgrn2(q, f, i, g, lb, hg_gnorm)
    o_b = _ssd(z, xbc, dt, ssd_conv_w, ssd_conv_b, ssd_dt_bias, ssd_a_log, ssd_d, ssd_norm)
    o_c = _s5(u, s5_a_re, s5_a_im, s5_b_re, s5_b_im, s5_c_re, s5_c_im, s5_d, s5_log_dt, s5_glu_w, s5_glu_b)
    return jnp.concatenate([o_a, o_b, o_c], axis=-1) @ w_out


def setup_inputs(seed: int = 0) -> dict:
    key = jax.random.key(seed)
    ks = jax.random.split(key, 26)
    nrm = jax.random.normal
    f32 = jnp.float32
    dt0 = jnp.exp(jax.random.uniform(ks[12], (DEPTH, SSD_HEADS), f32, math.log(1e-3), math.log(1e-1)))
    return {
        'x': nrm(ks[0], (BATCH, SEQ, D_MODEL), f32),
        'norm_g': 1.0 + 0.02 * nrm(ks[1], (DEPTH, 6, D_MODEL), f32),
        'ffn_w_gate': nrm(ks[2], (DEPTH, 2, D_MODEL, D_FF), f32) * D_MODEL ** -0.5,
        'ffn_w_up': nrm(ks[3], (DEPTH, 2, D_MODEL, D_FF), f32) * D_MODEL ** -0.5,
        'ffn_w_down': nrm(ks[4], (DEPTH, 2, D_FF, D_MODEL), f32) * D_FF ** -0.5,
        'w_in': nrm(ks[5], (DEPTH, D_MODEL, D_IN), f32) * D_MODEL ** -0.5,
        'w_out': nrm(ks[6], (DEPTH, D_MIX, D_MODEL), f32) * D_MIX ** -0.5,
        'hg_lb_logits': 0.1 * nrm(ks[7], (DEPTH, D_HG), f32),
        'hg_gnorm': 1.0 + 0.02 * nrm(ks[8], (DEPTH, HG_DV), f32),
        'ssd_conv_w': nrm(ks[9], (DEPTH, SSD_CONV, D_XBC), f32) * SSD_CONV ** -0.5,
        'ssd_conv_b': 0.01 * nrm(ks[10], (DEPTH, D_XBC), f32),
        'ssd_dt_bias': dt0 + jnp.log(-jnp.expm1(-dt0)),
        'ssd_A_log': jnp.log(jax.random.uniform(ks[11], (DEPTH, SSD_HEADS), f32, 1.0, 16.0)),
        'ssd_D': 1.0 + 0.1 * nrm(ks[13], (DEPTH, SSD_HEADS), f32),
        'ssd_norm': 1.0 + 0.02 * nrm(ks[14], (DEPTH, D_SSD), f32),
        's5_A_re': -0.5 + 0.01 * nrm(ks[15], (DEPTH, S5_GROUPS, S5_STATE), f32),
        's5_A_im': math.pi * jnp.arange(S5_STATE, dtype=f32) + 0.01 * nrm(ks[16], (DEPTH, S5_GROUPS, S5_STATE), f32),
        's5_B_re': nrm(ks[17], (DEPTH, S5_GROUPS, S5_STATE, S5_GROUP_CH), f32) * (2 * S5_GROUP_CH) ** -0.5,
        's5_B_im': nrm(ks[18], (DEPTH, S5_GROUPS, S5_STATE, S5_GROUP_CH), f32) * (2 * S5_GROUP_CH) ** -0.5,
        's5_C_re': nrm(ks[19], (DEPTH, S5_GROUPS, S5_GROUP_CH, S5_STATE), f32) * S5_STATE ** -0.5,
        's5_C_im': nrm(ks[20], (DEPTH, S5_GROUPS, S5_GROUP_CH, S5_STATE), f32) * S5_STATE ** -0.5,
        's5_D': nrm(ks[21], (DEPTH, D_S5), f32),
        's5_log_dt': jax.random.uniform(ks[22], (DEPTH, S5_GROUPS), f32, math.log(1e-3), math.log(1e-1)),
        's5_glu_w': nrm(ks[23], (DEPTH, D_S5, D_S5), f32) * D_S5 ** -0.5,
        's5_glu_b': 0.01 * nrm(ks[24], (DEPTH, D_S5), f32),
    }


def reference(x, norm_g, ffn_w_gate, ffn_w_up, ffn_w_down, w_in, w_out, hg_lb_logits, hg_gnorm,
              ssd_conv_w, ssd_conv_b, ssd_dt_bias, ssd_A_log, ssd_D, ssd_norm, s5_A_re, s5_A_im,
              s5_B_re, s5_B_im, s5_C_re, s5_C_im, s5_D, s5_log_dt, s5_glu_w, s5_glu_b):
    lb_all = jnp.cumsum(jax.nn.softmax(hg_lb_logits.astype(jnp.float32), axis=0), axis=0)
    lb_all = lb_all - lb_all[:1]
    h = x
    for l in range(DEPTH):
        g = norm_g[l]
        y = _swiglu(_rmsnorm(h, g[0]), ffn_w_gate[l, 0], ffn_w_up[l, 0], ffn_w_down[l, 0])
        h = h + 0.5 * _rmsnorm(y, g[1])
        y = _mixer(_rmsnorm(h, g[2]), w_in[l], w_out[l], lb_all[l], hg_gnorm[l],
                   ssd_conv_w[l], ssd_conv_b[l], ssd_dt_bias[l], ssd_A_log[l], ssd_D[l], ssd_norm[l],
                   s5_A_re[l], s5_A_im[l], s5_B_re[l], s5_B_im[l], s5_C_re[l], s5_C_im[l], s5_D[l],
                   s5_log_dt[l], s5_glu_w[l], s5_glu_b[l])
        h = h + _rmsnorm(y, g[3])
        y = _swiglu(_rmsnorm(h, g[4]), ffn_w_gate[l, 1], ffn_w_up[l, 1], ffn_w_down[l, 1])
        h = h + 0.5 * _rmsnorm(y, g[5])
    return h
```

```python
import functools

import numpy as np
import jax
import jax.numpy as jnp
from jax import lax
from jax.experimental import pallas as pl
from jax.experimental.pallas import tpu as pltpu

F32 = jnp.float32
BF16 = jnp.bfloat16
EPS = 1e-6

D_MODEL = 1024
D_FF = 2816
HG_HEADS, HG_DK = 4, 128
D_HG = HG_HEADS * HG_DK
SSD_HEADS, SSD_HEADDIM, SSD_GROUPS, SSD_STATE, SSD_CONV = 8, 64, 2, 128, 4
D_SSD = SSD_HEADS * SSD_HEADDIM
D_XBC = D_SSD + 2 * SSD_GROUPS * SSD_STATE
S5_GROUPS, S5_GROUP_CH, S5_STATE = 32, 16, 64
D_S5 = S5_GROUPS * S5_GROUP_CH
D_MIX = D_HG + D_SSD + D_S5
CHUNK = 64
S5_T = 16
LANE = 128
SUBLANE = 8

COL_QFIG, COL_XBC, COL_Z, COL_U, COL_DT = 0, 2048, 3072, 3584, 4096
D_PROJ = COL_DT + LANE
VMEM_LIMIT = 56 * 1024 * 1024
NEG_BIG = -1e30


def _dot(a, b):
    return jnp.dot(a, b, preferred_element_type=F32)


def _dot_nt(a, b):
    return lax.dot_general(a, b, (((1,), (1,)), ((), ())), preferred_element_type=F32)


def _dot_tn(a, b):
    return lax.dot_general(a, b, (((0,), (0,)), ((), ())), preferred_element_type=F32)


def _dot_split(m, x):
    hi = x.astype(BF16)
    lo = (x - hi.astype(F32)).astype(BF16)
    return _dot(m, hi) + _dot(m, lo)


def _rms(x, g):
    return x * lax.rsqrt(jnp.mean(x * x, axis=-1, keepdims=True) + EPS) * g


def _silu(x):
    return x * jax.nn.sigmoid(x)


def _params(sem):
    return pltpu.CompilerParams(dimension_semantics=sem, vmem_limit_bytes=VMEM_LIMIT)


def _ffn_body(x_ref, gpre_ref, wg_ref, wu_ref, wd_ref, gpost_ref, o_ref, xn_ref, acc_ref):
    f = pl.program_id(1)

    @pl.when(f == 0)
    def _():
        xn_ref[...] = _rms(x_ref[...], gpre_ref[...]).astype(BF16)

    xn = xn_ref[...]
    gate = _dot(xn, wg_ref[...])
    up = _dot(xn, wu_ref[...])
    act = (_silu(gate) * up).astype(BF16)
    part = _dot(act, wd_ref[...])

    @pl.when(f == 0)
    def _():
        acc_ref[...] = part

    @pl.when(f > 0)
    def _():
        acc_ref[...] += part

    @pl.when(f == pl.num_programs(1) - 1)
    def _():
        o_ref[...] = x_ref[...] + 0.5 * _rms(acc_ref[...], gpost_ref[...])


def _ffn(h, ng, wg, wu, wd, layer, which, tm, tf):
    n, d = h.shape
    nf = D_FF // tf
    kpre, kpost = (0, 1) if which == 0 else (4, 5)
    return pl.pallas_call(
        _ffn_body,
        grid=(n // tm, nf),
        in_specs=[
            pl.BlockSpec((tm, d), lambda i, f: (i, 0)),
            pl.BlockSpec((None, 1, d), lambda i, f: (layer * 6 + kpre, 0, 0)),
            pl.BlockSpec((None, None, d, tf), lambda i, f: (layer, which, 0, f)),
            pl.BlockSpec((None, None, d, tf), lambda i, f: (layer, which, 0, f)),
            pl.BlockSpec((None, None, tf, d), lambda i, f: (layer, which, f, 0)),
            pl.BlockSpec((None, 1, d), lambda i, f: (layer * 6 + kpost, 0, 0)),
        ],
        out_specs=pl.BlockSpec((tm, d), lambda i, f: (i, 0)),
        out_shape=jax.ShapeDtypeStruct((n, d), F32),
        scratch_shapes=[pltpu.VMEM((tm, d), BF16), pltpu.VMEM((tm, d), F32)],
        compiler_params=_params(("arbitrary", "arbitrary")),
        name="ffn",
    )(h, ng, wg, wu, wd, ng)


def _inproj_body(x_ref, g_ref, w_ref, o_ref, xn_ref):
    @pl.when(pl.program_id(1) == 0)
    def _():
        xn_ref[...] = _rms(x_ref[...], g_ref[...]).astype(BF16)

    o_ref[...] = _dot(xn_ref[...], w_ref[...])


def _inproj(h, ng, w_in_p, layer, tm, tn):
    n, d = h.shape
    return pl.pallas_call(
        _inproj_body,
        grid=(n // tm, D_PROJ // tn),
        in_specs=[
            pl.BlockSpec((tm, d), lambda i, j: (i, 0)),
            pl.BlockSpec((None, 1, d), lambda i, j: (layer * 6 + 2, 0, 0)),
            pl.BlockSpec((None, d, tn), lambda i, j: (layer, 0, j)),
        ],
        out_specs=pl.BlockSpec((tm, tn), lambda i, j: (i, j)),
        out_shape=jax.ShapeDtypeStruct((n, D_PROJ), F32),
        scratch_shapes=[pltpu.VMEM((tm, d), BF16)],
        compiler_params=_params(("arbitrary", "arbitrary")),
        name="inproj",
    )(h, ng, w_in_p)


HG_LEVELS = (32, 16)
HG_DIAG = SUBLANE


def _block_tri_up(m, n=CHUNK):
    t = np.arange(n)
    same = (t[:, None] // m) == (t[None, :] // m)
    tri = same & (t[None, :] <= t[:, None])
    up = same & (t[None, :] > t[:, None])
    return tri.astype(np.float32), up.astype(np.float32)


def _hgrn_consts():
    mats, masks = [], []
    for m in (CHUNK,) + HG_LEVELS + (HG_DIAG,):
        tri, up = _block_tri_up(m)
        mats += [tri, up]
    t = np.arange(CHUNK)
    for m in HG_LEVELS + (HG_DIAG,):
        bt, bs = t[:, None] // m, t[None, :] // m
        masks.append(((bt % 2 == 1) & (bs == bt - 1)).astype(np.float32))
    return jnp.asarray(np.concatenate(mats, 0), BF16), jnp.asarray(np.stack(masks, 0), F32)


def _hgrn_body(p_ref, cm_ref, mask_ref, loglb_ref, l1m_ref, omlb_ref, gn_ref, o_ref, st_ref, *, n_chunks):
    @pl.when(pl.program_id(1) == 0)
    def _():
        st_ref[...] = jnp.zeros_like(st_ref)

    cm = cm_ref[...]
    loglb, l1m, omlb, gn = loglb_ref[...], l1m_ref[...], omlb_ref[...], gn_ref[...]
    nlev = len(HG_LEVELS) + 1
    sub_iota = lax.broadcasted_iota(jnp.int32, (CHUNK // HG_DIAG, HG_DIAG, HG_DK), 1)

    def chunk_step(c, carry):
        rows = pl.ds(pl.multiple_of(c * CHUNK, CHUNK), CHUNK)
        q = _silu(p_ref[rows, 0:D_HG])
        fr = p_ref[rows, D_HG:2 * D_HG]
        v = p_ref[rows, 2 * D_HG:3 * D_HG]
        gr = p_ref[rows, 3 * D_HG:4 * D_HG]
        ls = jnp.minimum(fr, 0.0) - jnp.log1p(jnp.exp(-jnp.abs(fr)))
        bb = l1m + ls
        lf = jnp.maximum(loglb, bb) + jnp.log1p(jnp.exp(-jnp.abs(loglb - bb)))
        kk = omlb * jnp.exp(ls - fr)
        cs = _dot_split(cm, lf)
        outs = []
        for h in range(HG_HEADS):
            hs = slice(h * HG_DK, (h + 1) * HG_DK)
            qh, kh, vh = q[:, hs], kk[:, hs], v[:, hs]
            vb = vh.astype(BF16)
            st = st_ref[h]
            c64, r64 = cs[0:CHUNK, hs], cs[CHUNK:2 * CHUNK, hs]
            o = _dot_nt((qh * jnp.exp(c64)).astype(BF16), st.astype(BF16))
            scores = jnp.zeros((CHUNK, CHUNK), F32)
            for li in range(nlev):
                base = 2 * CHUNK * (li + 1)
                cmm, rmm = cs[base:base + CHUNK, hs], cs[base + CHUNK:base + 2 * CHUNK, hs]
                p = _dot_nt((qh * jnp.exp(cmm)).astype(BF16), (kh * jnp.exp(rmm)).astype(BF16))
                scores = scores + mask_ref[li] * p
            o = o + _dot(scores.astype(BF16), vb)
            shp = (CHUNK // HG_DIAG, HG_DIAG, HG_DK)
            c8 = cs[2 * CHUNK * nlev:2 * CHUNK * nlev + CHUNK, hs].reshape(shp)
            q3, k3, v3 = qh.reshape(shp), kh.reshape(shp), vh.reshape(shp)
            od = jnp.zeros(shp, F32)
            for j in range(HG_DIAG):
                e = jnp.where(sub_iota >= j, c8 - c8[:, j:j + 1, :], NEG_BIG)
                sc = jnp.sum(q3 * jnp.exp(e) * k3[:, j:j + 1, :], axis=-1, keepdims=True)
                od = od + sc * v3[:, j:j + 1, :]
            o = o + od.reshape(CHUNK, HG_DK)
            ke = (kh * jnp.exp(r64)).astype(BF16)
            st_ref[h] = jnp.exp(c64[CHUNK - 1:CHUNK, :]) * st + _dot_tn(vb, ke)
            outs.append(o * lax.rsqrt(jnp.mean(o * o, axis=-1, keepdims=True) + EPS) * gn)
        o_ref[rows, :] = jnp.concatenate(outs, axis=-1) * _silu(gr)
        return carry

    lax.fori_loop(0, n_chunks, chunk_step, 0)


def _hgrn(proj3, consts, loglb, l1m, omlb, gn, tl):
    b, l, _ = proj3.shape
    cm, masks = consts
    vec = pl.BlockSpec((1, D_HG), lambda bi, i: (0, 0))
    return pl.pallas_call(
        functools.partial(_hgrn_body, n_chunks=tl // CHUNK),
        grid=(b, l // tl),
        in_specs=[
            pl.BlockSpec((None, tl, 4 * D_HG), lambda bi, i: (bi, i, 0)),
            pl.BlockSpec(cm.shape, lambda bi, i: (0, 0)),
            pl.BlockSpec(masks.shape, lambda bi, i: (0, 0, 0)),
            vec, vec, vec,
            pl.BlockSpec((1, HG_DK), lambda bi, i: (0, 0)),
        ],
        out_specs=pl.BlockSpec((None, tl, D_HG), lambda bi, i: (bi, i, 0)),
        out_shape=jax.ShapeDtypeStruct((b, l, D_HG), F32),
        scratch_shapes=[pltpu.VMEM((HG_HEADS, HG_DK, HG_DK), F32)],
        compiler_params=_params(("arbitrary", "arbitrary")),
        name="hgrn2",
    )(proj3, cm, masks, loglb, l1m, omlb, gn)


def _ssd_consts():
    tri, up = _block_tri_up(CHUNK)
    t = np.arange(CHUNK)
    su = (t[:, None] > t[None, :]).astype(np.float32)
    causal = (t[:, None] >= t[None, :]).astype(np.float32)
    expand = np.zeros((LANE, D_SSD), np.float32)
    for h in range(SSD_HEADS):
        expand[h, h * SSD_HEADDIM:(h + 1) * SSD_HEADDIM] = 1.0
    return (jnp.asarray(np.concatenate([tri, up], 0), BF16), jnp.asarray(np.tile(su, (1, SSD_HEADS)), F32),
            jnp.asarray(causal, F32), jnp.asarray(expand, BF16))


def _ssd_body(xbc_ref, z_ref, dt_ref, cw_ref, cb_ref, dtb_ref, a_ref, d_ref, nw_ref, tu_ref, su_ref, causal_ref,
              ex_ref, o_ref, st_ref, ext_ref, xs_ref, dtr_ref, *, tl):
    pad = SUBLANE

    @pl.when(pl.program_id(1) == 0)
    def _():
        st_ref[...] = jnp.zeros_like(st_ref)
        ext_ref[0:pad, :] = jnp.zeros((pad, D_XBC), F32)

    ext_ref[pad:pad + tl, :] = xbc_ref[...]
    conv = jnp.broadcast_to(cb_ref[...], (tl, D_XBC))
    for j in range(SSD_CONV):
        conv = conv + cw_ref[j:j + 1, :] * ext_ref[pl.ds(pad - (SSD_CONV - 1) + j, tl), :]
    ext_ref[0:pad, :] = xbc_ref[tl - pad:tl, :]
    xs_ref[...] = _silu(conv)
    dtin = dt_ref[...] + dtb_ref[...]
    sp = jnp.maximum(dtin, 0.0) + jnp.log1p(jnp.exp(-jnp.abs(dtin)))
    dtr_ref[...] = _dot_split_r(sp, ex_ref[...])

    tu, su, causal = tu_ref[...], su_ref[...], causal_ref[...]
    a_rep, d_rep, nw = a_ref[...], d_ref[...], nw_ref[...]
    gw = D_SSD // SSD_GROUPS
    hpg = SSD_HEADS // SSD_GROUPS

    def chunk_step(c, carry):
        rows = pl.ds(pl.multiple_of(c * CHUNK, CHUNK), CHUNK)
        x = xs_ref[rows, 0:D_SSD]
        dtr = dtr_ref[rows, :]
        adt = dtr * a_rep
        r = _dot_split(tu, jnp.concatenate([adt, adt * su], axis=-1))
        acs, rs = r[0:CHUNK, 0:D_SSD], r[CHUNK:2 * CHUNK, 0:D_SSD]
        dm = r[0:CHUNK, D_SSD:2 * D_SSD]
        xdt = x * dtr
        xdt_b = xdt.astype(BF16)
        ys = []
        for g in range(SSD_GROUPS):
            gs = slice(g * gw, (g + 1) * gw)
            bg = xs_ref[rows, D_SSD + g * SSD_STATE:D_SSD + (g + 1) * SSD_STATE].astype(BF16)
            cg = xs_ref[rows, D_SSD + (SSD_GROUPS + g) * SSD_STATE:
                        D_SSD + (SSD_GROUPS + g + 1) * SSD_STATE].astype(BF16)
            st = st_ref[g]
            scores = _dot_nt(cg, bg)
            y_off = _dot(cg, st.astype(BF16)) * jnp.exp(acs[:, gs])
            yd = []
            for hh in range(hpg):
                sl = slice((g * hpg + hh) * SSD_HEADDIM, (g * hpg + hh + 1) * SSD_HEADDIM)
                w = (scores * causal * jnp.exp(dm[:, sl])).astype(BF16)
                yd.append(_dot(w, xdt_b[:, sl]))
            ys.append(jnp.concatenate(yd, axis=-1) + y_off)
            xd = (xdt[:, gs] * jnp.exp(rs[:, gs])).astype(BF16)
            st_ref[g] = jnp.exp(acs[CHUNK - 1:CHUNK, gs]) * st + _dot_tn(bg, xd)
        y = jnp.concatenate(ys, axis=-1) + x * d_rep
        y = y * _silu(z_ref[rows, :])
        yn = [y[:, g * gw:(g + 1) * gw] * lax.rsqrt(
            jnp.mean(y[:, g * gw:(g + 1) * gw] * y[:, g * gw:(g + 1) * gw], axis=-1, keepdims=True) + EPS)
            for g in range(SSD_GROUPS)]
        o_ref[rows, :] = jnp.concatenate(yn, axis=-1) * nw
        return carry

    lax.fori_loop(0, tl // CHUNK, chunk_step, 0)


def _dot_split_r(x, m):
    hi = x.astype(BF16)
    lo = (x - hi.astype(F32)).astype(BF16)
    return _dot(hi, m) + _dot(lo, m)


def _ssd(proj3, consts, cw, cb, dtb, a_rep, d_rep, nw, tl):
    b, l, _ = proj3.shape
    tu, su, causal, expand = consts
    full = lambda a: pl.BlockSpec(a.shape, lambda bi, i: (0,) * a.ndim)
    return pl.pallas_call(
        functools.partial(_ssd_body, tl=tl),
        grid=(b, l // tl),
        in_specs=[
            pl.BlockSpec((None, tl, D_XBC), lambda bi, i: (bi, i, COL_XBC // D_XBC)),
            pl.BlockSpec((None, tl, D_SSD), lambda bi, i: (bi, i, COL_Z // D_SSD)),
            pl.BlockSpec((None, tl, LANE), lambda bi, i: (bi, i, COL_DT // LANE)),
            full(cw), full(cb), full(dtb), full(a_rep), full(d_rep), full(nw),
            full(tu), full(su), full(causal), full(expand),
        ],
        out_specs=pl.BlockSpec((None, tl, D_SSD), lambda bi, i: (bi, i, 0)),
        out_shape=jax.ShapeDtypeStruct((b, l, D_SSD), F32),
        scratch_shapes=[
            pltpu.VMEM((SSD_GROUPS, SSD_STATE, D_SSD // SSD_GROUPS), F32),
            pltpu.VMEM((tl + SUBLANE, D_XBC), F32),
            pltpu.VMEM((tl, D_XBC), F32),
            pltpu.VMEM((tl, D_SSD), F32),
        ],
        compiler_params=_params(("arbitrary", "arbitrary")),
        name="ssd",
    )(proj3, proj3, proj3, cw, cb, dtb, a_rep, d_rep, nw, tu, su, causal, expand)


def _s5_consts(a_re, a_im, b_re, b_im, c_re, c_im, log_dt):
    t_ = S5_T
    hp = lax.Precision.HIGHEST
    ar, ai = a_re.astype(F32), a_im.astype(F32)
    delta = jnp.exp(log_dt.astype(F32))[:, None]
    mag = jnp.exp(ar * delta)
    ab_re, ab_im = mag * jnp.cos(ai * delta), mag * jnp.sin(ai * delta)
    den = ar * ar + ai * ai
    nr, ni = ab_re - 1.0, ab_im
    fr = (nr * ar + ni * ai) / den
    fi = (ni * ar - nr * ai) / den
    br, bi = b_re.astype(F32), b_im.astype(F32)
    bb_re = fr[..., None] * br - fi[..., None] * bi
    bb_im = fr[..., None] * bi + fi[..., None] * br
    tau = jnp.arange(t_ + 1, dtype=F32)[:, None, None]
    pmag = jnp.exp(tau * (ar * delta))
    pr, pi = pmag * jnp.cos(tau * (ai * delta)), pmag * jnp.sin(tau * (ai * delta))
    abr = pr[..., None] * bb_re - pi[..., None] * bb_im
    abi = pr[..., None] * bb_im + pi[..., None] * bb_re
    cr, ci = c_re.astype(F32), c_im.astype(F32)
    kern = (jnp.einsum('gdp,tgpc->tgdc', cr, abr[:t_], precision=hp)
            - jnp.einsum('gdp,tgpc->tgdc', ci, abi[:t_], precision=hp))
    jj, ii = np.arange(t_)[:, None], np.arange(t_)[None, :]
    kt = kern[np.clip(ii - jj, 0, t_ - 1)] * jnp.asarray(ii >= jj, F32)[:, :, None, None, None]
    toep = kt.transpose(2, 0, 4, 1, 3).reshape(S5_GROUPS, t_ * S5_GROUP_CH, t_ * S5_GROUP_CH)
    rev = np.arange(t_ - 1, -1, -1)
    wr = abr[rev].transpose(1, 0, 3, 2).reshape(S5_GROUPS, t_ * S5_GROUP_CH, S5_STATE)
    wi = abi[rev].transpose(1, 0, 3, 2).reshape(S5_GROUPS, t_ * S5_GROUP_CH, S5_STATE)
    wz = jnp.concatenate([wr, wi], axis=-1)
    p1r, p1i = pr[1:], pi[1:]
    vr = cr[None] * p1r[:, :, None, :] - ci[None] * p1i[:, :, None, :]
    vi = -(cr[None] * p1i[:, :, None, :] + ci[None] * p1r[:, :, None, :])
    vr = vr.transpose(1, 3, 0, 2).reshape(S5_GROUPS, S5_STATE, t_ * S5_GROUP_CH)
    vi = vi.transpose(1, 3, 0, 2).reshape(S5_GROUPS, S5_STATE, t_ * S5_GROUP_CH)
    a_t = jnp.stack([pr[t_].reshape(1, -1), pi[t_].reshape(1, -1)], 0)
    return toep.astype(BF16), wz.astype(BF16), vr.astype(BF16), vi.astype(BF16), a_t


def _s5_body(u_ref, toep_ref, wz_ref, vr_ref, vi_ref, at_ref, o_ref, zr_ref, zi_ref, xr_ref, xi_ref, sr_ref, si_ref,
             *, n_rows):
    @pl.when(pl.program_id(1) == 0)
    def _():
        sr_ref[...] = jnp.zeros_like(sr_ref)
        si_ref[...] = jnp.zeros_like(si_ref)

    for g in range(S5_GROUPS):
        z = _dot(u_ref[g].astype(BF16), wz_ref[g])
        zr_ref[:, g * S5_STATE:(g + 1) * S5_STATE] = z[:, 0:S5_STATE]
        zi_ref[:, g * S5_STATE:(g + 1) * S5_STATE] = z[:, S5_STATE:2 * S5_STATE]

    ar, ai = at_ref[0], at_ref[1]

    def step(c, carry):
        xr, xi = carry
        row = pl.ds(c, 1)
        xr_ref[row, :] = xr
        xi_ref[row, :] = xi
        return (xr * ar - xi * ai + zr_ref[row, :], xr * ai + xi * ar + zi_ref[row, :])

    xr, xi = lax.fori_loop(0, n_rows, step, (sr_ref[0:1, :], si_ref[0:1, :]))
    sr_ref[0:1, :] = xr
    si_ref[0:1, :] = xi

    for g in range(S5_GROUPS):
        gs = slice(g * S5_STATE, (g + 1) * S5_STATE)
        o_ref[g] = (_dot(u_ref[g].astype(BF16), toep_ref[g])
                    + _dot(xr_ref[:, gs].astype(BF16), vr_ref[g])
                    + _dot(xi_ref[:, gs].astype(BF16), vi_ref[g]))


def _s5(ur, consts, batch, n_rows):
    toep, wz, vr, vi, a_t = consts
    g, rows_total, w = ur.shape
    per_batch = rows_total // batch // n_rows
    full = lambda a: pl.BlockSpec(a.shape, lambda bi, i: (0,) * a.ndim)
    gp = S5_GROUPS * S5_STATE
    return pl.pallas_call(
        functools.partial(_s5_body, n_rows=n_rows),
        grid=(batch, per_batch),
        in_specs=[pl.BlockSpec((g, n_rows, w), lambda bi, i: (0, bi * per_batch + i, 0)),
                  full(toep), full(wz), full(vr), full(vi), full(a_t)],
        out_specs=pl.BlockSpec((g, n_rows, w), lambda bi, i: (0, bi * per_batch + i, 0)),
        out_shape=jax.ShapeDtypeStruct(ur.shape, F32),
        scratch_shapes=[pltpu.VMEM((n_rows, gp), F32)] * 4 + [pltpu.VMEM((SUBLANE, gp), F32)] * 2,
        compiler_params=_params(("arbitrary", "arbitrary")),
        name="s5",
    )(ur, toep, wz, vr, vi, a_t)


def _outproj_body(h_ref, oa_ref, ob_ref, yc_ref, u_ref, d_ref, gw_ref, gb_ref, wo_ref, g_ref, o_ref):
    y = yc_ref[...] + d_ref[...] * u_ref[...]
    y = jax.nn.gelu(y)
    oc = y * jax.nn.sigmoid(_dot(y.astype(BF16), gw_ref[...]) + gb_ref[...])
    mix = (_dot(oa_ref[...].astype(BF16), wo_ref[0:D_HG, :])
           + _dot(ob_ref[...].astype(BF16), wo_ref[D_HG:D_HG + D_SSD, :])
           + _dot(oc.astype(BF16), wo_ref[D_HG + D_SSD:D_MIX, :]))
    o_ref[...] = h_ref[...] + _rms(mix, g_ref[...])


def _outproj(h, oa, ob, yc, proj, s5_d, glu_w, glu_b, w_out, ng, layer, tm):
    n, d = h.shape
    tok = lambda w: pl.BlockSpec((tm, w), lambda i: (i, 0))
    return pl.pallas_call(
        _outproj_body,
        grid=(n // tm,),
        in_specs=[
            tok(d), tok(D_HG), tok(D_SSD), tok(D_S5),
            pl.BlockSpec((tm, D_S5), lambda i: (i, COL_U // D_S5)),
            pl.BlockSpec((None, 1, D_S5), lambda i: (layer, 0, 0)),
            pl.BlockSpec((None, D_S5, D_S5), lambda i: (layer, 0, 0)),
            pl.BlockSpec((None, 1, D_S5), lambda i: (layer, 0, 0)),
            pl.BlockSpec((None, D_MIX, d), lambda i: (layer, 0, 0)),
            pl.BlockSpec((None, 1, d), lambda i: (layer * 6 + 3, 0, 0)),
        ],
        out_specs=tok(d),
        out_shape=jax.ShapeDtypeStruct((n, d), F32),
        compiler_params=_params(("arbitrary",)),
        name="outproj",
    )(h, oa, ob, yc, proj, s5_d, glu_w, glu_b, w_out, ng)


def _tile(n, pref):
    return pref if n % pref == 0 else n


def kernel(x, norm_g, ffn_w_gate, ffn_w_up, ffn_w_down, w_in, w_out, hg_lb_logits, hg_gnorm, ssd_conv_w, ssd_conv_b,
           ssd_dt_bias, ssd_A_log, ssd_D, ssd_norm, s5_A_re, s5_A_im, s5_B_re, s5_B_im, s5_C_re, s5_C_im, s5_D,
           s5_log_dt, s5_glu_w, s5_glu_b):
    bsz, seq, d = x.shape
    depth = norm_g.shape[0]
    n = bsz * seq
    assert d == D_MODEL and seq % CHUNK == 0 and w_in.shape[-1] == COL_DT + SSD_HEADS

    ng = norm_g.astype(F32).reshape(depth * 6, 1, d)
    wg, wu, wd = ffn_w_gate.astype(BF16), ffn_w_up.astype(BF16), ffn_w_down.astype(BF16)
    o = 4 * D_HG
    w_in_p = jnp.concatenate([
        w_in[..., 0:o], w_in[..., o + D_SSD:o + D_SSD + D_XBC], w_in[..., o:o + D_SSD],
        w_in[..., o + D_SSD + D_XBC + SSD_HEADS:], w_in[..., o + D_SSD + D_XBC:o + D_SSD + D_XBC + SSD_HEADS],
        jnp.zeros(w_in.shape[:-1] + (LANE - SSD_HEADS,), w_in.dtype)], axis=-1).astype(BF16)
    w_out_b = w_out.astype(BF16)
    glu_w_b = s5_glu_w.astype(BF16)
    lb = jnp.cumsum(jax.nn.softmax(hg_lb_logits.astype(F32), axis=0), axis=0)
    lb = lb - lb[:1]
    hg_c = _hgrn_consts()
    ssd_c = _ssd_consts()

    tm = _tile(n, 512)
    tl_h = _tile(seq, 512)
    tl_s = _tile(seq, 512)
    s5_rows = _tile(seq // S5_T, 128)

    h = x.astype(F32).reshape(n, d)
    for l in range(depth):
        h = _ffn(h, ng, wg, wu, wd, l, 0, tm, D_FF // 2)
        proj = _inproj(h, ng, w_in_p, l, tm, D_PROJ // 3)
        proj3 = proj.reshape(bsz, seq, D_PROJ)
        lbl = lb[l][None, :]
        oa = _hgrn(proj3, hg_c, jnp.log(lbl), jnp.log1p(-lbl), 1.0 - lbl, hg_gnorm[l].astype(F32)[None, :], tl_h)
        a_rep = jnp.repeat(-jnp.exp(ssd_A_log[l].astype(F32)), SSD_HEADDIM)[None, :]
        d_rep = jnp.repeat(ssd_D[l].astype(F32), SSD_HEADDIM)[None, :]
        dtb = jnp.pad(ssd_dt_bias[l].astype(F32), (0, LANE - SSD_HEADS))[None, :]
        ob = _ssd(proj3, ssd_c, ssd_conv_w[l].astype(F32), ssd_conv_b[l].astype(F32)[None, :], dtb, a_rep, d_rep,
                  ssd_norm[l].astype(F32)[None, :], tl_s)
        u = proj3[:, :, COL_U:COL_U + D_S5]
        ur = u.reshape(bsz, seq // S5_T, S5_T, S5_GROUPS, S5_GROUP_CH).transpose(3, 0, 1, 2, 4)
        ur = ur.reshape(S5_GROUPS, n // S5_T, S5_T * S5_GROUP_CH)
        s5_c = _s5_consts(s5_A_re[l], s5_A_im[l], s5_B_re[l], s5_B_im[l], s5_C_re[l], s5_C_im[l], s5_log_dt[l])
        yr = _s5(ur, s5_c, bsz, s5_rows)
        yc = yr.reshape(S5_GROUPS, bsz, seq // S5_T, S5_T, S5_GROUP_CH).transpose(1, 2, 3, 0, 4).reshape(n, D_S5)
        h = _outproj(h, oa.reshape(n, D_HG), ob.reshape(n, D_SSD), yc, proj, s5_D.astype(F32)[:, None, :], glu_w_b,
                     s5_glu_b.astype(F32)[:, None, :], w_out_b, ng, l, tm)
        h = _ffn(h, ng, wg, wu, wd, l, 1, tm, D_FF // 2)
    return h.reshape(bsz, seq, d).astype(x.dtype)
```

```python
import functools

import numpy as np
import jax
import jax.numpy as jnp
from jax import lax
from jax.experimental import pallas as pl
from jax.experimental.pallas import tpu as pltpu

F32 = jnp.float32
BF16 = jnp.bfloat16
EPS = 1e-6

D_MODEL = 1024
D_FF = 2816
HG_HEADS, HG_DK = 4, 128
D_HG = HG_HEADS * HG_DK
SSD_HEADS, SSD_HEADDIM, SSD_GROUPS, SSD_STATE, SSD_CONV = 8, 64, 2, 128, 4
D_SSD = SSD_HEADS * SSD_HEADDIM
D_XBC = D_SSD + 2 * SSD_GROUPS * SSD_STATE
S5_GROUPS, S5_GROUP_CH, S5_STATE = 32, 16, 64
D_S5 = S5_GROUPS * S5_GROUP_CH
D_MIX = D_HG + D_SSD + D_S5
CHUNK = 64
LANE = 128
SUBLANE = 8
S5_T = 16
S5_LT = D_S5 // LANE
S5_GPT = LANE // S5_GROUP_CH
S5_SW = S5_GPT * S5_STATE

COL_XBC, COL_Z, COL_U, COL_DT = 4 * D_HG, 4 * D_HG + D_XBC, 4 * D_HG + D_XBC + D_SSD, 4 * D_HG + D_XBC + D_SSD + D_S5
D_PROJ = COL_DT + LANE
VMEM_LIMIT = 56 * 1024 * 1024


def _dot(a, b):
    return jnp.dot(a, b, preferred_element_type=F32)


def _dot_nt(a, b):
    return lax.dot_general(a, b, (((1,), (1,)), ((), ())), preferred_element_type=F32)


def _dot_tn(a, b):
    return lax.dot_general(a, b, (((0,), (0,)), ((), ())), preferred_element_type=F32)


def _split(x):
    hi = x.astype(BF16)
    return hi, (x - hi.astype(F32)).astype(BF16)


def _dot_split(m2, x):
    hi, lo = _split(x)
    return _dot(m2, jnp.concatenate([hi, lo], axis=0))


def _dot_split_r(x, m2):
    hi, lo = _split(x)
    return _dot(jnp.concatenate([hi, lo], axis=1), m2)


def _rms(x, g):
    return x * lax.rsqrt(jnp.mean(x * x, axis=-1, keepdims=True) + EPS) * g


def _silu(x):
    return x * jax.nn.sigmoid(x)


def _params(sem):
    return pltpu.CompilerParams(dimension_semantics=sem, vmem_limit_bytes=VMEM_LIMIT)


def _full(a):
    return pl.BlockSpec(a.shape, lambda *_: (0,) * a.ndim)


def _ffn_body(x_ref, gpre_ref, wg_ref, wu_ref, wd_ref, gpost_ref, o_ref, xn_ref, acc_ref):
    f = pl.program_id(1)

    @pl.when(f == 0)
    def _():
        xn_ref[...] = _rms(x_ref[...], gpre_ref[...]).astype(BF16)

    xn = xn_ref[...]
    gate = _dot(xn, wg_ref[...])
    up = _dot(xn, wu_ref[...])
    act = (_silu(gate) * up).astype(BF16)
    part = _dot(act, wd_ref[...])

    @pl.when(f == 0)
    def _():
        acc_ref[...] = part

    @pl.when(f > 0)
    def _():
        acc_ref[...] += part

    @pl.when(f == pl.num_programs(1) - 1)
    def _():
        o_ref[...] = x_ref[...] + 0.5 * _rms(acc_ref[...], gpost_ref[...])


def _ffn(h, ng, wg, wu, wd, layer, which, tm, tf):
    n, d = h.shape
    kpre, kpost = (0, 1) if which == 0 else (4, 5)
    return pl.pallas_call(
        _ffn_body,
        grid=(n // tm, D_FF // tf),
        in_specs=[
            pl.BlockSpec((tm, d), lambda i, f: (i, 0)),
            pl.BlockSpec((None, 1, d), lambda i, f: (layer * 6 + kpre, 0, 0)),
            pl.BlockSpec((None, None, d, tf), lambda i, f: (layer, which, 0, f)),
            pl.BlockSpec((None, None, d, tf), lambda i, f: (layer, which, 0, f)),
            pl.BlockSpec((None, None, tf, d), lambda i, f: (layer, which, f, 0)),
            pl.BlockSpec((None, 1, d), lambda i, f: (layer * 6 + kpost, 0, 0)),
        ],
        out_specs=pl.BlockSpec((tm, d), lambda i, f: (i, 0)),
        out_shape=jax.ShapeDtypeStruct((n, d), F32),
        scratch_shapes=[pltpu.VMEM((tm, d), BF16), pltpu.VMEM((tm, d), F32)],
        compiler_params=_params(("arbitrary", "arbitrary")),
        name="ffn",
    )(h, ng, wg, wu, wd, ng)


def _inproj_body(x_ref, g_ref, w_ref, qfig_ref, xbc_ref, z_ref, u_ref, dt_ref):
    xn = _rms(x_ref[...], g_ref[...]).astype(BF16)
    qfig_ref[...] = _dot(xn, w_ref[:, 0:COL_XBC])
    xbc_ref[...] = _dot(xn, w_ref[:, COL_XBC:COL_Z])
    z_ref[...] = _dot(xn, w_ref[:, COL_Z:COL_U])
    u = _dot(xn, w_ref[:, COL_U:COL_DT])
    for k in range(S5_LT):
        u_ref[k] = u[:, k * LANE:(k + 1) * LANE]
    dt_ref[...] = _dot(xn, w_ref[:, COL_DT:D_PROJ])


def _inproj(h, ng, w_in_p, layer, tm):
    n, d = h.shape
    tok = lambda w: pl.BlockSpec((tm, w), lambda i: (i, 0))
    out_w = (4 * D_HG, D_XBC, D_SSD)
    return pl.pallas_call(
        _inproj_body,
        grid=(n // tm,),
        in_specs=[
            tok(d),
            pl.BlockSpec((None, 1, d), lambda i: (layer * 6 + 2, 0, 0)),
            pl.BlockSpec((None, d, D_PROJ), lambda i: (layer, 0, 0)),
        ],
        out_specs=[tok(w) for w in out_w] + [pl.BlockSpec((S5_LT, tm, LANE), lambda i: (0, i, 0)), tok(LANE)],
        out_shape=[jax.ShapeDtypeStruct((n, w), F32) for w in out_w]
        + [jax.ShapeDtypeStruct((S5_LT, n, LANE), F32), jax.ShapeDtypeStruct((n, LANE), F32)],
        compiler_params=_params(("arbitrary",)),
        name="inproj",
    )(h, ng, w_in_p)


HG_LEVELS = (32, 16, 8, 4, 2)


def _block_tri_up(m, n=CHUNK):
    t = np.arange(n)
    same = (t[:, None] // m) == (t[None, :] // m)
    tri = same & (t[None, :] <= t[:, None])
    up = same & (t[None, :] > t[:, None])
    return tri.astype(np.float32), up.astype(np.float32)


def _hgrn_consts():
    mats, masks = [], []
    for m in (CHUNK,) + HG_LEVELS:
        mats += list(_block_tri_up(m))
    t = np.arange(CHUNK)
    for m in HG_LEVELS + (1,):
        bt, bs = t[:, None] // m, t[None, :] // m
        masks.append(((bt % 2 == 1) & (bs == bt - 1)).astype(np.float32))
    masks.append(np.eye(CHUNK, dtype=np.float32))
    cm = np.concatenate(mats, 0)
    return jnp.asarray(np.concatenate([cm, cm], 1), BF16), jnp.asarray(np.stack(masks, 0), F32)


def _hgrn_body(p_ref, cm_ref, mask_ref, loglb_ref, l1m_ref, omlb_ref, gn_ref, o_ref, st_ref, *, n_chunks):
    @pl.when(pl.program_id(1) == 0)
    def _():
        st_ref[...] = jnp.zeros_like(st_ref)

    cm = cm_ref[...]
    loglb, l1m, omlb, gn = loglb_ref[...], l1m_ref[...], omlb_ref[...], gn_ref[...]
    nl = len(HG_LEVELS)

    def chunk_step(c, carry):
        rows = pl.ds(pl.multiple_of(c * CHUNK, CHUNK), CHUNK)
        q = _silu(p_ref[rows, 0:D_HG])
        fr = p_ref[rows, D_HG:2 * D_HG]
        v = p_ref[rows, 2 * D_HG:3 * D_HG].astype(BF16)
        gr = p_ref[rows, 3 * D_HG:4 * D_HG]
        ls = jnp.minimum(fr, 0.0) - jnp.log1p(jnp.exp(-jnp.abs(fr)))
        bb = l1m + ls
        lf = jnp.maximum(loglb, bb) + jnp.log1p(jnp.exp(-jnp.abs(loglb - bb)))
        kk = omlb * jnp.exp(ls - fr)
        cs = _dot_split(cm, lf)
        qb, kb = q.astype(BF16), kk.astype(BF16)
        qf = (q * jnp.exp(lf)).astype(BF16)
        outs = []
        for h in range(HG_HEADS):
            hs = slice(h * HG_DK, (h + 1) * HG_DK)
            qh, kh = q[:, hs], kk[:, hs]
            st = st_ref[h]
            c64, r64 = cs[0:CHUNK, hs], cs[CHUNK:2 * CHUNK, hs]
            scores = mask_ref[nl + 1] * _dot_nt(qb[:, hs], kb[:, hs]) + mask_ref[nl] * _dot_nt(qf[:, hs], kb[:, hs])
            for li in range(nl):
                base = 2 * CHUNK * (li + 1)
                cmm, rmm = cs[base:base + CHUNK, hs], cs[base + CHUNK:base + 2 * CHUNK, hs]
                p = _dot_nt((qh * jnp.exp(cmm)).astype(BF16), (kh * jnp.exp(rmm)).astype(BF16))
                scores = scores + mask_ref[li] * p
            o = _dot_nt((qh * jnp.exp(c64)).astype(BF16), st.astype(BF16)) + _dot(scores.astype(BF16), v[:, hs])
            ke = (kh * jnp.exp(r64)).astype(BF16)
            st_ref[h] = jnp.exp(c64[CHUNK - 1:CHUNK, :]) * st + _dot_tn(v[:, hs], ke)
            outs.append(o * lax.rsqrt(jnp.mean(o * o, axis=-1, keepdims=True) + EPS) * gn)
        o_ref[rows, :] = jnp.concatenate(outs, axis=-1) * _silu(gr)
        return carry

    lax.fori_loop(0, n_chunks, chunk_step, 0, unroll=2)


def _hgrn(qfig3, consts, loglb, l1m, omlb, gn, tl):
    b, l, _ = qfig3.shape
    cm, masks = consts
    return pl.pallas_call(
        functools.partial(_hgrn_body, n_chunks=tl // CHUNK),
        grid=(b, l // tl),
        in_specs=[pl.BlockSpec((None, tl, 4 * D_HG), lambda bi, i: (bi, i, 0)),
                  _full(cm), _full(masks), _full(loglb), _full(l1m), _full(omlb), _full(gn)],
        out_specs=pl.BlockSpec((None, tl, D_HG), lambda bi, i: (bi, i, 0)),
        out_shape=jax.ShapeDtypeStruct((b, l, D_HG), F32),
        scratch_shapes=[pltpu.VMEM((HG_HEADS, HG_DK, HG_DK), F32)],
        compiler_params=_params(("arbitrary", "arbitrary")),
        name="hgrn2",
    )(qfig3, cm, masks, loglb, l1m, omlb, gn)


def _ssd_consts():
    tri, up = _block_tri_up(CHUNK)
    t = np.arange(CHUNK)
    su = (t[:, None] > t[None, :]).astype(np.float32)
    causal = (t[:, None] >= t[None, :]).astype(np.float32)
    expand = np.zeros((LANE, D_SSD), np.float32)
    for h in range(SSD_HEADS):
        expand[h, h * SSD_HEADDIM:(h + 1) * SSD_HEADDIM] = 1.0
    tu = np.concatenate([tri, up], 0)
    return (jnp.asarray(np.concatenate([tu, tu], 1), BF16), jnp.asarray(np.tile(su, (1, SSD_HEADS)), F32),
            jnp.asarray(causal, F32), jnp.asarray(np.concatenate([expand, expand], 0), BF16))


def _ssd_body(xbc_ref, z_ref, dt_ref, cw_ref, cb_ref, dtb_ref, a_ref, d_ref, nw_ref, tu_ref, su_ref, causal_ref,
              ex_ref, o_ref, st_ref, ext_ref, *, tl):
    pad = SUBLANE

    @pl.when(pl.program_id(1) == 0)
    def _():
        st_ref[...] = jnp.zeros_like(st_ref)
        ext_ref[0:pad, :] = jnp.zeros((pad, D_XBC), F32)

    ext_ref[pad:pad + tl, :] = xbc_ref[...]

    tu, su, causal, expand = tu_ref[...], su_ref[...], causal_ref[...], ex_ref[...]
    a_rep, d_rep, nw, cb, dtb = a_ref[...], d_ref[...], nw_ref[...], cb_ref[...], dtb_ref[...]
    gw = D_SSD // SSD_GROUPS
    hpg = SSD_HEADS // SSD_GROUPS

    def chunk_step(c, carry):
        r0 = pl.multiple_of(c * CHUNK, CHUNK)
        rows = pl.ds(r0, CHUNK)
        win = ext_ref[pl.ds(r0, CHUNK + pad), :]
        conv = cb
        for j in range(SSD_CONV):
            lo = pad - (SSD_CONV - 1) + j
            conv = conv + cw_ref[j:j + 1, :] * win[lo:lo + CHUNK, :]
        xs = _silu(conv)
        x = xs[:, 0:D_SSD]
        dtin = dt_ref[rows, :] + dtb
        sp = jnp.maximum(dtin, 0.0) + jnp.log1p(jnp.exp(-jnp.abs(dtin)))
        dtr = _dot_split_r(sp, expand)
        adt = dtr * a_rep
        r = _dot_split(tu, jnp.concatenate([adt, adt * su], axis=-1))
        acs, rs = r[0:CHUNK, 0:D_SSD], r[CHUNK:2 * CHUNK, 0:D_SSD]
        dm = r[0:CHUNK, D_SSD:2 * D_SSD]
        xdt = x * dtr
        xdt_b = xdt.astype(BF16)
        ys = []
        for g in range(SSD_GROUPS):
            gs = slice(g * gw, (g + 1) * gw)
            bg = xs[:, D_SSD + g * SSD_STATE:D_SSD + (g + 1) * SSD_STATE].astype(BF16)
            cg = xs[:, D_SSD + (SSD_GROUPS + g) * SSD_STATE:D_SSD + (SSD_GROUPS + g + 1) * SSD_STATE].astype(BF16)
            st = st_ref[g]
            scores = _dot_nt(cg, bg) * causal
            y_off = _dot(cg, st.astype(BF16)) * jnp.exp(acs[:, gs])
            yd = []
            for hh in range(hpg):
                sl = slice((g * hpg + hh) * SSD_HEADDIM, (g * hpg + hh + 1) * SSD_HEADDIM)
                yd.append(_dot((scores * jnp.exp(dm[:, sl])).astype(BF16), xdt_b[:, sl]))
            ys.append(jnp.concatenate(yd, axis=-1) + y_off)
            xd = (xdt[:, gs] * jnp.exp(rs[:, gs])).astype(BF16)
            st_ref[g] = jnp.exp(acs[CHUNK - 1:CHUNK, gs]) * st + _dot_tn(bg, xd)
        y = jnp.concatenate(ys, axis=-1) + x * d_rep
        y = y * _silu(z_ref[rows, :])
        yn = [y[:, g * gw:(g + 1) * gw] * lax.rsqrt(
            jnp.mean(y[:, g * gw:(g + 1) * gw] * y[:, g * gw:(g + 1) * gw], axis=-1, keepdims=True) + EPS)
            for g in range(SSD_GROUPS)]
        o_ref[rows, :] = jnp.concatenate(yn, axis=-1) * nw
        return carry

    lax.fori_loop(0, tl // CHUNK, chunk_step, 0, unroll=2)
    ext_ref[0:pad, :] = xbc_ref[tl - pad:tl, :]


def _ssd(xbc3, z3, dt3, consts, cw, cb, dtb, a_rep, d_rep, nw, tl):
    b, l, _ = xbc3.shape
    tok = lambda w: pl.BlockSpec((None, tl, w), lambda bi, i: (bi, i, 0))
    params = (cw, cb, dtb, a_rep, d_rep, nw) + tuple(consts)
    return pl.pallas_call(
        functools.partial(_ssd_body, tl=tl),
        grid=(b, l // tl),
        in_specs=[tok(D_XBC), tok(D_SSD), tok(LANE)] + [_full(a) for a in params],
        out_specs=tok(D_SSD),
        out_shape=jax.ShapeDtypeStruct((b, l, D_SSD), F32),
        scratch_shapes=[
            pltpu.VMEM((SSD_GROUPS, SSD_STATE, D_SSD // SSD_GROUPS), F32),
            pltpu.VMEM((tl + SUBLANE, D_XBC), F32),
        ],
        compiler_params=_params(("arbitrary", "arbitrary")),
        name="ssd",
    )(xbc3, z3, dt3, *params)


def _s5_consts(a_re, a_im, b_re, b_im, c_re, c_im, log_dt):
    t_ = S5_T
    eye = jnp.eye(S5_GPT, dtype=BF16)
    ar, ai = a_re.astype(F32), a_im.astype(F32)
    delta = jnp.exp(log_dt.astype(F32))[:, None]
    mag = jnp.exp(ar * delta)
    ab_re, ab_im = mag * jnp.cos(ai * delta), mag * jnp.sin(ai * delta)
    den = ar * ar + ai * ai
    nr, ni = ab_re - 1.0, ab_im
    fr = (nr * ar + ni * ai) / den
    fi = (ni * ar - nr * ai) / den
    br, bi = b_re.astype(F32), b_im.astype(F32)
    bb_re = fr[..., None] * br - fi[..., None] * bi
    bb_im = fr[..., None] * bi + fi[..., None] * br
    tau = jnp.arange(t_ + 1, dtype=F32)[:, None, None]
    pmag = jnp.exp(tau * (ar * delta))
    pr, pi = pmag * jnp.cos(tau * (ai * delta)), pmag * jnp.sin(tau * (ai * delta))
    abr = pr[..., None] * bb_re - pi[..., None] * bb_im
    abi = pr[..., None] * bb_im + pi[..., None] * bb_re
    cr, ci = c_re.astype(F32), c_im.astype(F32)
    hp = lax.Precision.HIGHEST
    kern = (jnp.einsum('gdp,tgpc->tgdc', cr, abr[:t_], precision=hp)
            - jnp.einsum('gdp,tgpc->tgdc', ci, abi[:t_], precision=hp))
    jj, ii = np.arange(t_)[:, None], np.arange(t_)[None, :]
    kt = kern[np.clip(ii - jj, 0, t_ - 1)] * jnp.asarray(ii >= jj, F32)[:, :, None, None, None]
    kt = kt.astype(BF16).reshape(t_, t_, S5_LT, S5_GPT, S5_GROUP_CH, S5_GROUP_CH)
    toep = jnp.einsum('jikgdc,gh->kjgcihd', kt, eye).reshape(S5_LT, t_ * LANE, t_ * LANE)
    rev = np.arange(t_ - 1, -1, -1)
    w_shape = (t_, S5_LT, S5_GPT, S5_STATE, S5_GROUP_CH)
    wz = [jnp.einsum('jkgpc,gh->kjgchp', w[rev].astype(BF16).reshape(w_shape), eye).reshape(S5_LT, t_ * LANE, S5_SW)
          for w in (abr, abi)]
    p1r, p1i = pr[1:], pi[1:]
    vr = cr[None] * p1r[:, :, None, :] - ci[None] * p1i[:, :, None, :]
    vi = -(cr[None] * p1i[:, :, None, :] + ci[None] * p1r[:, :, None, :])
    v_shape = (t_, S5_LT, S5_GPT, S5_GROUP_CH, S5_STATE)
    vv = [jnp.einsum('ikgdp,gh->kgpihd', w.astype(BF16).reshape(v_shape), eye).reshape(S5_LT, S5_SW, t_ * LANE)
          for w in (vr, vi)]
    a_t = jnp.stack([pr[t_].reshape(S5_LT, S5_SW), pi[t_].reshape(S5_LT, S5_SW)], axis=1)
    return toep, jnp.concatenate(wz, axis=-1), jnp.concatenate(vv, axis=1), a_t


def _s5_body(u_ref, toep_ref, wz_ref, vv_ref, at_ref, o_ref, z_ref, xp_ref, st_ref, *, n_rows):
    @pl.when(pl.program_id(2) == 0)
    def _():
        st_ref[...] = jnp.zeros_like(st_ref)

    ub = u_ref[...].astype(BF16)
    z_ref[...] = _dot(ub, wz_ref[...])
    ar, ai = at_ref[0:1, :], at_ref[1:2, :]
    re, im = slice(0, S5_SW), slice(S5_SW, 2 * S5_SW)

    def step(c, carry):
        xr, xi = carry
        row = pl.ds(c, 1)
        xp_ref[row, re] = xr
        xp_ref[row, im] = xi
        return (xr * ar - xi * ai + z_ref[row, re], xr * ai + xi * ar + z_ref[row, im])

    xr, xi = lax.fori_loop(0, n_rows, step, (st_ref[0:1, re], st_ref[0:1, im]), unroll=8)
    st_ref[0:1, re] = xr
    st_ref[0:1, im] = xi
    o_ref[...] = _dot(ub, toep_ref[...]) + _dot(xp_ref[...].astype(BF16), vv_ref[...])


def _s5(u4, consts, layer, batch, n_rows):
    toep, wz, vv, a_t = consts
    _, rows_total, w = u4.shape
    per_batch = rows_total // batch // n_rows
    blk = pl.BlockSpec((None, n_rows, w), lambda k, bi, i: (k, bi * per_batch + i, 0))
    wspec = lambda a: pl.BlockSpec((None, None) + a.shape[2:], lambda k, bi, i: (layer, k, 0, 0))
    return pl.pallas_call(
        functools.partial(_s5_body, n_rows=n_rows),
        grid=(S5_LT, batch, per_batch),
        in_specs=[blk, wspec(toep), wspec(wz), wspec(vv), wspec(a_t)],
        out_specs=blk,
        out_shape=jax.ShapeDtypeStruct(u4.shape, F32),
        scratch_shapes=[pltpu.VMEM((n_rows, 2 * S5_SW), F32), pltpu.VMEM((n_rows, 2 * S5_SW), F32),
                        pltpu.VMEM((SUBLANE, 2 * S5_SW), F32)],
        compiler_params=_params(("arbitrary", "arbitrary", "arbitrary")),
        name="s5",
    )(u4, toep, wz, vv, a_t)


def _outproj_body(h_ref, oa_ref, ob_ref, *rest):
    y_refs, u_refs = rest[0:S5_LT], rest[S5_LT:2 * S5_LT]
    d_ref, gw_ref, gb_ref, wo_ref, g_ref, o_ref = rest[2 * S5_LT:]
    yc = jnp.concatenate([r[...] for r in y_refs], axis=-1)
    u = jnp.concatenate([r[...] for r in u_refs], axis=-1)
    y = jax.nn.gelu(yc + d_ref[...] * u)
    oc = y * jax.nn.sigmoid(_dot(y.astype(BF16), gw_ref[...]) + gb_ref[...])
    mix = (_dot(oa_ref[...].astype(BF16), wo_ref[0:D_HG, :])
           + _dot(ob_ref[...].astype(BF16), wo_ref[D_HG:D_HG + D_SSD, :])
           + _dot(oc.astype(BF16), wo_ref[D_HG + D_SSD:D_MIX, :]))
    o_ref[...] = h_ref[...] + _rms(mix, g_ref[...])


def _outproj(h, oa, ob, y4, u4, s5_d, glu_w, glu_b, w_out, ng, layer, tm):
    n, d = h.shape
    tok = lambda w: pl.BlockSpec((tm, w), lambda i: (i, 0))
    tile = lambda k: pl.BlockSpec((None, tm, LANE), lambda i: (k, i, 0))
    tiles = [tile(k) for k in range(S5_LT)]
    return pl.pallas_call(
        _outproj_body,
        grid=(n // tm,),
        in_specs=[tok(d), tok(D_HG), tok(D_SSD)] + tiles + tiles + [
            pl.BlockSpec((None, 1, D_S5), lambda i: (layer, 0, 0)),
            pl.BlockSpec((None, D_S5, D_S5), lambda i: (layer, 0, 0)),
            pl.BlockSpec((None, 1, D_S5), lambda i: (layer, 0, 0)),
            pl.BlockSpec((None, D_MIX, d), lambda i: (layer, 0, 0)),
            pl.BlockSpec((None, 1, d), lambda i: (layer * 6 + 3, 0, 0)),
        ],
        out_specs=tok(d),
        out_shape=jax.ShapeDtypeStruct((n, d), F32),
        compiler_params=_params(("arbitrary",)),
        name="outproj",
    )(h, oa, ob, *([y4] * S5_LT), *([u4] * S5_LT), s5_d, glu_w, glu_b, w_out, ng)


def _tile(n, pref):
    return pref if n % pref == 0 else n


def kernel(x, norm_g, ffn_w_gate, ffn_w_up, ffn_w_down, w_in, w_out, hg_lb_logits, hg_gnorm, ssd_conv_w, ssd_conv_b,
           ssd_dt_bias, ssd_A_log, ssd_D, ssd_norm, s5_A_re, s5_A_im, s5_B_re, s5_B_im, s5_C_re, s5_C_im, s5_D,
           s5_log_dt, s5_glu_w, s5_glu_b):
    bsz, seq, d = x.shape
    depth = norm_g.shape[0]
    n = bsz * seq
    assert d == D_MODEL and seq % (CHUNK * S5_T) == 0 and w_in.shape[-1] == COL_DT + SSD_HEADS

    ng = norm_g.astype(F32).reshape(depth * 6, 1, d)
    wg, wu, wd = ffn_w_gate.astype(BF16), ffn_w_up.astype(BF16), ffn_w_down.astype(BF16)
    o = 4 * D_HG
    w_in_p = jnp.concatenate([
        w_in[..., 0:o], w_in[..., o + D_SSD:o + D_SSD + D_XBC], w_in[..., o:o + D_SSD],
        w_in[..., o + D_SSD + D_XBC + SSD_HEADS:], w_in[..., o + D_SSD + D_XBC:o + D_SSD + D_XBC + SSD_HEADS],
        jnp.zeros(w_in.shape[:-1] + (LANE - SSD_HEADS,), w_in.dtype)], axis=-1).astype(BF16)
    w_out_b = w_out.astype(BF16)
    glu_w_b = s5_glu_w.astype(BF16)
    lb = jnp.cumsum(jax.nn.softmax(hg_lb_logits.astype(F32), axis=0), axis=0)
    lb = (lb - lb[:1])[:, None, :]
    loglb, l1m, omlb = jnp.log(lb), jnp.log1p(-lb), 1.0 - lb
    gn = hg_gnorm.astype(F32)[:, None, :]
    a_rep = jnp.repeat(-jnp.exp(ssd_A_log.astype(F32)), SSD_HEADDIM, axis=-1)[:, None, :]
    d_rep = jnp.repeat(ssd_D.astype(F32), SSD_HEADDIM, axis=-1)[:, None, :]
    dtb = jnp.pad(ssd_dt_bias.astype(F32), ((0, 0), (0, LANE - SSD_HEADS)))[:, None, :]
    cw, cb, nw = ssd_conv_w.astype(F32), ssd_conv_b.astype(F32)[:, None, :], ssd_norm.astype(F32)[:, None, :]
    s5_c = jax.vmap(_s5_consts)(s5_A_re, s5_A_im, s5_B_re, s5_B_im, s5_C_re, s5_C_im, s5_log_dt)
    s5_d, glu_b = s5_D.astype(F32)[:, None, :], s5_glu_b.astype(F32)[:, None, :]
    hg_c = _hgrn_consts()
    ssd_c = _ssd_consts()

    tm = _tile(n, 512)
    tl_h = _tile(seq, 512)
    tl_s = _tile(seq, 512)
    s5_rows = _tile(seq // S5_T, 256)

    h = x.astype(F32).reshape(n, d)
    for l in range(depth):
        h = _ffn(h, ng, wg, wu, wd, l, 0, tm, D_FF // 2)
        qfig, xbc, z, u4, dt = _inproj(h, ng, w_in_p, l, tm)
        tok3 = lambda a: a.reshape(bsz, seq, a.shape[-1])
        oa = _hgrn(tok3(qfig), hg_c, loglb[l], l1m[l], omlb[l], gn[l], tl_h)
        ob = _ssd(tok3(xbc), tok3(z), tok3(dt), ssd_c, cw[l], cb[l], dtb[l], a_rep[l], d_rep[l], nw[l], tl_s)
        y4 = _s5(u4.reshape(S5_LT, n // S5_T, S5_T * LANE), s5_c, l, bsz, s5_rows).reshape(S5_LT, n, LANE)
        h = _outproj(h, oa.reshape(n, D_HG), ob.reshape(n, D_SSD), y4, u4, s5_d, glu_w_b, glu_b, w_out_b, ng, l, tm)
        h = _ffn(h, ng, wg, wu, wd, l, 1, tm, D_FF // 2)
    return h.reshape(bsz, seq, d).astype(x.dtype)
```

```python
import functools

import numpy as np
import jax
import jax.numpy as jnp
from jax import lax
from jax.experimental import pallas as pl
from jax.experimental.pallas import tpu as pltpu

F32 = jnp.float32
BF16 = jnp.bfloat16
EPS = 1e-6

D_MODEL = 1024
D_FF = 2816
HG_HEADS, HG_DK = 4, 128
D_HG = HG_HEADS * HG_DK
SSD_HEADS, SSD_HEADDIM, SSD_GROUPS, SSD_STATE, SSD_CONV = 8, 64, 2, 128, 4
D_SSD = SSD_HEADS * SSD_HEADDIM
D_XBC = D_SSD + 2 * SSD_GROUPS * SSD_STATE
S5_GROUPS, S5_GROUP_CH, S5_STATE = 32, 16, 64
D_S5 = S5_GROUPS * S5_GROUP_CH
D_MIX = D_HG + D_SSD + D_S5
CHUNK = 64
LANE = 128
SUBLANE = 8
S5_T = 16
S5_LT = D_S5 // LANE
S5_GPT = LANE // S5_GROUP_CH
S5_SW = S5_GPT * S5_STATE

COL_XBC, COL_Z, COL_U, COL_DT = 4 * D_HG, 4 * D_HG + D_XBC, 4 * D_HG + D_XBC + D_SSD, 4 * D_HG + D_XBC + D_SSD + D_S5
D_PROJ = COL_DT + LANE
VMEM_LIMIT = 56 * 1024 * 1024


def _dot(a, b):
    return jnp.dot(a, b, preferred_element_type=F32)


def _dot_nt(a, b):
    return lax.dot_general(a, b, (((1,), (1,)), ((), ())), preferred_element_type=F32)


def _dot_tn(a, b):
    return lax.dot_general(a, b, (((0,), (0,)), ((), ())), preferred_element_type=F32)


def _split(x):
    hi = x.astype(BF16)
    return hi, (x - hi.astype(F32)).astype(BF16)


def _dot_split(m2, x):
    hi, lo = _split(x)
    return _dot(m2, jnp.concatenate([hi, lo], axis=0))


def _dot_split_r(x, m2):
    hi, lo = _split(x)
    return _dot(jnp.concatenate([hi, lo], axis=1), m2)


def _rms(x, g):
    return x * lax.rsqrt(jnp.mean(x * x, axis=-1, keepdims=True) + EPS) * g


def _silu(x):
    return x * jax.nn.sigmoid(x)


def _params(sem):
    return pltpu.CompilerParams(dimension_semantics=sem, vmem_limit_bytes=VMEM_LIMIT)


def _full(a):
    return pl.BlockSpec(a.shape, lambda *_: (0,) * a.ndim)


def _ffn_body(x_ref, gpre_ref, wg_ref, wu_ref, wd_ref, gpost_ref, o_ref, xn_ref, acc_ref):
    f = pl.program_id(1)

    @pl.when(f == 0)
    def _():
        xn_ref[...] = _rms(x_ref[...], gpre_ref[...]).astype(BF16)

    xn = xn_ref[...]
    gate = _dot(xn, wg_ref[...])
    up = _dot(xn, wu_ref[...])
    act = (_silu(gate) * up).astype(BF16)
    part = _dot(act, wd_ref[...])

    @pl.when(f == 0)
    def _():
        acc_ref[...] = part

    @pl.when(f > 0)
    def _():
        acc_ref[...] += part

    @pl.when(f == pl.num_programs(1) - 1)
    def _():
        o_ref[...] = x_ref[...] + 0.5 * _rms(acc_ref[...], gpost_ref[...])


def _ffn(h, ng, wg, wu, wd, layer, which, tm, tf):
    n, d = h.shape
    kpre, kpost = (0, 1) if which == 0 else (4, 5)
    return pl.pallas_call(
        _ffn_body,
        grid=(n // tm, D_FF // tf),
        in_specs=[
            pl.BlockSpec((tm, d), lambda i, f: (i, 0)),
            pl.BlockSpec((None, 1, d), lambda i, f: (layer * 6 + kpre, 0, 0)),
            pl.BlockSpec((None, None, d, tf), lambda i, f: (layer, which, 0, f)),
            pl.BlockSpec((None, None, d, tf), lambda i, f: (layer, which, 0, f)),
            pl.BlockSpec((None, None, tf, d), lambda i, f: (layer, which, f, 0)),
            pl.BlockSpec((None, 1, d), lambda i, f: (layer * 6 + kpost, 0, 0)),
        ],
        out_specs=pl.BlockSpec((tm, d), lambda i, f: (i, 0)),
        out_shape=jax.ShapeDtypeStruct((n, d), F32),
        scratch_shapes=[pltpu.VMEM((tm, d), BF16), pltpu.VMEM((tm, d), F32)],
        compiler_params=_params(("arbitrary", "arbitrary")),
        name="ffn",
    )(h, ng, wg, wu, wd, ng)


def _inproj_body(x_ref, g_ref, w_ref, qfig_ref, xbc_ref, z_ref, u_ref, dt_ref):
    xn = _rms(x_ref[...], g_ref[...]).astype(BF16)
    qfig_ref[...] = _dot(xn, w_ref[:, 0:COL_XBC])
    xbc_ref[...] = _dot(xn, w_ref[:, COL_XBC:COL_Z])
    z_ref[...] = _dot(xn, w_ref[:, COL_Z:COL_U])
    u = _dot(xn, w_ref[:, COL_U:COL_DT])
    for k in range(S5_LT):
        u_ref[k] = u[:, k * LANE:(k + 1) * LANE]
    dt_ref[...] = _dot(xn, w_ref[:, COL_DT:D_PROJ])


def _inproj(h, ng, w_in_p, layer, tm):
    n, d = h.shape
    tok = lambda w: pl.BlockSpec((tm, w), lambda i: (i, 0))
    out_w = (4 * D_HG, D_XBC, D_SSD)
    return pl.pallas_call(
        _inproj_body,
        grid=(n // tm,),
        in_specs=[
            tok(d),
            pl.BlockSpec((None, 1, d), lambda i: (layer * 6 + 2, 0, 0)),
            pl.BlockSpec((None, d, D_PROJ), lambda i: (layer, 0, 0)),
        ],
        out_specs=[tok(w) for w in out_w] + [pl.BlockSpec((S5_LT, tm, LANE), lambda i: (0, i, 0)), tok(LANE)],
        out_shape=[jax.ShapeDtypeStruct((n, w), F32) for w in out_w]
        + [jax.ShapeDtypeStruct((S5_LT, n, LANE), F32), jax.ShapeDtypeStruct((n, LANE), F32)],
        compiler_params=_params(("arbitrary",)),
        name="inproj",
    )(h, ng, w_in_p)


HG_LEVELS = (32, 16, 8, 4, 2)


def _block_tri_up(m, n=CHUNK):
    t = np.arange(n)
    same = (t[:, None] // m) == (t[None, :] // m)
    tri = same & (t[None, :] <= t[:, None])
    up = same & (t[None, :] > t[:, None])
    return tri.astype(np.float32), up.astype(np.float32)


def _hgrn_consts():
    mats, masks = [], []
    for m in (CHUNK,) + HG_LEVELS:
        mats += list(_block_tri_up(m))
    t = np.arange(CHUNK)
    for m in HG_LEVELS + (1,):
        bt, bs = t[:, None] // m, t[None, :] // m
        masks.append(((bt % 2 == 1) & (bs == bt - 1)).astype(np.float32))
    masks.append(np.eye(CHUNK, dtype=np.float32))
    cm = np.concatenate(mats, 0)
    return jnp.asarray(np.concatenate([cm, cm], 1), BF16), jnp.asarray(np.stack(masks, 0), F32)


def _hgrn_body(p_ref, cm_ref, mask_ref, loglb_ref, l1m_ref, omlb_ref, gn_ref, o_ref, st_ref, *, n_chunks):
    @pl.when(pl.program_id(1) == 0)
    def _():
        st_ref[...] = jnp.zeros_like(st_ref)

    cm = cm_ref[...]
    loglb, l1m, omlb, gn = loglb_ref[...], l1m_ref[...], omlb_ref[...], gn_ref[...]
    nl = len(HG_LEVELS)

    def chunk_step(c, carry):
        rows = pl.ds(pl.multiple_of(c * CHUNK, CHUNK), CHUNK)
        q = _silu(p_ref[rows, 0:D_HG])
        fr = p_ref[rows, D_HG:2 * D_HG]
        v = p_ref[rows, 2 * D_HG:3 * D_HG].astype(BF16)
        gr = p_ref[rows, 3 * D_HG:4 * D_HG]
        ls = jnp.minimum(fr, 0.0) - jnp.log1p(jnp.exp(-jnp.abs(fr)))
        bb = l1m + ls
        lf = jnp.maximum(loglb, bb) + jnp.log1p(jnp.exp(-jnp.abs(loglb - bb)))
        kk = omlb * jnp.exp(ls - fr)
        cs = _dot_split(cm, lf)
        qb, kb = q.astype(BF16), kk.astype(BF16)
        qf = (q * jnp.exp(lf)).astype(BF16)
        outs = []
        for h in range(HG_HEADS):
            hs = slice(h * HG_DK, (h + 1) * HG_DK)
            qh, kh = q[:, hs], kk[:, hs]
            st = st_ref[h]
            c64, r64 = cs[0:CHUNK, hs], cs[CHUNK:2 * CHUNK, hs]
            scores = mask_ref[nl + 1] * _dot_nt(qb[:, hs], kb[:, hs]) + mask_ref[nl] * _dot_nt(qf[:, hs], kb[:, hs])
            for li in range(nl):
                base = 2 * CHUNK * (li + 1)
                cmm, rmm = cs[base:base + CHUNK, hs], cs[base + CHUNK:base + 2 * CHUNK, hs]
                p = _dot_nt((qh * jnp.exp(cmm)).astype(BF16), (kh * jnp.exp(rmm)).astype(BF16))
                scores = scores + mask_ref[li] * p
            o = _dot_nt((qh * jnp.exp(c64)).astype(BF16), st.astype(BF16)) + _dot(scores.astype(BF16), v[:, hs])
            ke = (kh * jnp.exp(r64)).astype(BF16)
            st_ref[h] = jnp.exp(c64[CHUNK - 1:CHUNK, :]) * st + _dot_tn(v[:, hs], ke)
            outs.append(o * lax.rsqrt(jnp.mean(o * o, axis=-1, keepdims=True) + EPS) * gn)
        o_ref[rows, :] = jnp.concatenate(outs, axis=-1) * _silu(gr)
        return carry

    lax.fori_loop(0, n_chunks, chunk_step, 0, unroll=2)


def _hgrn(qfig3, consts, loglb, l1m, omlb, gn, tl):
    b, l, _ = qfig3.shape
    cm, masks = consts
    return pl.pallas_call(
        functools.partial(_hgrn_body, n_chunks=tl // CHUNK),
        grid=(b, l // tl),
        in_specs=[pl.BlockSpec((None, tl, 4 * D_HG), lambda bi, i: (bi, i, 0)),
                  _full(cm), _full(masks), _full(loglb), _full(l1m), _full(omlb), _full(gn)],
        out_specs=pl.BlockSpec((None, tl, D_HG), lambda bi, i: (bi, i, 0)),
        out_shape=jax.ShapeDtypeStruct((b, l, D_HG), F32),
        scratch_shapes=[pltpu.VMEM((HG_HEADS, HG_DK, HG_DK), F32)],
        compiler_params=_params(("arbitrary", "arbitrary")),
        name="hgrn2",
    )(qfig3, cm, masks, loglb, l1m, omlb, gn)


def _ssd_consts():
    tri, up = _block_tri_up(CHUNK)
    t = np.arange(CHUNK)
    su = (t[:, None] > t[None, :]).astype(np.float32)
    causal = (t[:, None] >= t[None, :]).astype(np.float32)
    expand = np.zeros((LANE, D_SSD), np.float32)
    for h in range(SSD_HEADS):
        expand[h, h * SSD_HEADDIM:(h + 1) * SSD_HEADDIM] = 1.0
    tu = np.concatenate([tri, up], 0)
    return (jnp.asarray(np.concatenate([tu, tu], 1), BF16), jnp.asarray(np.tile(su, (1, SSD_HEADS)), F32),
            jnp.asarray(causal, F32), jnp.asarray(np.concatenate([expand, expand], 0), BF16))


def _ssd_body(xbc_ref, z_ref, dt_ref, cw_ref, cb_ref, dtb_ref, a_ref, d_ref, nw_ref, tu_ref, su_ref, causal_ref,
              ex_ref, o_ref, st_ref, ext_ref, *, tl):
    pad = SUBLANE

    @pl.when(pl.program_id(1) == 0)
    def _():
        st_ref[...] = jnp.zeros_like(st_ref)
        ext_ref[0:pad, :] = jnp.zeros((pad, D_XBC), F32)

    ext_ref[pad:pad + tl, :] = xbc_ref[...]

    tu, su, causal, expand = tu_ref[...], su_ref[...], causal_ref[...], ex_ref[...]
    a_rep, d_rep, nw, cb, dtb = a_ref[...], d_ref[...], nw_ref[...], cb_ref[...], dtb_ref[...]
    gw = D_SSD // SSD_GROUPS
    hpg = SSD_HEADS // SSD_GROUPS

    def chunk_step(c, carry):
        r0 = pl.multiple_of(c * CHUNK, CHUNK)
        rows = pl.ds(r0, CHUNK)
        win = ext_ref[pl.ds(r0, CHUNK + pad), :]
        conv = cb
        for j in range(SSD_CONV):
            lo = pad - (SSD_CONV - 1) + j
            conv = conv + cw_ref[j:j + 1, :] * win[lo:lo + CHUNK, :]
        xs = _silu(conv)
        x = xs[:, 0:D_SSD]
        dtin = dt_ref[rows, :] + dtb
        sp = jnp.maximum(dtin, 0.0) + jnp.log1p(jnp.exp(-jnp.abs(dtin)))
        dtr = _dot_split_r(sp, expand)
        adt = dtr * a_rep
        r = _dot_split(tu, jnp.concatenate([adt, adt * su], axis=-1))
        acs, rs = r[0:CHUNK, 0:D_SSD], r[CHUNK:2 * CHUNK, 0:D_SSD]
        dm = r[0:CHUNK, D_SSD:2 * D_SSD]
        xdt = x * dtr
        xdt_b = xdt.astype(BF16)
        ys = []
        for g in range(SSD_GROUPS):
            gs = slice(g * gw, (g + 1) * gw)
            bg = xs[:, D_SSD + g * SSD_STATE:D_SSD + (g + 1) * SSD_STATE].astype(BF16)
            cg = xs[:, D_SSD + (SSD_GROUPS + g) * SSD_STATE:D_SSD + (SSD_GROUPS + g + 1) * SSD_STATE].astype(BF16)
            st = st_ref[g]
            scores = _dot_nt(cg, bg) * causal
            y_off = _dot(cg, st.astype(BF16)) * jnp.exp(acs[:, gs])
            yd = []
            for hh in range(hpg):
                sl = slice((g * hpg + hh) * SSD_HEADDIM, (g * hpg + hh + 1) * SSD_HEADDIM)
                yd.append(_dot((scores * jnp.exp(dm[:, sl])).astype(BF16), xdt_b[:, sl]))
            ys.append(jnp.concatenate(yd, axis=-1) + y_off)
            xd = (xdt[:, gs] * jnp.exp(rs[:, gs])).astype(BF16)
            st_ref[g] = jnp.exp(acs[CHUNK - 1:CHUNK, gs]) * st + _dot_tn(bg, xd)
        y = jnp.concatenate(ys, axis=-1) + x * d_rep
        y = y * _silu(z_ref[rows, :])
        yn = [y[:, g * gw:(g + 1) * gw] * lax.rsqrt(
            jnp.mean(y[:, g * gw:(g + 1) * gw] * y[:, g * gw:(g + 1) * gw], axis=-1, keepdims=True) + EPS)
            for g in range(SSD_GROUPS)]
        o_ref[rows, :] = jnp.concatenate(yn, axis=-1) * nw
        return carry

    lax.fori_loop(0, tl // CHUNK, chunk_step, 0, unroll=2)
    ext_ref[0:pad, :] = xbc_ref[tl - pad:tl, :]


def _ssd(xbc3, z3, dt3, consts, cw, cb, dtb, a_rep, d_rep, nw, tl):
    b, l, _ = xbc3.shape
    tok = lambda w: pl.BlockSpec((None, tl, w), lambda bi, i: (bi, i, 0))
    params = (cw, cb, dtb, a_rep, d_rep, nw) + tuple(consts)
    return pl.pallas_call(
        functools.partial(_ssd_body, tl=tl),
        grid=(b, l // tl),
        in_specs=[tok(D_XBC), tok(D_SSD), tok(LANE)] + [_full(a) for a in params],
        out_specs=tok(D_SSD),
        out_shape=jax.ShapeDtypeStruct((b, l, D_SSD), F32),
        scratch_shapes=[
            pltpu.VMEM((SSD_GROUPS, SSD_STATE, D_SSD // SSD_GROUPS), F32),
            pltpu.VMEM((tl + SUBLANE, D_XBC), F32),
        ],
        compiler_params=_params(("arbitrary", "arbitrary")),
        name="ssd",
    )(xbc3, z3, dt3, *params)


def _s5_expand(r, rep, row_group, col_group):
    big = jnp.dot(r, rep, preferred_element_type=BF16)
    rows, cols = big.shape[-2:]
    rg = row_group(lax.broadcasted_iota(jnp.int32, (rows, cols), 0))
    cg = col_group(lax.broadcasted_iota(jnp.int32, (rows, cols), 1))
    return jnp.where(rg == cg, big, jnp.zeros_like(big))


def _s5_consts(a_re, a_im, b_re, b_im, c_re, c_im, log_dt):
    t_, gc, gp = S5_T, S5_GROUP_CH, S5_STATE
    ar, ai = a_re.astype(F32), a_im.astype(F32)
    delta = jnp.exp(log_dt.astype(F32))[:, None]
    mag = jnp.exp(ar * delta)
    ab_re, ab_im = mag * jnp.cos(ai * delta), mag * jnp.sin(ai * delta)
    den = ar * ar + ai * ai
    nr, ni = ab_re - 1.0, ab_im
    fr = (nr * ar + ni * ai) / den
    fi = (ni * ar - nr * ai) / den
    br, bi = b_re.astype(F32).transpose(0, 2, 1), b_im.astype(F32).transpose(0, 2, 1)
    bb_re = fr[:, None, :] * br - fi[:, None, :] * bi
    bb_im = fr[:, None, :] * bi + fi[:, None, :] * br
    lam_r, lam_i = ar * delta, ai * delta
    tau = jnp.arange(t_ + 1, dtype=F32)
    pr = jnp.exp(tau[:, None, None] * lam_r) * jnp.cos(tau[:, None, None] * lam_i)
    pi = jnp.exp(tau[:, None, None] * lam_r) * jnp.sin(tau[:, None, None] * lam_i)
    qr = jnp.exp(lam_r[..., None] * tau) * jnp.cos(lam_i[..., None] * tau)
    qi = jnp.exp(lam_r[..., None] * tau) * jnp.sin(lam_i[..., None] * tau)
    abr = pr[:, :, None, :] * bb_re - pi[:, :, None, :] * bb_im
    abi = pr[:, :, None, :] * bb_im + pi[:, :, None, :] * bb_re
    cr, ci = c_re.astype(F32), c_im.astype(F32)
    hp = lax.Precision.HIGHEST
    km = (jnp.einsum('tgcp,gdp->gctd', abr[:t_], cr, precision=hp)
          - jnp.einsum('tgcp,gdp->gctd', abi[:t_], ci, precision=hp))
    tt, jj, ii = np.arange(t_)[:, None, None], np.arange(t_)[None, :, None], np.arange(t_)[None, None, :]
    shift = np.einsum('tji,de->tdjie', (tt == ii - jj).astype(np.float32), np.eye(gc, dtype=np.float32))
    shift = jnp.asarray(shift.reshape(t_ * gc, t_ * t_ * gc), BF16)
    r_t = jnp.dot(km.astype(BF16).reshape(S5_GROUPS * gc, t_ * gc), shift, preferred_element_type=BF16)
    r_t = r_t.reshape(S5_LT, S5_GPT, gc, t_, t_ * gc).transpose(0, 3, 1, 2, 4).reshape(S5_LT, t_ * LANE, t_ * gc)
    rev = np.arange(t_ - 1, -1, -1)
    r_w = jnp.concatenate([abr[rev], abi[rev]], axis=-1).astype(BF16)
    r_w = r_w.reshape(t_, S5_LT, S5_GPT, gc, 2 * gp).transpose(1, 0, 2, 3, 4).reshape(S5_LT, t_ * LANE, 2 * gp)
    crt, cit = cr.transpose(0, 2, 1)[:, :, None, :], ci.transpose(0, 2, 1)[:, :, None, :]
    q1r, q1i = qr[:, :, 1:, None], qi[:, :, 1:, None]
    vr = (crt * q1r - cit * q1i).astype(BF16).reshape(S5_LT, S5_SW, t_ * gc)
    vi = (-(crt * q1i + cit * q1r)).astype(BF16).reshape(S5_LT, S5_SW, t_ * gc)
    r_v = jnp.concatenate([vr, vi], axis=1)
    src, dst = np.arange(t_ * gc), np.arange(t_ * LANE)
    rep = jnp.asarray((src[:, None] // gc == dst[None, :] // LANE) & (src[:, None] % gc == dst[None, :] % gc), BF16)
    src, dst = np.arange(2 * gp), np.arange(2 * S5_SW)
    repw = jnp.asarray((src[:, None] // gp == dst[None, :] // S5_SW) & (src[:, None] % gp == dst[None, :] % gp), BF16)
    io_group = lambda x: (x // gc) % S5_GPT
    st_group = lambda x: (x % S5_SW) // gp
    toep = _s5_expand(r_t, rep, io_group, io_group)
    wz = _s5_expand(r_w, repw, io_group, st_group)
    vv = _s5_expand(r_v, rep, st_group, io_group)
    a_t = jnp.stack([pr[t_].reshape(S5_LT, S5_SW), pi[t_].reshape(S5_LT, S5_SW)], axis=1)
    return toep, wz, vv, a_t


def _s5_body(u_ref, toep_ref, wz_ref, vv_ref, at_ref, o_ref, z_ref, xp_ref, st_ref, *, n_rows):
    @pl.when(pl.program_id(2) == 0)
    def _():
        st_ref[...] = jnp.zeros_like(st_ref)

    ub = u_ref[...].astype(BF16)
    z_ref[...] = _dot(ub, wz_ref[...])
    ar, ai = at_ref[0:1, :], at_ref[1:2, :]
    re, im = slice(0, S5_SW), slice(S5_SW, 2 * S5_SW)

    def step(c, carry):
        xr, xi = carry
        row = pl.ds(c, 1)
        xp_ref[row, re] = xr
        xp_ref[row, im] = xi
        return (xr * ar - xi * ai + z_ref[row, re], xr * ai + xi * ar + z_ref[row, im])

    xr, xi = lax.fori_loop(0, n_rows, step, (st_ref[0:1, re], st_ref[0:1, im]), unroll=8)
    st_ref[0:1, re] = xr
    st_ref[0:1, im] = xi
    o_ref[...] = _dot(ub, toep_ref[...]) + _dot(xp_ref[...].astype(BF16), vv_ref[...])


def _s5(u4, consts, layer, batch, n_rows):
    toep, wz, vv, a_t = consts
    _, rows_total, w = u4.shape
    per_batch = rows_total // batch // n_rows
    blk = pl.BlockSpec((None, n_rows, w), lambda k, bi, i: (k, bi * per_batch + i, 0))
    wspec = lambda a: pl.BlockSpec((None, None) + a.shape[2:], lambda k, bi, i: (layer, k, 0, 0))
    return pl.pallas_call(
        functools.partial(_s5_body, n_rows=n_rows),
        grid=(S5_LT, batch, per_batch),
        in_specs=[blk, wspec(toep), wspec(wz), wspec(vv), wspec(a_t)],
        out_specs=blk,
        out_shape=jax.ShapeDtypeStruct(u4.shape, F32),
        scratch_shapes=[pltpu.VMEM((n_rows, 2 * S5_SW), F32), pltpu.VMEM((n_rows, 2 * S5_SW), F32),
                        pltpu.VMEM((SUBLANE, 2 * S5_SW), F32)],
        compiler_params=_params(("arbitrary", "arbitrary", "arbitrary")),
        name="s5",
    )(u4, toep, wz, vv, a_t)


def _outproj_body(h_ref, oa_ref, ob_ref, *rest):
    y_refs, u_refs = rest[0:S5_LT], rest[S5_LT:2 * S5_LT]
    d_ref, gw_ref, gb_ref, wo_ref, g_ref, o_ref = rest[2 * S5_LT:]
    yc = jnp.concatenate([r[...] for r in y_refs], axis=-1)
    u = jnp.concatenate([r[...] for r in u_refs], axis=-1)
    y = jax.nn.gelu(yc + d_ref[...] * u)
    oc = y * jax.nn.sigmoid(_dot(y.astype(BF16), gw_ref[...]) + gb_ref[...])
    mix = (_dot(oa_ref[...].astype(BF16), wo_ref[0:D_HG, :])
           + _dot(ob_ref[...].astype(BF16), wo_ref[D_HG:D_HG + D_SSD, :])
           + _dot(oc.astype(BF16), wo_ref[D_HG + D_SSD:D_MIX, :]))
    o_ref[...] = h_ref[...] + _rms(mix, g_ref[...])


def _outproj(h, oa, ob, y4, u4, s5_d, glu_w, glu_b, w_out, ng, layer, tm):
    n, d = h.shape
    tok = lambda w: pl.BlockSpec((tm, w), lambda i: (i, 0))
    tile = lambda k: pl.BlockSpec((None, tm, LANE), lambda i: (k, i, 0))
    tiles = [tile(k) for k in range(S5_LT)]
    return pl.pallas_call(
        _outproj_body,
        grid=(n // tm,),
        in_specs=[tok(d), tok(D_HG), tok(D_SSD)] + tiles + tiles + [
            pl.BlockSpec((None, 1, D_S5), lambda i: (layer, 0, 0)),
            pl.BlockSpec((None, D_S5, D_S5), lambda i: (layer, 0, 0)),
            pl.BlockSpec((None, 1, D_S5), lambda i: (layer, 0, 0)),
            pl.BlockSpec((None, D_MIX, d), lambda i: (layer, 0, 0)),
            pl.BlockSpec((None, 1, d), lambda i: (layer * 6 + 3, 0, 0)),
        ],
        out_specs=tok(d),
        out_shape=jax.ShapeDtypeStruct((n, d), F32),
        compiler_params=_params(("arbitrary",)),
        name="outproj",
    )(h, oa, ob, *([y4] * S5_LT), *([u4] * S5_LT), s5_d, glu_w, glu_b, w_out, ng)


def _tile(n, pref):
    return pref if n % pref == 0 else n


def kernel(x, norm_g, ffn_w_gate, ffn_w_up, ffn_w_down, w_in, w_out, hg_lb_logits, hg_gnorm, ssd_conv_w, ssd_conv_b,
           ssd_dt_bias, ssd_A_log, ssd_D, ssd_norm, s5_A_re, s5_A_im, s5_B_re, s5_B_im, s5_C_re, s5_C_im, s5_D,
           s5_log_dt, s5_glu_w, s5_glu_b):
    bsz, seq, d = x.shape
    depth = norm_g.shape[0]
    n = bsz * seq
    assert d == D_MODEL and seq % (CHUNK * S5_T) == 0 and w_in.shape[-1] == COL_DT + SSD_HEADS

    ng = norm_g.astype(F32).reshape(depth * 6, 1, d)
    wg, wu, wd = ffn_w_gate.astype(BF16), ffn_w_up.astype(BF16), ffn_w_down.astype(BF16)
    o = 4 * D_HG
    w_in_p = jnp.concatenate([
        w_in[..., 0:o], w_in[..., o + D_SSD:o + D_SSD + D_XBC], w_in[..., o:o + D_SSD],
        w_in[..., o + D_SSD + D_XBC + SSD_HEADS:], w_in[..., o + D_SSD + D_XBC:o + D_SSD + D_XBC + SSD_HEADS],
        jnp.zeros(w_in.shape[:-1] + (LANE - SSD_HEADS,), w_in.dtype)], axis=-1).astype(BF16)
    w_out_b = w_out.astype(BF16)
    glu_w_b = s5_glu_w.astype(BF16)
    lb = jnp.cumsum(jax.nn.softmax(hg_lb_logits.astype(F32), axis=0), axis=0)
    lb = (lb - lb[:1])[:, None, :]
    loglb, l1m, omlb = jnp.log(lb), jnp.log1p(-lb), 1.0 - lb
    gn = hg_gnorm.astype(F32)[:, None, :]
    a_rep = jnp.repeat(-jnp.exp(ssd_A_log.astype(F32)), SSD_HEADDIM, axis=-1)[:, None, :]
    d_rep = jnp.repeat(ssd_D.astype(F32), SSD_HEADDIM, axis=-1)[:, None, :]
    dtb = jnp.pad(ssd_dt_bias.astype(F32), ((0, 0), (0, LANE - SSD_HEADS)))[:, None, :]
    cw, cb, nw = ssd_conv_w.astype(F32), ssd_conv_b.astype(F32)[:, None, :], ssd_norm.astype(F32)[:, None, :]
    s5_c = jax.vmap(_s5_consts)(s5_A_re, s5_A_im, s5_B_re, s5_B_im, s5_C_re, s5_C_im, s5_log_dt)
    s5_d, glu_b = s5_D.astype(F32)[:, None, :], s5_glu_b.astype(F32)[:, None, :]
    hg_c = _hgrn_consts()
    ssd_c = _ssd_consts()

    tm = _tile(n, 512)
    tl_h = _tile(seq, 512)
    tl_s = _tile(seq, 512)
    s5_rows = _tile(seq // S5_T, 256)

    h = x.astype(F32).reshape(n, d)
    for l in range(depth):
        h = _ffn(h, ng, wg, wu, wd, l, 0, tm, D_FF // 2)
        qfig, xbc, z, u4, dt = _inproj(h, ng, w_in_p, l, tm)
        tok3 = lambda a: a.reshape(bsz, seq, a.shape[-1])
        oa = _hgrn(tok3(qfig), hg_c, loglb[l], l1m[l], omlb[l], gn[l], tl_h)
        ob = _ssd(tok3(xbc), tok3(z), tok3(dt), ssd_c, cw[l], cb[l], dtb[l], a_rep[l], d_rep[l], nw[l], tl_s)
        y4 = _s5(u4.reshape(S5_LT, n // S5_T, S5_T * LANE), s5_c, l, bsz, s5_rows).reshape(S5_LT, n, LANE)
        h = _outproj(h, oa.reshape(n, D_HG), ob.reshape(n, D_SSD), y4, u4, s5_d, glu_w_b, glu_b, w_out_b, ng, l, tm)
        h = _ffn(h, ng, wg, wu, wd, l, 1, tm, D_FF // 2)
    return h.reshape(bsz, seq, d).astype(x.dtype)
```

```python
import functools

import numpy as np
import jax
import jax.numpy as jnp
from jax import lax
from jax.experimental import pallas as pl
from jax.experimental.pallas import tpu as pltpu

F32 = jnp.float32
BF16 = jnp.bfloat16
EPS = 1e-6

D_MODEL = 1024
D_FF = 2816
HG_HEADS, HG_DK = 4, 128
D_HG = HG_HEADS * HG_DK
SSD_HEADS, SSD_HEADDIM, SSD_GROUPS, SSD_STATE, SSD_CONV = 8, 64, 2, 128, 4
D_SSD = SSD_HEADS * SSD_HEADDIM
D_XBC = D_SSD + 2 * SSD_GROUPS * SSD_STATE
S5_GROUPS, S5_GROUP_CH, S5_STATE = 32, 16, 64
D_S5 = S5_GROUPS * S5_GROUP_CH
D_MIX = D_HG + D_SSD + D_S5
CHUNK = 64
LANE = 128
SUBLANE = 8
S5_T = 16
S5_LT = D_S5 // LANE
S5_GPT = LANE // S5_GROUP_CH
S5_SW = S5_GPT * S5_STATE

COL_XBC, COL_Z, COL_U, COL_DT = 4 * D_HG, 4 * D_HG + D_XBC, 4 * D_HG + D_XBC + D_SSD, 4 * D_HG + D_XBC + D_SSD + D_S5
D_PROJ = COL_DT + LANE
VMEM_LIMIT = 56 * 1024 * 1024


def _dot(a, b):
    return jnp.dot(a, b, preferred_element_type=F32)


def _dot_nt(a, b):
    return lax.dot_general(a, b, (((1,), (1,)), ((), ())), preferred_element_type=F32)


def _dot_tn(a, b):
    return lax.dot_general(a, b, (((0,), (0,)), ((), ())), preferred_element_type=F32)


def _split(x):
    hi = x.astype(BF16)
    return hi, (x - hi.astype(F32)).astype(BF16)


def _dot_split(m2, x):
    hi, lo = _split(x)
    return _dot(m2, jnp.concatenate([hi, lo], axis=0))


def _dot_split_r(x, m2):
    hi, lo = _split(x)
    return _dot(jnp.concatenate([hi, lo], axis=1), m2)


def _rms(x, g):
    return x * lax.rsqrt(jnp.mean(x * x, axis=-1, keepdims=True) + EPS) * g


def _silu(x):
    return x * jax.nn.sigmoid(x)


def _params(sem):
    return pltpu.CompilerParams(dimension_semantics=sem, vmem_limit_bytes=VMEM_LIMIT)


def _full(a):
    return pl.BlockSpec(a.shape, lambda *_: (0,) * a.ndim)


def _ffn_body(x_ref, gpre_ref, wg_ref, wu_ref, wd_ref, gpost_ref, o_ref, xn_ref, acc_ref):
    f = pl.program_id(1)

    @pl.when(f == 0)
    def _():
        xn_ref[...] = _rms(x_ref[...], gpre_ref[...]).astype(BF16)

    xn = xn_ref[...]
    gate = _dot(xn, wg_ref[...])
    up = _dot(xn, wu_ref[...])
    act = (_silu(gate) * up).astype(BF16)
    part = _dot(act, wd_ref[...])

    @pl.when(f == 0)
    def _():
        acc_ref[...] = part

    @pl.when(f > 0)
    def _():
        acc_ref[...] += part

    @pl.when(f == pl.num_programs(1) - 1)
    def _():
        o_ref[...] = x_ref[...] + 0.5 * _rms(acc_ref[...], gpost_ref[...])


def _ffn(h, ng, wg, wu, wd, layer, which, tm, tf):
    n, d = h.shape
    kpre, kpost = (0, 1) if which == 0 else (4, 5)
    return pl.pallas_call(
        _ffn_body,
        grid=(n // tm, D_FF // tf),
        in_specs=[
            pl.BlockSpec((tm, d), lambda i, f: (i, 0)),
            pl.BlockSpec((None, 1, d), lambda i, f: (layer * 6 + kpre, 0, 0)),
            pl.BlockSpec((None, None, d, tf), lambda i, f: (layer, which, 0, f)),
            pl.BlockSpec((None, None, d, tf), lambda i, f: (layer, which, 0, f)),
            pl.BlockSpec((None, None, tf, d), lambda i, f: (layer, which, f, 0)),
            pl.BlockSpec((None, 1, d), lambda i, f: (layer * 6 + kpost, 0, 0)),
        ],
        out_specs=pl.BlockSpec((tm, d), lambda i, f: (i, 0)),
        out_shape=jax.ShapeDtypeStruct((n, d), F32),
        scratch_shapes=[pltpu.VMEM((tm, d), BF16), pltpu.VMEM((tm, d), F32)],
        compiler_params=_params(("arbitrary", "arbitrary")),
        name="ffn",
    )(h, ng, wg, wu, wd, ng)


def _inproj_body(x_ref, g_ref, w_ref, qfig_ref, xbc_ref, z_ref, u_ref, dt_ref):
    xn = _rms(x_ref[...], g_ref[...]).astype(BF16)
    qfig_ref[...] = _dot(xn, w_ref[:, 0:COL_XBC])
    xbc_ref[...] = _dot(xn, w_ref[:, COL_XBC:COL_Z])
    z_ref[...] = _dot(xn, w_ref[:, COL_Z:COL_U])
    u = _dot(xn, w_ref[:, COL_U:COL_DT])
    for k in range(S5_LT):
        u_ref[k] = u[:, k * LANE:(k + 1) * LANE]
    dt_ref[...] = _dot(xn, w_ref[:, COL_DT:D_PROJ])


def _inproj(h, ng, w_in_p, layer, tm):
    n, d = h.shape
    tok = lambda w: pl.BlockSpec((tm, w), lambda i: (i, 0))
    out_w = (4 * D_HG, D_XBC, D_SSD)
    return pl.pallas_call(
        _inproj_body,
        grid=(n // tm,),
        in_specs=[
            tok(d),
            pl.BlockSpec((None, 1, d), lambda i: (layer * 6 + 2, 0, 0)),
            pl.BlockSpec((None, d, D_PROJ), lambda i: (layer, 0, 0)),
        ],
        out_specs=[tok(w) for w in out_w] + [pl.BlockSpec((S5_LT, tm, LANE), lambda i: (0, i, 0)), tok(LANE)],
        out_shape=[jax.ShapeDtypeStruct((n, w), F32) for w in out_w]
        + [jax.ShapeDtypeStruct((S5_LT, n, LANE), F32), jax.ShapeDtypeStruct((n, LANE), F32)],
        compiler_params=_params(("arbitrary",)),
        name="inproj",
    )(h, ng, w_in_p)


HG_LEVELS = (32, 16, 8, 4, 2)


def _block_tri_up(m, n=CHUNK):
    t = np.arange(n)
    same = (t[:, None] // m) == (t[None, :] // m)
    tri = same & (t[None, :] <= t[:, None])
    up = same & (t[None, :] > t[:, None])
    return tri.astype(np.float32), up.astype(np.float32)


def _hgrn_consts():
    mats, masks = [], []
    for m in (CHUNK,) + HG_LEVELS:
        mats += list(_block_tri_up(m))
    t = np.arange(CHUNK)
    for m in HG_LEVELS + (1,):
        bt, bs = t[:, None] // m, t[None, :] // m
        masks.append(((bt % 2 == 1) & (bs == bt - 1)).astype(np.float32))
    masks.append(np.eye(CHUNK, dtype=np.float32))
    cm = np.concatenate(mats, 0)
    return jnp.asarray(np.concatenate([cm, cm], 1), BF16), jnp.asarray(np.stack(masks, 0), F32)


def _hgrn_body(p_ref, cm_ref, mask_ref, loglb_ref, l1m_ref, omlb_ref, gn_ref, o_ref, st_ref, *, n_chunks):
    @pl.when(pl.program_id(0) == 0)
    def _():
        st_ref[...] = jnp.zeros_like(st_ref)

    cm = cm_ref[...]
    loglb, l1m, omlb, gn = loglb_ref[...], l1m_ref[...], omlb_ref[...], gn_ref[...]
    nl = len(HG_LEVELS)

    def chunk_one(b, rows):
        q = _silu(p_ref[b, rows, 0:D_HG])
        fr = p_ref[b, rows, D_HG:2 * D_HG]
        v = p_ref[b, rows, 2 * D_HG:3 * D_HG].astype(BF16)
        gr = p_ref[b, rows, 3 * D_HG:4 * D_HG]
        ls = jnp.minimum(fr, 0.0) - jnp.log1p(jnp.exp(-jnp.abs(fr)))
        bb = l1m + ls
        lf = jnp.maximum(loglb, bb) + jnp.log1p(jnp.exp(-jnp.abs(loglb - bb)))
        kk = omlb * jnp.exp(ls - fr)
        cs = _dot_split(cm, lf)
        qb, kb = q.astype(BF16), kk.astype(BF16)
        qf = (q * jnp.exp(lf)).astype(BF16)
        outs = []
        for h in range(HG_HEADS):
            hs = slice(h * HG_DK, (h + 1) * HG_DK)
            qh, kh = q[:, hs], kk[:, hs]
            st = st_ref[b, h]
            c64, r64 = cs[0:CHUNK, hs], cs[CHUNK:2 * CHUNK, hs]
            scores = mask_ref[nl + 1] * _dot_nt(qb[:, hs], kb[:, hs]) + mask_ref[nl] * _dot_nt(qf[:, hs], kb[:, hs])
            for li in range(nl):
                base = 2 * CHUNK * (li + 1)
                cmm, rmm = cs[base:base + CHUNK, hs], cs[base + CHUNK:base + 2 * CHUNK, hs]
                p = _dot_nt((qh * jnp.exp(cmm)).astype(BF16), (kh * jnp.exp(rmm)).astype(BF16))
                scores = scores + mask_ref[li] * p
            o = _dot_nt((qh * jnp.exp(c64)).astype(BF16), st.astype(BF16)) + _dot(scores.astype(BF16), v[:, hs])
            ke = (kh * jnp.exp(r64)).astype(BF16)
            st_ref[b, h] = jnp.exp(c64[CHUNK - 1:CHUNK, :]) * st + _dot_tn(v[:, hs], ke)
            outs.append(o * lax.rsqrt(jnp.mean(o * o, axis=-1, keepdims=True) + EPS) * gn)
        o_ref[b, rows, :] = jnp.concatenate(outs, axis=-1) * _silu(gr)

    def chunk_step(c, carry):
        rows = pl.ds(pl.multiple_of(c * CHUNK, CHUNK), CHUNK)
        for b in range(p_ref.shape[0]):
            chunk_one(b, rows)
        return carry

    lax.fori_loop(0, n_chunks, chunk_step, 0, unroll=2)


def _hgrn(qfig3, consts, loglb, l1m, omlb, gn, tl):
    b, l, _ = qfig3.shape
    cm, masks = consts
    return pl.pallas_call(
        functools.partial(_hgrn_body, n_chunks=tl // CHUNK),
        grid=(l // tl,),
        in_specs=[pl.BlockSpec((b, tl, 4 * D_HG), lambda i: (0, i, 0)),
                  _full(cm), _full(masks), _full(loglb), _full(l1m), _full(omlb), _full(gn)],
        out_specs=pl.BlockSpec((b, tl, D_HG), lambda i: (0, i, 0)),
        out_shape=jax.ShapeDtypeStruct((b, l, D_HG), F32),
        scratch_shapes=[pltpu.VMEM((b, HG_HEADS, HG_DK, HG_DK), F32)],
        compiler_params=_params(("arbitrary",)),
        name="hgrn2",
    )(qfig3, cm, masks, loglb, l1m, omlb, gn)


def _ssd_consts():
    tri, up = _block_tri_up(CHUNK)
    t = np.arange(CHUNK)
    su = (t[:, None] > t[None, :]).astype(np.float32)
    causal = (t[:, None] >= t[None, :]).astype(np.float32)
    expand = np.zeros((LANE, D_SSD), np.float32)
    for h in range(SSD_HEADS):
        expand[h, h * SSD_HEADDIM:(h + 1) * SSD_HEADDIM] = 1.0
    tu = np.concatenate([tri, up], 0)
    return (jnp.asarray(np.concatenate([tu, tu], 1), BF16), jnp.asarray(np.tile(su, (1, SSD_HEADS)), F32),
            jnp.asarray(causal, F32), jnp.asarray(np.concatenate([expand, expand], 0), BF16))


def _ssd_body(xbc_ref, z_ref, dt_ref, cw_ref, cb_ref, dtb_ref, a_ref, d_ref, nw_ref, tu_ref, su_ref, causal_ref,
              ex_ref, o_ref, st_ref, ext_ref, *, tl):
    pad = SUBLANE
    nb = xbc_ref.shape[0]

    @pl.when(pl.program_id(0) == 0)
    def _():
        st_ref[...] = jnp.zeros_like(st_ref)
        ext_ref[:, 0:pad, :] = jnp.zeros((nb, pad, D_XBC), F32)

    ext_ref[:, pad:pad + tl, :] = xbc_ref[...]

    tu, su, causal, expand = tu_ref[...], su_ref[...], causal_ref[...], ex_ref[...]
    a_rep, d_rep, nw, cb, dtb = a_ref[...], d_ref[...], nw_ref[...], cb_ref[...], dtb_ref[...]
    gw = D_SSD // SSD_GROUPS
    hpg = SSD_HEADS // SSD_GROUPS

    def chunk_one(b, r0):
        rows = pl.ds(r0, CHUNK)
        win = ext_ref[b, pl.ds(r0, CHUNK + pad), :]
        conv = cb
        for j in range(SSD_CONV):
            lo = pad - (SSD_CONV - 1) + j
            conv = conv + cw_ref[j:j + 1, :] * win[lo:lo + CHUNK, :]
        xs = _silu(conv)
        x = xs[:, 0:D_SSD]
        dtin = dt_ref[b, rows, :] + dtb
        sp = jnp.maximum(dtin, 0.0) + jnp.log1p(jnp.exp(-jnp.abs(dtin)))
        dtr = _dot_split_r(sp, expand)
        adt = dtr * a_rep
        r = _dot_split(tu, jnp.concatenate([adt, adt * su], axis=-1))
        acs, rs = r[0:CHUNK, 0:D_SSD], r[CHUNK:2 * CHUNK, 0:D_SSD]
        dm = r[0:CHUNK, D_SSD:2 * D_SSD]
        xdt = x * dtr
        xdt_b = xdt.astype(BF16)
        ys = []
        for g in range(SSD_GROUPS):
            gs = slice(g * gw, (g + 1) * gw)
            bg = xs[:, D_SSD + g * SSD_STATE:D_SSD + (g + 1) * SSD_STATE].astype(BF16)
            cg = xs[:, D_SSD + (SSD_GROUPS + g) * SSD_STATE:D_SSD + (SSD_GROUPS + g + 1) * SSD_STATE].astype(BF16)
            st = st_ref[b, g]
            scores = _dot_nt(cg, bg) * causal
            y_off = _dot(cg, st.astype(BF16)) * jnp.exp(acs[:, gs])
            yd = []
            for hh in range(hpg):
                sl = slice((g * hpg + hh) * SSD_HEADDIM, (g * hpg + hh + 1) * SSD_HEADDIM)
                yd.append(_dot((scores * jnp.exp(dm[:, sl])).astype(BF16), xdt_b[:, sl]))
            ys.append(jnp.concatenate(yd, axis=-1) + y_off)
            xd = (xdt[:, gs] * jnp.exp(rs[:, gs])).astype(BF16)
            st_ref[b, g] = jnp.exp(acs[CHUNK - 1:CHUNK, gs]) * st + _dot_tn(bg, xd)
        y = jnp.concatenate(ys, axis=-1) + x * d_rep
        y = y * _silu(z_ref[b, rows, :])
        yn = [y[:, g * gw:(g + 1) * gw] * lax.rsqrt(
            jnp.mean(y[:, g * gw:(g + 1) * gw] * y[:, g * gw:(g + 1) * gw], axis=-1, keepdims=True) + EPS)
            for g in range(SSD_GROUPS)]
        o_ref[b, rows, :] = jnp.concatenate(yn, axis=-1) * nw

    def chunk_step(c, carry):
        r0 = pl.multiple_of(c * CHUNK, CHUNK)
        for b in range(nb):
            chunk_one(b, r0)
        return carry

    lax.fori_loop(0, tl // CHUNK, chunk_step, 0, unroll=2)
    ext_ref[:, 0:pad, :] = xbc_ref[:, tl - pad:tl, :]


def _ssd(xbc3, z3, dt3, consts, cw, cb, dtb, a_rep, d_rep, nw, tl):
    b, l, _ = xbc3.shape
    tok = lambda w: pl.BlockSpec((b, tl, w), lambda i: (0, i, 0))
    params = (cw, cb, dtb, a_rep, d_rep, nw) + tuple(consts)
    return pl.pallas_call(
        functools.partial(_ssd_body, tl=tl),
        grid=(l // tl,),
        in_specs=[tok(D_XBC), tok(D_SSD), tok(LANE)] + [_full(a) for a in params],
        out_specs=tok(D_SSD),
        out_shape=jax.ShapeDtypeStruct((b, l, D_SSD), F32),
        scratch_shapes=[
            pltpu.VMEM((b, SSD_GROUPS, SSD_STATE, D_SSD // SSD_GROUPS), F32),
            pltpu.VMEM((b, tl + SUBLANE, D_XBC), F32),
        ],
        compiler_params=_params(("arbitrary",)),
        name="ssd",
    )(xbc3, z3, dt3, *params)


def _s5_expand(r, rep, row_group, col_group):
    big = jnp.dot(r, rep, preferred_element_type=BF16)
    rows, cols = big.shape[-2:]
    rg = row_group(lax.broadcasted_iota(jnp.int32, (rows, cols), 0))
    cg = col_group(lax.broadcasted_iota(jnp.int32, (rows, cols), 1))
    return jnp.where(rg == cg, big, jnp.zeros_like(big))


def _s5_consts(a_re, a_im, b_re, b_im, c_re, c_im, log_dt):
    t_, gc, gp = S5_T, S5_GROUP_CH, S5_STATE
    ar, ai = a_re.astype(F32), a_im.astype(F32)
    delta = jnp.exp(log_dt.astype(F32))[:, None]
    mag = jnp.exp(ar * delta)
    ab_re, ab_im = mag * jnp.cos(ai * delta), mag * jnp.sin(ai * delta)
    den = ar * ar + ai * ai
    nr, ni = ab_re - 1.0, ab_im
    fr = (nr * ar + ni * ai) / den
    fi = (ni * ar - nr * ai) / den
    br, bi = b_re.astype(F32).transpose(0, 2, 1), b_im.astype(F32).transpose(0, 2, 1)
    bb_re = fr[:, None, :] * br - fi[:, None, :] * bi
    bb_im = fr[:, None, :] * bi + fi[:, None, :] * br
    lam_r, lam_i = ar * delta, ai * delta
    tau = jnp.arange(t_ + 1, dtype=F32)
    pr = jnp.exp(tau[:, None, None] * lam_r) * jnp.cos(tau[:, None, None] * lam_i)
    pi = jnp.exp(tau[:, None, None] * lam_r) * jnp.sin(tau[:, None, None] * lam_i)
    qr = jnp.exp(lam_r[..., None] * tau) * jnp.cos(lam_i[..., None] * tau)
    qi = jnp.exp(lam_r[..., None] * tau) * jnp.sin(lam_i[..., None] * tau)
    abr = pr[:, :, None, :] * bb_re - pi[:, :, None, :] * bb_im
    abi = pr[:, :, None, :] * bb_im + pi[:, :, None, :] * bb_re
    cr, ci = c_re.astype(F32), c_im.astype(F32)
    hp = lax.Precision.HIGHEST
    km = (jnp.einsum('tgcp,gdp->gctd', abr[:t_], cr, precision=hp)
          - jnp.einsum('tgcp,gdp->gctd', abi[:t_], ci, precision=hp))
    tt, jj, ii = np.arange(t_)[:, None, None], np.arange(t_)[None, :, None], np.arange(t_)[None, None, :]
    shift = np.einsum('tji,de->tdjie', (tt == ii - jj).astype(np.float32), np.eye(gc, dtype=np.float32))
    shift = jnp.asarray(shift.reshape(t_ * gc, t_ * t_ * gc), BF16)
    r_t = jnp.dot(km.astype(BF16).reshape(S5_GROUPS * gc, t_ * gc), shift, preferred_element_type=BF16)
    r_t = r_t.reshape(S5_LT, S5_GPT, gc, t_, t_ * gc).transpose(0, 3, 1, 2, 4).reshape(S5_LT, t_ * LANE, t_ * gc)
    rev = np.arange(t_ - 1, -1, -1)
    r_w = jnp.concatenate([abr[rev], abi[rev]], axis=-1).astype(BF16)
    r_w = r_w.reshape(t_, S5_LT, S5_GPT, gc, 2 * gp).transpose(1, 0, 2, 3, 4).reshape(S5_LT, t_ * LANE, 2 * gp)
    crt, cit = cr.transpose(0, 2, 1)[:, :, None, :], ci.transpose(0, 2, 1)[:, :, None, :]
    q1r, q1i = qr[:, :, 1:, None], qi[:, :, 1:, None]
    vr = (crt * q1r - cit * q1i).astype(BF16).reshape(S5_LT, S5_SW, t_ * gc)
    vi = (-(crt * q1i + cit * q1r)).astype(BF16).reshape(S5_LT, S5_SW, t_ * gc)
    r_v = jnp.concatenate([vr, vi], axis=1)
    src, dst = np.arange(t_ * gc), np.arange(t_ * LANE)
    rep = jnp.asarray((src[:, None] // gc == dst[None, :] // LANE) & (src[:, None] % gc == dst[None, :] % gc), BF16)
    src, dst = np.arange(2 * gp), np.arange(2 * S5_SW)
    repw = jnp.asarray((src[:, None] // gp == dst[None, :] // S5_SW) & (src[:, None] % gp == dst[None, :] % gp), BF16)
    io_group = lambda x: (x // gc) % S5_GPT
    st_group = lambda x: (x % S5_SW) // gp
    toep = _s5_expand(r_t, rep, io_group, io_group)
    wz = _s5_expand(r_w, repw, io_group, st_group)
    vv = _s5_expand(r_v, rep, st_group, io_group)
    a_t = jnp.stack([pr[t_].reshape(S5_LT, S5_SW), pi[t_].reshape(S5_LT, S5_SW)], axis=1)
    return toep, wz, vv, a_t


def _s5_body(u_ref, toep_ref, wz_ref, vv_ref, at_ref, o_ref, z_ref, xp_ref, st_ref, *, n_rows):
    @pl.when(pl.program_id(2) == 0)
    def _():
        st_ref[...] = jnp.zeros_like(st_ref)

    fold = lambda j: pl.ds(j, n_rows, stride=S5_T)
    ub = jnp.concatenate([u_ref[fold(j), :] for j in range(S5_T)], axis=-1).astype(BF16)
    z_ref[...] = _dot(ub, wz_ref[...])
    ar, ai = at_ref[0:1, :], at_ref[1:2, :]
    re, im = slice(0, S5_SW), slice(S5_SW, 2 * S5_SW)

    def step(c, carry):
        xr, xi = carry
        row = pl.ds(c, 1)
        xp_ref[row, re] = xr
        xp_ref[row, im] = xi
        return (xr * ar - xi * ai + z_ref[row, re], xr * ai + xi * ar + z_ref[row, im])

    xr, xi = lax.fori_loop(0, n_rows, step, (st_ref[0:1, re], st_ref[0:1, im]), unroll=8)
    st_ref[0:1, re] = xr
    st_ref[0:1, im] = xi
    y = _dot(ub, toep_ref[...]) + _dot(xp_ref[...].astype(BF16), vv_ref[...])
    for i in range(S5_T):
        o_ref[fold(i), :] = y[:, i * LANE:(i + 1) * LANE]


def _s5(u4, consts, layer, batch, n_rows):
    toep, wz, vv, a_t = consts
    _, n, w = u4.shape
    per_batch = n // batch // (n_rows * S5_T)
    blk = pl.BlockSpec((None, n_rows * S5_T, w), lambda k, bi, i: (k, bi * per_batch + i, 0))
    wspec = lambda a: pl.BlockSpec((None, None) + a.shape[2:], lambda k, bi, i: (layer, k, 0, 0))
    return pl.pallas_call(
        functools.partial(_s5_body, n_rows=n_rows),
        grid=(S5_LT, batch, per_batch),
        in_specs=[blk, wspec(toep), wspec(wz), wspec(vv), wspec(a_t)],
        out_specs=blk,
        out_shape=jax.ShapeDtypeStruct(u4.shape, F32),
        scratch_shapes=[pltpu.VMEM((n_rows, 2 * S5_SW), F32), pltpu.VMEM((n_rows, 2 * S5_SW), F32),
                        pltpu.VMEM((SUBLANE, 2 * S5_SW), F32)],
        compiler_params=_params(("arbitrary", "arbitrary", "arbitrary")),
        name="s5",
    )(u4, toep, wz, vv, a_t)


def _outproj_body(h_ref, oa_ref, ob_ref, *rest):
    y_refs, u_refs = rest[0:S5_LT], rest[S5_LT:2 * S5_LT]
    d_ref, gw_ref, gb_ref, wo_ref, g_ref, o_ref = rest[2 * S5_LT:]
    yc = jnp.concatenate([r[...] for r in y_refs], axis=-1)
    u = jnp.concatenate([r[...] for r in u_refs], axis=-1)
    y = jax.nn.gelu(yc + d_ref[...] * u)
    oc = y * jax.nn.sigmoid(_dot(y.astype(BF16), gw_ref[...]) + gb_ref[...])
    mix = (_dot(oa_ref[...].astype(BF16), wo_ref[0:D_HG, :])
           + _dot(ob_ref[...].astype(BF16), wo_ref[D_HG:D_HG + D_SSD, :])
           + _dot(oc.astype(BF16), wo_ref[D_HG + D_SSD:D_MIX, :]))
    o_ref[...] = h_ref[...] + _rms(mix, g_ref[...])


def _outproj(h, oa, ob, y4, u4, s5_d, glu_w, glu_b, w_out, ng, layer, tm):
    n, d = h.shape
    tok = lambda w: pl.BlockSpec((tm, w), lambda i: (i, 0))
    tile = lambda k: pl.BlockSpec((None, tm, LANE), lambda i: (k, i, 0))
    tiles = [tile(k) for k in range(S5_LT)]
    return pl.pallas_call(
        _outproj_body,
        grid=(n // tm,),
        in_specs=[tok(d), tok(D_HG), tok(D_SSD)] + tiles + tiles + [
            pl.BlockSpec((None, 1, D_S5), lambda i: (layer, 0, 0)),
            pl.BlockSpec((None, D_S5, D_S5), lambda i: (layer, 0, 0)),
            pl.BlockSpec((None, 1, D_S5), lambda i: (layer, 0, 0)),
            pl.BlockSpec((None, D_MIX, d), lambda i: (layer, 0, 0)),
            pl.BlockSpec((None, 1, d), lambda i: (layer * 6 + 3, 0, 0)),
        ],
        out_specs=tok(d),
        out_shape=jax.ShapeDtypeStruct((n, d), F32),
        compiler_params=_params(("arbitrary",)),
        name="outproj",
    )(h, oa, ob, *([y4] * S5_LT), *([u4] * S5_LT), s5_d, glu_w, glu_b, w_out, ng)


def _tile(n, pref):
    return pref if n % pref == 0 else n


def kernel(x, norm_g, ffn_w_gate, ffn_w_up, ffn_w_down, w_in, w_out, hg_lb_logits, hg_gnorm, ssd_conv_w, ssd_conv_b,
           ssd_dt_bias, ssd_A_log, ssd_D, ssd_norm, s5_A_re, s5_A_im, s5_B_re, s5_B_im, s5_C_re, s5_C_im, s5_D,
           s5_log_dt, s5_glu_w, s5_glu_b):
    bsz, seq, d = x.shape
    depth = norm_g.shape[0]
    n = bsz * seq
    assert d == D_MODEL and seq % (CHUNK * S5_T) == 0 and w_in.shape[-1] == COL_DT + SSD_HEADS

    ng = norm_g.astype(F32).reshape(depth * 6, 1, d)
    wg, wu, wd = ffn_w_gate.astype(BF16), ffn_w_up.astype(BF16), ffn_w_down.astype(BF16)
    o = 4 * D_HG
    w_in_p = jnp.concatenate([
        w_in[..., 0:o], w_in[..., o + D_SSD:o + D_SSD + D_XBC], w_in[..., o:o + D_SSD],
        w_in[..., o + D_SSD + D_XBC + SSD_HEADS:], w_in[..., o + D_SSD + D_XBC:o + D_SSD + D_XBC + SSD_HEADS],
        jnp.zeros(w_in.shape[:-1] + (LANE - SSD_HEADS,), w_in.dtype)], axis=-1).astype(BF16)
    w_out_b = w_out.astype(BF16)
    glu_w_b = s5_glu_w.astype(BF16)
    lb = jnp.cumsum(jax.nn.softmax(hg_lb_logits.astype(F32), axis=0), axis=0)
    lb = (lb - lb[:1])[:, None, :]
    loglb, l1m, omlb = jnp.log(lb), jnp.log1p(-lb), 1.0 - lb
    gn = hg_gnorm.astype(F32)[:, None, :]
    a_rep = jnp.repeat(-jnp.exp(ssd_A_log.astype(F32)), SSD_HEADDIM, axis=-1)[:, None, :]
    d_rep = jnp.repeat(ssd_D.astype(F32), SSD_HEADDIM, axis=-1)[:, None, :]
    dtb = jnp.pad(ssd_dt_bias.astype(F32), ((0, 0), (0, LANE - SSD_HEADS)))[:, None, :]
    cw, cb, nw = ssd_conv_w.astype(F32), ssd_conv_b.astype(F32)[:, None, :], ssd_norm.astype(F32)[:, None, :]
    s5_c = jax.vmap(_s5_consts)(s5_A_re, s5_A_im, s5_B_re, s5_B_im, s5_C_re, s5_C_im, s5_log_dt)
    s5_d, glu_b = s5_D.astype(F32)[:, None, :], s5_glu_b.astype(F32)[:, None, :]
    hg_c = _hgrn_consts()
    ssd_c = _ssd_consts()

    tm = _tile(n, 512)
    tl_h = _tile(seq, 256)
    tl_s = _tile(seq, 256)
    s5_rows = _tile(seq // S5_T, 256)

    h = x.astype(F32).reshape(n, d)
    for l in range(depth):
        h = _ffn(h, ng, wg, wu, wd, l, 0, tm, D_FF // 2)
        qfig, xbc, z, u4, dt = _inproj(h, ng, w_in_p, l, tm)
        tok3 = lambda a: a.reshape(bsz, seq, a.shape[-1])
        oa = _hgrn(tok3(qfig), hg_c, loglb[l], l1m[l], omlb[l], gn[l], tl_h)
        ob = _ssd(tok3(xbc), tok3(z), tok3(dt), ssd_c, cw[l], cb[l], dtb[l], a_rep[l], d_rep[l], nw[l], tl_s)
        y4 = _s5(u4, s5_c, l, bsz, s5_rows)
        h = _outproj(h, oa.reshape(n, D_HG), ob.reshape(n, D_SSD), y4, u4, s5_d, glu_w_b, glu_b, w_out_b, ng, l, tm)
        h = _ffn(h, ng, wg, wu, wd, l, 1, tm, D_FF // 2)
    return h.reshape(bsz, seq, d).astype(x.dtype)
```

```python
import functools

import numpy as np
import jax
import jax.numpy as jnp
from jax import lax
from jax.experimental import pallas as pl
from jax.experimental.pallas import tpu as pltpu

F32 = jnp.float32
BF16 = jnp.bfloat16
EPS = 1e-6

D_MODEL = 1024
D_FF = 2816
HG_HEADS, HG_DK = 4, 128
D_HG = HG_HEADS * HG_DK
SSD_HEADS, SSD_HEADDIM, SSD_GROUPS, SSD_STATE, SSD_CONV = 8, 64, 2, 128, 4
D_SSD = SSD_HEADS * SSD_HEADDIM
D_XBC = D_SSD + 2 * SSD_GROUPS * SSD_STATE
S5_GROUPS, S5_GROUP_CH, S5_STATE = 32, 16, 64
D_S5 = S5_GROUPS * S5_GROUP_CH
D_MIX = D_HG + D_SSD + D_S5
CHUNK = 64
LANE = 128
SUBLANE = 8
S5_T = 16
S5_LT = D_S5 // LANE
S5_GPT = LANE // S5_GROUP_CH
S5_SW = S5_GPT * S5_STATE

COL_XBC, COL_Z, COL_U, COL_DT = 4 * D_HG, 4 * D_HG + D_XBC, 4 * D_HG + D_XBC + D_SSD, 4 * D_HG + D_XBC + D_SSD + D_S5
D_PROJ = COL_DT + LANE
VMEM_LIMIT = 56 * 1024 * 1024


def _dot(a, b):
    return jnp.dot(a, b, preferred_element_type=F32)


def _dot_nt(a, b):
    return lax.dot_general(a, b, (((1,), (1,)), ((), ())), preferred_element_type=F32)


def _dot_tn(a, b):
    return lax.dot_general(a, b, (((0,), (0,)), ((), ())), preferred_element_type=F32)


def _split(x):
    hi = x.astype(BF16)
    return hi, (x - hi.astype(F32)).astype(BF16)


def _dot_split(m2, x):
    hi, lo = _split(x)
    return _dot(m2, jnp.concatenate([hi, lo], axis=0))


def _dot_split_r(x, m2):
    hi, lo = _split(x)
    return _dot(jnp.concatenate([hi, lo], axis=1), m2)


def _rms(x, g):
    return x * lax.rsqrt(jnp.mean(x * x, axis=-1, keepdims=True) + EPS) * g


def _silu(x):
    return x * jax.nn.sigmoid(x)


def _params(sem):
    return pltpu.CompilerParams(dimension_semantics=sem, vmem_limit_bytes=VMEM_LIMIT)


def _full(a):
    return pl.BlockSpec(a.shape, lambda *_: (0,) * a.ndim)


def _ffn_body(x_ref, gpre_ref, wg_ref, wu_ref, wd_ref, gpost_ref, o_ref):
    x = x_ref[...]
    xn = _rms(x, gpre_ref[...]).astype(BF16)
    act = (_silu(_dot(xn, wg_ref[...])) * _dot(xn, wu_ref[...])).astype(BF16)
    o_ref[...] = x + 0.5 * _rms(_dot(act, wd_ref[...]), gpost_ref[...])


def _ffn(h, ng, wg, wu, wd, layer, which, tm):
    n, d = h.shape
    kpre, kpost = (0, 1) if which == 0 else (4, 5)
    weight = lambda r, c: pl.BlockSpec((None, None, r, c), lambda i: (layer, which, 0, 0),
                                       pipeline_mode=pl.Buffered(1))
    return pl.pallas_call(
        _ffn_body,
        grid=(n // tm,),
        in_specs=[
            pl.BlockSpec((tm, d), lambda i: (i, 0)),
            pl.BlockSpec((None, 1, d), lambda i: (layer * 6 + kpre, 0, 0)),
            weight(d, D_FF), weight(d, D_FF), weight(D_FF, d),
            pl.BlockSpec((None, 1, d), lambda i: (layer * 6 + kpost, 0, 0)),
        ],
        out_specs=pl.BlockSpec((tm, d), lambda i: (i, 0)),
        out_shape=jax.ShapeDtypeStruct((n, d), F32),
        compiler_params=_params(("arbitrary",)),
        name="ffn",
    )(h, ng, wg, wu, wd, ng)


def _inproj_body(x_ref, g_ref, w_ref, qfig_ref, xbc_ref, z_ref, u_ref, dt_ref):
    xn = _rms(x_ref[...], g_ref[...]).astype(BF16)
    qfig_ref[...] = _dot(xn, w_ref[:, 0:COL_XBC])
    xbc_ref[...] = _dot(xn, w_ref[:, COL_XBC:COL_Z])
    z_ref[...] = _dot(xn, w_ref[:, COL_Z:COL_U])
    u = _dot(xn, w_ref[:, COL_U:COL_DT])
    for k in range(S5_LT):
        u_ref[k] = u[:, k * LANE:(k + 1) * LANE]
    dt_ref[...] = _dot(xn, w_ref[:, COL_DT:D_PROJ])


def _inproj(h, ng, w_in_p, layer, tm):
    n, d = h.shape
    tok = lambda w: pl.BlockSpec((tm, w), lambda i: (i, 0))
    out_w = (4 * D_HG, D_XBC, D_SSD)
    return pl.pallas_call(
        _inproj_body,
        grid=(n // tm,),
        in_specs=[
            tok(d),
            pl.BlockSpec((None, 1, d), lambda i: (layer * 6 + 2, 0, 0)),
            pl.BlockSpec((None, d, D_PROJ), lambda i: (layer, 0, 0)),
        ],
        out_specs=[tok(w) for w in out_w] + [pl.BlockSpec((S5_LT, tm, LANE), lambda i: (0, i, 0)), tok(LANE)],
        out_shape=[jax.ShapeDtypeStruct((n, w), F32) for w in out_w]
        + [jax.ShapeDtypeStruct((S5_LT, n, LANE), F32), jax.ShapeDtypeStruct((n, LANE), F32)],
        compiler_params=_params(("arbitrary",)),
        name="inproj",
    )(h, ng, w_in_p)


HG_LEVELS = (32, 16, 8, 4, 2)


def _block_tri_up(m, n=CHUNK):
    t = np.arange(n)
    same = (t[:, None] // m) == (t[None, :] // m)
    tri = same & (t[None, :] <= t[:, None])
    up = same & (t[None, :] > t[:, None])
    return tri.astype(np.float32), up.astype(np.float32)


def _hgrn_consts():
    mats, masks = [], []
    for m in (CHUNK,) + HG_LEVELS:
        mats += list(_block_tri_up(m))
    t = np.arange(CHUNK)
    for m in HG_LEVELS + (1,):
        bt, bs = t[:, None] // m, t[None, :] // m
        masks.append(((bt % 2 == 1) & (bs == bt - 1)).astype(np.float32))
    masks.append(np.eye(CHUNK, dtype=np.float32))
    cm = np.concatenate(mats, 0)
    return jnp.asarray(np.concatenate([cm, cm], 1), BF16), jnp.asarray(np.stack(masks, 0), F32)


def _hgrn_body(p_ref, cm_ref, mask_ref, loglb_ref, l1m_ref, omlb_ref, gn_ref, o_ref, st_ref, *, n_chunks):
    @pl.when(pl.program_id(0) == 0)
    def _():
        st_ref[...] = jnp.zeros_like(st_ref)

    cm = cm_ref[...]
    loglb, l1m, omlb, gn = loglb_ref[...], l1m_ref[...], omlb_ref[...], gn_ref[...]
    nl = len(HG_LEVELS)

    def chunk_one(b, rows):
        q = _silu(p_ref[b, rows, 0:D_HG])
        fr = p_ref[b, rows, D_HG:2 * D_HG]
        v = p_ref[b, rows, 2 * D_HG:3 * D_HG].astype(BF16)
        gr = p_ref[b, rows, 3 * D_HG:4 * D_HG]
        ls = jnp.minimum(fr, 0.0) - jnp.log1p(jnp.exp(-jnp.abs(fr)))
        bb = l1m + ls
        lf = jnp.maximum(loglb, bb) + jnp.log1p(jnp.exp(-jnp.abs(loglb - bb)))
        kk = omlb * jnp.exp(ls - fr)
        cs = _dot_split(cm, lf)
        qb, kb = q.astype(BF16), kk.astype(BF16)
        qf = (q * jnp.exp(lf)).astype(BF16)
        outs = []
        for h in range(HG_HEADS):
            hs = slice(h * HG_DK, (h + 1) * HG_DK)
            qh, kh = q[:, hs], kk[:, hs]
            st = st_ref[b, h]
            c64, r64 = cs[0:CHUNK, hs], cs[CHUNK:2 * CHUNK, hs]
            scores = mask_ref[nl + 1] * _dot_nt(qb[:, hs], kb[:, hs]) + mask_ref[nl] * _dot_nt(qf[:, hs], kb[:, hs])
            for li in range(nl):
                base = 2 * CHUNK * (li + 1)
                cmm, rmm = cs[base:base + CHUNK, hs], cs[base + CHUNK:base + 2 * CHUNK, hs]
                p = _dot_nt((qh * jnp.exp(cmm)).astype(BF16), (kh * jnp.exp(rmm)).astype(BF16))
                scores = scores + mask_ref[li] * p
            o = _dot_nt((qh * jnp.exp(c64)).astype(BF16), st.astype(BF16)) + _dot(scores.astype(BF16), v[:, hs])
            ke = (kh * jnp.exp(r64)).astype(BF16)
            st_ref[b, h] = jnp.exp(c64[CHUNK - 1:CHUNK, :]) * st + _dot_tn(v[:, hs], ke)
            outs.append(o * lax.rsqrt(jnp.mean(o * o, axis=-1, keepdims=True) + EPS) * gn)
        o_ref[b, rows, :] = jnp.concatenate(outs, axis=-1) * _silu(gr)

    def chunk_step(c, carry):
        rows = pl.ds(pl.multiple_of(c * CHUNK, CHUNK), CHUNK)
        for b in range(p_ref.shape[0]):
            chunk_one(b, rows)
        return carry

    lax.fori_loop(0, n_chunks, chunk_step, 0, unroll=2)


def _hgrn(qfig3, consts, loglb, l1m, omlb, gn, tl):
    b, l, _ = qfig3.shape
    cm, masks = consts
    return pl.pallas_call(
        functools.partial(_hgrn_body, n_chunks=tl // CHUNK),
        grid=(l // tl,),
        in_specs=[pl.BlockSpec((b, tl, 4 * D_HG), lambda i: (0, i, 0)),
                  _full(cm), _full(masks), _full(loglb), _full(l1m), _full(omlb), _full(gn)],
        out_specs=pl.BlockSpec((b, tl, D_HG), lambda i: (0, i, 0)),
        out_shape=jax.ShapeDtypeStruct((b, l, D_HG), F32),
        scratch_shapes=[pltpu.VMEM((b, HG_HEADS, HG_DK, HG_DK), F32)],
        compiler_params=_params(("arbitrary",)),
        name="hgrn2",
    )(qfig3, cm, masks, loglb, l1m, omlb, gn)


def _ssd_consts():
    tri, up = _block_tri_up(CHUNK)
    t = np.arange(CHUNK)
    su = (t[:, None] > t[None, :]).astype(np.float32)
    causal = (t[:, None] >= t[None, :]).astype(np.float32)
    expand = np.zeros((LANE, D_SSD), np.float32)
    for h in range(SSD_HEADS):
        expand[h, h * SSD_HEADDIM:(h + 1) * SSD_HEADDIM] = 1.0
    tu = np.concatenate([tri, up], 0)
    return (jnp.asarray(np.concatenate([tu, tu], 1), BF16), jnp.asarray(np.tile(su, (1, SSD_HEADS)), F32),
            jnp.asarray(causal, F32), jnp.asarray(np.concatenate([expand, expand], 0), BF16))


def _ssd_body(xbc_ref, z_ref, dt_ref, cw_ref, cb_ref, dtb_ref, a_ref, d_ref, nw_ref, tu_ref, su_ref, causal_ref,
              ex_ref, o_ref, st_ref, ext_ref, *, tl):
    pad = SUBLANE
    nb = xbc_ref.shape[0]

    @pl.when(pl.program_id(0) == 0)
    def _():
        st_ref[...] = jnp.zeros_like(st_ref)
        ext_ref[:, 0:pad, :] = jnp.zeros((nb, pad, D_XBC), F32)

    ext_ref[:, pad:pad + tl, :] = xbc_ref[...]

    tu, su, causal, expand = tu_ref[...], su_ref[...], causal_ref[...], ex_ref[...]
    a_rep, d_rep, nw, cb, dtb = a_ref[...], d_ref[...], nw_ref[...], cb_ref[...], dtb_ref[...]
    gw = D_SSD // SSD_GROUPS
    hpg = SSD_HEADS // SSD_GROUPS

    def chunk_one(b, r0):
        rows = pl.ds(r0, CHUNK)
        win = ext_ref[b, pl.ds(r0, CHUNK + pad), :]
        conv = cb
        for j in range(SSD_CONV):
            lo = pad - (SSD_CONV - 1) + j
            conv = conv + cw_ref[j:j + 1, :] * win[lo:lo + CHUNK, :]
        xs = _silu(conv)
        x = xs[:, 0:D_SSD]
        dtin = dt_ref[b, rows, :] + dtb
        sp = jnp.maximum(dtin, 0.0) + jnp.log1p(jnp.exp(-jnp.abs(dtin)))
        dtr = _dot_split_r(sp, expand)
        adt = dtr * a_rep
        r = _dot_split(tu, jnp.concatenate([adt, adt * su], axis=-1))
        acs, rs = r[0:CHUNK, 0:D_SSD], r[CHUNK:2 * CHUNK, 0:D_SSD]
        dm = r[0:CHUNK, D_SSD:2 * D_SSD]
        xdt = x * dtr
        xdt_b = xdt.astype(BF16)
        ys = []
        for g in range(SSD_GROUPS):
            gs = slice(g * gw, (g + 1) * gw)
            bg = xs[:, D_SSD + g * SSD_STATE:D_SSD + (g + 1) * SSD_STATE].astype(BF16)
            cg = xs[:, D_SSD + (SSD_GROUPS + g) * SSD_STATE:D_SSD + (SSD_GROUPS + g + 1) * SSD_STATE].astype(BF16)
            st = st_ref[b, g]
            scores = _dot_nt(cg, bg) * causal
            y_off = _dot(cg, st.astype(BF16)) * jnp.exp(acs[:, gs])
            yd = []
            for hh in range(hpg):
                sl = slice((g * hpg + hh) * SSD_HEADDIM, (g * hpg + hh + 1) * SSD_HEADDIM)
                yd.append(_dot((scores * jnp.exp(dm[:, sl])).astype(BF16), xdt_b[:, sl]))
            ys.append(jnp.concatenate(yd, axis=-1) + y_off)
            xd = (xdt[:, gs] * jnp.exp(rs[:, gs])).astype(BF16)
            st_ref[b, g] = jnp.exp(acs[CHUNK - 1:CHUNK, gs]) * st + _dot_tn(bg, xd)
        y = jnp.concatenate(ys, axis=-1) + x * d_rep
        y = y * _silu(z_ref[b, rows, :])
        yn = [y[:, g * gw:(g + 1) * gw] * lax.rsqrt(
            jnp.mean(y[:, g * gw:(g + 1) * gw] * y[:, g * gw:(g + 1) * gw], axis=-1, keepdims=True) + EPS)
            for g in range(SSD_GROUPS)]
        o_ref[b, rows, :] = jnp.concatenate(yn, axis=-1) * nw

    def chunk_step(c, carry):
        r0 = pl.multiple_of(c * CHUNK, CHUNK)
        for b in range(nb):
            chunk_one(b, r0)
        return carry

    lax.fori_loop(0, tl // CHUNK, chunk_step, 0, unroll=2)
    ext_ref[:, 0:pad, :] = xbc_ref[:, tl - pad:tl, :]


def _ssd(xbc3, z3, dt3, consts, cw, cb, dtb, a_rep, d_rep, nw, tl):
    b, l, _ = xbc3.shape
    tok = lambda w: pl.BlockSpec((b, tl, w), lambda i: (0, i, 0))
    params = (cw, cb, dtb, a_rep, d_rep, nw) + tuple(consts)
    return pl.pallas_call(
        functools.partial(_ssd_body, tl=tl),
        grid=(l // tl,),
        in_specs=[tok(D_XBC), tok(D_SSD), tok(LANE)] + [_full(a) for a in params],
        out_specs=tok(D_SSD),
        out_shape=jax.ShapeDtypeStruct((b, l, D_SSD), F32),
        scratch_shapes=[
            pltpu.VMEM((b, SSD_GROUPS, SSD_STATE, D_SSD // SSD_GROUPS), F32),
            pltpu.VMEM((b, tl + SUBLANE, D_XBC), F32),
        ],
        compiler_params=_params(("arbitrary",)),
        name="ssd",
    )(xbc3, z3, dt3, *params)


def _s5_expand(r, rep, row_group, col_group):
    big = jnp.dot(r, rep, preferred_element_type=BF16)
    rows, cols = big.shape[-2:]
    rg = row_group(lax.broadcasted_iota(jnp.int32, (rows, cols), 0))
    cg = col_group(lax.broadcasted_iota(jnp.int32, (rows, cols), 1))
    return jnp.where(rg == cg, big, jnp.zeros_like(big))


def _s5_consts(a_re, a_im, b_re, b_im, c_re, c_im, log_dt):
    t_, gc, gp = S5_T, S5_GROUP_CH, S5_STATE
    ar, ai = a_re.astype(F32), a_im.astype(F32)
    delta = jnp.exp(log_dt.astype(F32))[:, None]
    mag = jnp.exp(ar * delta)
    ab_re, ab_im = mag * jnp.cos(ai * delta), mag * jnp.sin(ai * delta)
    den = ar * ar + ai * ai
    nr, ni = ab_re - 1.0, ab_im
    fr = (nr * ar + ni * ai) / den
    fi = (ni * ar - nr * ai) / den
    br, bi = b_re.astype(F32).transpose(0, 2, 1), b_im.astype(F32).transpose(0, 2, 1)
    bb_re = fr[:, None, :] * br - fi[:, None, :] * bi
    bb_im = fr[:, None, :] * bi + fi[:, None, :] * br
    lam_r, lam_i = ar * delta, ai * delta
    tau = jnp.arange(t_ + 1, dtype=F32)
    pr = jnp.exp(tau[:, None, None] * lam_r) * jnp.cos(tau[:, None, None] * lam_i)
    pi = jnp.exp(tau[:, None, None] * lam_r) * jnp.sin(tau[:, None, None] * lam_i)
    qr = jnp.exp(lam_r[..., None] * tau) * jnp.cos(lam_i[..., None] * tau)
    qi = jnp.exp(lam_r[..., None] * tau) * jnp.sin(lam_i[..., None] * tau)
    abr = pr[:, :, None, :] * bb_re - pi[:, :, None, :] * bb_im
    abi = pr[:, :, None, :] * bb_im + pi[:, :, None, :] * bb_re
    cr, ci = c_re.astype(F32), c_im.astype(F32)
    hp = lax.Precision.HIGHEST
    km = (jnp.einsum('tgcp,gdp->gctd', abr[:t_], cr, precision=hp)
          - jnp.einsum('tgcp,gdp->gctd', abi[:t_], ci, precision=hp))
    tt, jj, ii = np.arange(t_)[:, None, None], np.arange(t_)[None, :, None], np.arange(t_)[None, None, :]
    shift = np.einsum('tji,de->tdjie', (tt == ii - jj).astype(np.float32), np.eye(gc, dtype=np.float32))
    shift = jnp.asarray(shift.reshape(t_ * gc, t_ * t_ * gc), BF16)
    r_t = jnp.dot(km.astype(BF16).reshape(S5_GROUPS * gc, t_ * gc), shift, preferred_element_type=BF16)
    r_t = r_t.reshape(S5_LT, S5_GPT, gc, t_, t_ * gc).transpose(0, 3, 1, 2, 4).reshape(S5_LT, t_ * LANE, t_ * gc)
    rev = np.arange(t_ - 1, -1, -1)
    r_w = jnp.concatenate([abr[rev], abi[rev]], axis=-1).astype(BF16)
    r_w = r_w.reshape(t_, S5_LT, S5_GPT, gc, 2 * gp).transpose(1, 0, 2, 3, 4).reshape(S5_LT, t_ * LANE, 2 * gp)
    crt, cit = cr.transpose(0, 2, 1)[:, :, None, :], ci.transpose(0, 2, 1)[:, :, None, :]
    q1r, q1i = qr[:, :, 1:, None], qi[:, :, 1:, None]
    vr = (crt * q1r - cit * q1i).astype(BF16).reshape(S5_LT, S5_SW, t_ * gc)
    vi = (-(crt * q1i + cit * q1r)).astype(BF16).reshape(S5_LT, S5_SW, t_ * gc)
    r_v = jnp.concatenate([vr, vi], axis=1)
    src, dst = np.arange(t_ * gc), np.arange(t_ * LANE)
    rep = jnp.asarray((src[:, None] // gc == dst[None, :] // LANE) & (src[:, None] % gc == dst[None, :] % gc), BF16)
    src, dst = np.arange(2 * gp), np.arange(2 * S5_SW)
    repw = jnp.asarray((src[:, None] // gp == dst[None, :] // S5_SW) & (src[:, None] % gp == dst[None, :] % gp), BF16)
    io_group = lambda x: (x // gc) % S5_GPT
    st_group = lambda x: (x % S5_SW) // gp
    toep = _s5_expand(r_t, rep, io_group, io_group)
    wz = _s5_expand(r_w, repw, io_group, st_group)
    vv = _s5_expand(r_v, rep, st_group, io_group)
    a_t = jnp.stack([pr[t_].reshape(S5_LT, S5_SW), pi[t_].reshape(S5_LT, S5_SW)], axis=1)
    return toep, wz, vv, a_t


def _s5_body(u_ref, toep_ref, wz_ref, vv_ref, at_ref, o_ref, z_ref, xp_ref, st_ref, *, n_rows):
    @pl.when(pl.program_id(2) == 0)
    def _():
        st_ref[...] = jnp.zeros_like(st_ref)

    fold = lambda j: pl.ds(j, n_rows, stride=S5_T)
    ub = jnp.concatenate([u_ref[fold(j), :] for j in range(S5_T)], axis=-1).astype(BF16)
    z_ref[...] = _dot(ub, wz_ref[...])
    ar, ai = at_ref[0:1, :], at_ref[1:2, :]
    re, im = slice(0, S5_SW), slice(S5_SW, 2 * S5_SW)

    def step(c, carry):
        xr, xi = carry
        row = pl.ds(c, 1)
        xp_ref[row, re] = xr
        xp_ref[row, im] = xi
        return (xr * ar - xi * ai + z_ref[row, re], xr * ai + xi * ar + z_ref[row, im])

    xr, xi = lax.fori_loop(0, n_rows, step, (st_ref[0:1, re], st_ref[0:1, im]), unroll=8)
    st_ref[0:1, re] = xr
    st_ref[0:1, im] = xi
    y = _dot(ub, toep_ref[...]) + _dot(xp_ref[...].astype(BF16), vv_ref[...])
    for i in range(S5_T):
        o_ref[fold(i), :] = y[:, i * LANE:(i + 1) * LANE]


def _s5(u4, consts, layer, batch, n_rows):
    toep, wz, vv, a_t = consts
    _, n, w = u4.shape
    per_batch = n // batch // (n_rows * S5_T)
    blk = pl.BlockSpec((None, n_rows * S5_T, w), lambda k, bi, i: (k, bi * per_batch + i, 0))
    wspec = lambda a: pl.BlockSpec((None, None) + a.shape[2:], lambda k, bi, i: (layer, k, 0, 0))
    return pl.pallas_call(
        functools.partial(_s5_body, n_rows=n_rows),
        grid=(S5_LT, batch, per_batch),
        in_specs=[blk, wspec(toep), wspec(wz), wspec(vv), wspec(a_t)],
        out_specs=blk,
        out_shape=jax.ShapeDtypeStruct(u4.shape, F32),
        scratch_shapes=[pltpu.VMEM((n_rows, 2 * S5_SW), F32), pltpu.VMEM((n_rows, 2 * S5_SW), F32),
                        pltpu.VMEM((SUBLANE, 2 * S5_SW), F32)],
        compiler_params=_params(("arbitrary", "arbitrary", "arbitrary")),
        name="s5",
    )(u4, toep, wz, vv, a_t)


def _outproj_body(h_ref, oa_ref, ob_ref, *rest):
    y_refs, u_refs = rest[0:S5_LT], rest[S5_LT:2 * S5_LT]
    d_ref, gw_ref, gb_ref, wo_ref, g_ref, o_ref = rest[2 * S5_LT:]
    yc = jnp.concatenate([r[...] for r in y_refs], axis=-1)
    u = jnp.concatenate([r[...] for r in u_refs], axis=-1)
    y = jax.nn.gelu(yc + d_ref[...] * u)
    oc = y * jax.nn.sigmoid(_dot(y.astype(BF16), gw_ref[...]) + gb_ref[...])
    mix = (_dot(oa_ref[...].astype(BF16), wo_ref[0:D_HG, :])
           + _dot(ob_ref[...].astype(BF16), wo_ref[D_HG:D_HG + D_SSD, :])
           + _dot(oc.astype(BF16), wo_ref[D_HG + D_SSD:D_MIX, :]))
    o_ref[...] = h_ref[...] + _rms(mix, g_ref[...])


def _outproj(h, oa, ob, y4, u4, s5_d, glu_w, glu_b, w_out, ng, layer, tm):
    n, d = h.shape
    tok = lambda w: pl.BlockSpec((tm, w), lambda i: (i, 0))
    tile = lambda k: pl.BlockSpec((None, tm, LANE), lambda i: (k, i, 0))
    tiles = [tile(k) for k in range(S5_LT)]
    return pl.pallas_call(
        _outproj_body,
        grid=(n // tm,),
        in_specs=[tok(d), tok(D_HG), tok(D_SSD)] + tiles + tiles + [
            pl.BlockSpec((None, 1, D_S5), lambda i: (layer, 0, 0)),
            pl.BlockSpec((None, D_S5, D_S5), lambda i: (layer, 0, 0)),
            pl.BlockSpec((None, 1, D_S5), lambda i: (layer, 0, 0)),
            pl.BlockSpec((None, D_MIX, d), lambda i: (layer, 0, 0)),
            pl.BlockSpec((None, 1, d), lambda i: (layer * 6 + 3, 0, 0)),
        ],
        out_specs=tok(d),
        out_shape=jax.ShapeDtypeStruct((n, d), F32),
        compiler_params=_params(("arbitrary",)),
        name="outproj",
    )(h, oa, ob, *([y4] * S5_LT), *([u4] * S5_LT), s5_d, glu_w, glu_b, w_out, ng)


def _tile(n, pref):
    return pref if n % pref == 0 else n


def kernel(x, norm_g, ffn_w_gate, ffn_w_up, ffn_w_down, w_in, w_out, hg_lb_logits, hg_gnorm, ssd_conv_w, ssd_conv_b,
           ssd_dt_bias, ssd_A_log, ssd_D, ssd_norm, s5_A_re, s5_A_im, s5_B_re, s5_B_im, s5_C_re, s5_C_im, s5_D,
           s5_log_dt, s5_glu_w, s5_glu_b):
    bsz, seq, d = x.shape
    depth = norm_g.shape[0]
    n = bsz * seq
    assert d == D_MODEL and seq % (CHUNK * S5_T) == 0 and w_in.shape[-1] == COL_DT + SSD_HEADS

    ng = norm_g.astype(F32).reshape(depth * 6, 1, d)
    wg, wu, wd = ffn_w_gate.astype(BF16), ffn_w_up.astype(BF16), ffn_w_down.astype(BF16)
    o = 4 * D_HG
    w_in_p = jnp.concatenate([
        w_in[..., 0:o], w_in[..., o + D_SSD:o + D_SSD + D_XBC], w_in[..., o:o + D_SSD],
        w_in[..., o + D_SSD + D_XBC + SSD_HEADS:], w_in[..., o + D_SSD + D_XBC:o + D_SSD + D_XBC + SSD_HEADS],
        jnp.zeros(w_in.shape[:-1] + (LANE - SSD_HEADS,), w_in.dtype)], axis=-1).astype(BF16)
    w_out_b = w_out.astype(BF16)
    glu_w_b = s5_glu_w.astype(BF16)
    lb = jnp.cumsum(jax.nn.softmax(hg_lb_logits.astype(F32), axis=0), axis=0)
    lb = (lb - lb[:1])[:, None, :]
    loglb, l1m, omlb = jnp.log(lb), jnp.log1p(-lb), 1.0 - lb
    gn = hg_gnorm.astype(F32)[:, None, :]
    a_rep = jnp.repeat(-jnp.exp(ssd_A_log.astype(F32)), SSD_HEADDIM, axis=-1)[:, None, :]
    d_rep = jnp.repeat(ssd_D.astype(F32), SSD_HEADDIM, axis=-1)[:, None, :]
    dtb = jnp.pad(ssd_dt_bias.astype(F32), ((0, 0), (0, LANE - SSD_HEADS)))[:, None, :]
    cw, cb, nw = ssd_conv_w.astype(F32), ssd_conv_b.astype(F32)[:, None, :], ssd_norm.astype(F32)[:, None, :]
    s5_c = jax.vmap(_s5_consts)(s5_A_re, s5_A_im, s5_B_re, s5_B_im, s5_C_re, s5_C_im, s5_log_dt)
    s5_d, glu_b = s5_D.astype(F32)[:, None, :], s5_glu_b.astype(F32)[:, None, :]
    hg_c = _hgrn_consts()
    ssd_c = _ssd_consts()

    tm = _tile(n, 512)
    tl_h = _tile(seq, 256)
    tl_s = _tile(seq, 256)
    s5_rows = _tile(seq // S5_T, 256)

    h = x.astype(F32).reshape(n, d)
    for l in range(depth):
        h = _ffn(h, ng, wg, wu, wd, l, 0, tm)
        qfig, xbc, z, u4, dt = _inproj(h, ng, w_in_p, l, tm)
        tok3 = lambda a: a.reshape(bsz, seq, a.shape[-1])
        oa = _hgrn(tok3(qfig), hg_c, loglb[l], l1m[l], omlb[l], gn[l], tl_h)
        ob = _ssd(tok3(xbc), tok3(z), tok3(dt), ssd_c, cw[l], cb[l], dtb[l], a_rep[l], d_rep[l], nw[l], tl_s)
        y4 = _s5(u4, s5_c, l, bsz, s5_rows)
        h = _outproj(h, oa.reshape(n, D_HG), ob.reshape(n, D_SSD), y4, u4, s5_d, glu_w_b, glu_b, w_out_b, ng, l, tm)
        h = _ffn(h, ng, wg, wu, wd, l, 1, tm)
    return h.reshape(bsz, seq, d).astype(x.dtype)
```

```python
import functools

import numpy as np
import jax
import jax.numpy as jnp
from jax import lax
from jax.experimental import pallas as pl
from jax.experimental.pallas import tpu as pltpu

F32 = jnp.float32
BF16 = jnp.bfloat16
EPS = 1e-6
LOG2E = 1.4426950408889634

D_MODEL = 1024
D_FF = 2816
HG_HEADS, HG_DK = 4, 128
D_HG = HG_HEADS * HG_DK
SSD_HEADS, SSD_HEADDIM, SSD_GROUPS, SSD_STATE, SSD_CONV = 8, 64, 2, 128, 4
D_SSD = SSD_HEADS * SSD_HEADDIM
D_XBC = D_SSD + 2 * SSD_GROUPS * SSD_STATE
S5_GROUPS, S5_GROUP_CH, S5_STATE = 32, 16, 64
D_S5 = S5_GROUPS * S5_GROUP_CH
D_MIX = D_HG + D_SSD + D_S5
CHUNK = 64
LANE = 128
SUBLANE = 8
S5_T = 8
S5_LT = D_S5 // LANE
S5_GPT = LANE // S5_GROUP_CH
S5_SW = S5_GPT * S5_STATE

COL_XBC, COL_Z, COL_U, COL_DT = 4 * D_HG, 4 * D_HG + D_XBC, 4 * D_HG + D_XBC + D_SSD, 4 * D_HG + D_XBC + D_SSD + D_S5
D_PROJ = COL_DT + LANE
VMEM_LIMIT = 56 * 1024 * 1024


def _dot(a, b):
    return jnp.dot(a, b, preferred_element_type=F32)


def _dot_nt(a, b):
    return lax.dot_general(a, b, (((1,), (1,)), ((), ())), preferred_element_type=F32)


def _dot_tn(a, b):
    return lax.dot_general(a, b, (((0,), (0,)), ((), ())), preferred_element_type=F32)


def _split(x):
    hi = x.astype(BF16)
    return hi, (x - hi.astype(F32)).astype(BF16)


def _dot_split(m2, x):
    hi, lo = _split(x)
    return _dot(m2, jnp.concatenate([hi, lo], axis=0))


def _dot_split_r(x, m2):
    hi, lo = _split(x)
    return _dot(jnp.concatenate([hi, lo], axis=1), m2)


def _rms(x, g):
    return x * lax.rsqrt(jnp.mean(x * x, axis=-1, keepdims=True) + EPS) * g


def _silu(x):
    return x * jax.nn.sigmoid(x)


def _params(sem):
    return pltpu.CompilerParams(dimension_semantics=sem, vmem_limit_bytes=VMEM_LIMIT)


def _full(a):
    return pl.BlockSpec(a.shape, lambda *_: (0,) * a.ndim)


def _ffn_body(x_ref, gpre_ref, wg_ref, wu_ref, wd_ref, gpost_ref, o_ref):
    x = x_ref[...]
    xn = _rms(x, gpre_ref[...]).astype(BF16)
    act = (_silu(_dot(xn, wg_ref[...])) * _dot(xn, wu_ref[...])).astype(BF16)
    o_ref[...] = x + 0.5 * _rms(_dot(act, wd_ref[...]), gpost_ref[...])


def _ffn(h, ng, wg, wu, wd, layer, which, tm):
    n, d = h.shape
    kpre, kpost = (0, 1) if which == 0 else (4, 5)
    weight = lambda r, c: pl.BlockSpec((None, None, r, c), lambda i: (layer, which, 0, 0),
                                       pipeline_mode=pl.Buffered(1))
    return pl.pallas_call(
        _ffn_body,
        grid=(n // tm,),
        in_specs=[
            pl.BlockSpec((tm, d), lambda i: (i, 0)),
            pl.BlockSpec((None, 1, d), lambda i: (layer * 6 + kpre, 0, 0)),
            weight(d, D_FF), weight(d, D_FF), weight(D_FF, d),
            pl.BlockSpec((None, 1, d), lambda i: (layer * 6 + kpost, 0, 0)),
        ],
        out_specs=pl.BlockSpec((tm, d), lambda i: (i, 0)),
        out_shape=jax.ShapeDtypeStruct((n, d), F32),
        compiler_params=_params(("arbitrary",)),
        name="ffn",
    )(h, ng, wg, wu, wd, ng)


def _inproj_body(x_ref, g_ref, w_ref, qfig_ref, xbc_ref, z_ref, u_ref, dt_ref):
    xn = _rms(x_ref[...], g_ref[...]).astype(BF16)
    qfig_ref[...] = _dot(xn, w_ref[:, 0:COL_XBC])
    xbc_ref[...] = _dot(xn, w_ref[:, COL_XBC:COL_Z])
    z_ref[...] = _dot(xn, w_ref[:, COL_Z:COL_U])
    u = _dot(xn, w_ref[:, COL_U:COL_DT])
    for k in range(S5_LT):
        u_ref[k] = u[:, k * LANE:(k + 1) * LANE]
    dt_ref[...] = _dot(xn, w_ref[:, COL_DT:D_PROJ])


def _inproj(h, ng, w_in_p, layer, tm):
    n, d = h.shape
    tok = lambda w: pl.BlockSpec((tm, w), lambda i: (i, 0))
    out_w = (4 * D_HG, D_XBC, D_SSD)
    return pl.pallas_call(
        _inproj_body,
        grid=(n // tm,),
        in_specs=[
            tok(d),
            pl.BlockSpec((None, 1, d), lambda i: (layer * 6 + 2, 0, 0)),
            pl.BlockSpec((None, d, D_PROJ), lambda i: (layer, 0, 0)),
        ],
        out_specs=[tok(w) for w in out_w] + [pl.BlockSpec((S5_LT, tm, LANE), lambda i: (0, i, 0)), tok(LANE)],
        out_shape=[jax.ShapeDtypeStruct((n, w), F32) for w in out_w]
        + [jax.ShapeDtypeStruct((S5_LT, n, LANE), F32), jax.ShapeDtypeStruct((n, LANE), F32)],
        compiler_params=_params(("arbitrary",)),
        name="inproj",
    )(h, ng, w_in_p)


HG_LEVELS = (32, 16, 8, 4, 2)


def _block_tri_up(m, n=CHUNK):
    t = np.arange(n)
    same = (t[:, None] // m) == (t[None, :] // m)
    tri = same & (t[None, :] <= t[:, None])
    up = same & (t[None, :] > t[:, None])
    return tri.astype(np.float32), up.astype(np.float32)


def _hgrn_consts():
    mats, masks = [], []
    for m in (CHUNK,) + HG_LEVELS:
        mats += list(_block_tri_up(m))
    t = np.arange(CHUNK)
    for m in HG_LEVELS + (1,):
        bt, bs = t[:, None] // m, t[None, :] // m
        masks.append(((bt % 2 == 1) & (bs == bt - 1)).astype(np.float32))
    masks.append(np.eye(CHUNK, dtype=np.float32))
    cm = np.concatenate(mats, 0)
    return jnp.asarray(np.concatenate([cm, cm], 1), BF16), jnp.asarray(np.stack(masks, 0), F32)


def _hgrn_body(p_ref, cm_ref, mask_ref, loglb_ref, l1m_ref, omlb_ref, gn_ref, o_ref, st_ref, *, n_chunks):
    @pl.when(pl.program_id(0) == 0)
    def _():
        st_ref[...] = jnp.zeros_like(st_ref)

    cm = cm_ref[...]
    loglb, l1m, omlb, gn = loglb_ref[...], l1m_ref[...], omlb_ref[...], gn_ref[...]
    nl = len(HG_LEVELS)

    def chunk_one(b, rows):
        q = _silu(p_ref[b, rows, 0:D_HG])
        fr = p_ref[b, rows, D_HG:2 * D_HG]
        v = p_ref[b, rows, 2 * D_HG:3 * D_HG].astype(BF16)
        gr = p_ref[b, rows, 3 * D_HG:4 * D_HG]
        ls = jnp.minimum(fr, 0.0) - jnp.log1p(jnp.exp(-jnp.abs(fr)))
        bb = l1m + ls
        lf = jnp.maximum(loglb, bb) + jnp.log1p(jnp.exp(-jnp.abs(loglb - bb)))
        kk = omlb * jnp.exp(ls - fr)
        lf2 = lf * LOG2E
        cs = _dot_split(cm, lf2)
        qb, kb = q.astype(BF16), kk.astype(BF16)
        qf = (q * jnp.exp2(lf2)).astype(BF16)
        outs = []
        for h in range(HG_HEADS):
            hs = slice(h * HG_DK, (h + 1) * HG_DK)
            qh, kh = q[:, hs], kk[:, hs]
            st = st_ref[b, h]
            c64, r64 = cs[0:CHUNK, hs], cs[CHUNK:2 * CHUNK, hs]
            scores = mask_ref[nl + 1] * _dot_nt(qb[:, hs], kb[:, hs]) + mask_ref[nl] * _dot_nt(qf[:, hs], kb[:, hs])
            for li in range(nl):
                base = 2 * CHUNK * (li + 1)
                cmm, rmm = cs[base:base + CHUNK, hs], cs[base + CHUNK:base + 2 * CHUNK, hs]
                p = _dot_nt((qh * jnp.exp2(cmm)).astype(BF16), (kh * jnp.exp2(rmm)).astype(BF16))
                scores = scores + mask_ref[li] * p
            o = _dot_nt((qh * jnp.exp2(c64)).astype(BF16), st.astype(BF16)) + _dot(scores.astype(BF16), v[:, hs])
            ke = (kh * jnp.exp2(r64)).astype(BF16)
            st_ref[b, h] = jnp.exp2(c64[CHUNK - 1:CHUNK, :]) * st + _dot_tn(v[:, hs], ke)
            outs.append(o * lax.rsqrt(jnp.mean(o * o, axis=-1, keepdims=True) + EPS) * gn)
        o_ref[b, rows, :] = (jnp.concatenate(outs, axis=-1) * _silu(gr)).astype(o_ref.dtype)

    def chunk_step(c, carry):
        rows = pl.ds(pl.multiple_of(c * CHUNK, CHUNK), CHUNK)
        for b in range(p_ref.shape[0]):
            chunk_one(b, rows)
        return carry

    lax.fori_loop(0, n_chunks, chunk_step, 0, unroll=2)


def _hgrn(qfig3, consts, loglb, l1m, omlb, gn, tl):
    b, l, _ = qfig3.shape
    cm, masks = consts
    return pl.pallas_call(
        functools.partial(_hgrn_body, n_chunks=tl // CHUNK),
        grid=(l // tl,),
        in_specs=[pl.BlockSpec((b, tl, 4 * D_HG), lambda i: (0, i, 0)),
                  _full(cm), _full(masks), _full(loglb), _full(l1m), _full(omlb), _full(gn)],
        out_specs=pl.BlockSpec((b, tl, D_HG), lambda i: (0, i, 0)),
        out_shape=jax.ShapeDtypeStruct((b, l, D_HG), BF16),
        scratch_shapes=[pltpu.VMEM((b, HG_HEADS, HG_DK, HG_DK), F32)],
        compiler_params=_params(("arbitrary",)),
        name="hgrn2",
    )(qfig3, cm, masks, loglb, l1m, omlb, gn)


def _ssd_consts():
    tri, up = _block_tri_up(CHUNK)
    t = np.arange(CHUNK)
    su = (t[:, None] > t[None, :]).astype(np.float32)
    causal = (t[:, None] >= t[None, :]).astype(np.float32)
    expand = np.zeros((LANE, D_SSD), np.float32)
    for h in range(SSD_HEADS):
        expand[h, h * SSD_HEADDIM:(h + 1) * SSD_HEADDIM] = 1.0
    tu = np.concatenate([tri, up], 0)
    return (jnp.asarray(np.concatenate([tu, tu], 1), BF16), jnp.asarray(np.tile(su, (1, SSD_HEADS)), F32),
            jnp.asarray(causal, F32), jnp.asarray(np.concatenate([expand, expand], 0), BF16))


def _ssd_body(xbc_ref, z_ref, dt_ref, cw_ref, cb_ref, dtb_ref, a_ref, d_ref, nw_ref, tu_ref, su_ref, causal_ref,
              ex_ref, o_ref, st_ref, ext_ref, *, tl):
    pad = SUBLANE
    nb = xbc_ref.shape[0]

    @pl.when(pl.program_id(0) == 0)
    def _():
        st_ref[...] = jnp.zeros_like(st_ref)
        ext_ref[:, 0:pad, :] = jnp.zeros((nb, pad, D_XBC), F32)

    ext_ref[:, pad:pad + tl, :] = xbc_ref[...]

    tu, su, causal, expand = tu_ref[...], su_ref[...], causal_ref[...], ex_ref[...]
    a_rep, d_rep, nw, cb, dtb = a_ref[...], d_ref[...], nw_ref[...], cb_ref[...], dtb_ref[...]
    gw = D_SSD // SSD_GROUPS
    hpg = SSD_HEADS // SSD_GROUPS

    def chunk_one(b, r0):
        rows = pl.ds(r0, CHUNK)
        win = ext_ref[b, pl.ds(r0, CHUNK + pad), :]
        conv = cb + cw_ref[SSD_CONV - 1:SSD_CONV, :] * win[pad:pad + CHUNK, :]
        for j in range(SSD_CONV - 1):
            conv = conv + cw_ref[j:j + 1, :] * pltpu.roll(win, SSD_CONV - 1 - j, 0)[pad:pad + CHUNK, :]
        xs = _silu(conv)
        x = xs[:, 0:D_SSD]
        dtin = dt_ref[b, rows, :] + dtb
        sp = jnp.maximum(dtin, 0.0) + jnp.log1p(jnp.exp(-jnp.abs(dtin)))
        dtr = _dot_split_r(sp, expand)
        adt = dtr * a_rep
        r = _dot_split(tu, jnp.concatenate([adt, adt * su], axis=-1))
        acs, rs = r[0:CHUNK, 0:D_SSD], r[CHUNK:2 * CHUNK, 0:D_SSD]
        dm = r[0:CHUNK, D_SSD:2 * D_SSD]
        xdt = x * dtr
        xdt_b = xdt.astype(BF16)
        ys = []
        for g in range(SSD_GROUPS):
            gs = slice(g * gw, (g + 1) * gw)
            bg = xs[:, D_SSD + g * SSD_STATE:D_SSD + (g + 1) * SSD_STATE].astype(BF16)
            cg = xs[:, D_SSD + (SSD_GROUPS + g) * SSD_STATE:D_SSD + (SSD_GROUPS + g + 1) * SSD_STATE].astype(BF16)
            st = st_ref[b, g]
            scores = _dot_nt(cg, bg) * causal
            y_off = _dot(cg, st.astype(BF16)) * jnp.exp2(acs[:, gs])
            yd = []
            for hh in range(hpg):
                sl = slice((g * hpg + hh) * SSD_HEADDIM, (g * hpg + hh + 1) * SSD_HEADDIM)
                yd.append(_dot((scores * jnp.exp2(dm[:, sl])).astype(BF16), xdt_b[:, sl]))
            ys.append(jnp.concatenate(yd, axis=-1) + y_off)
            xd = (xdt[:, gs] * jnp.exp2(rs[:, gs])).astype(BF16)
            st_ref[b, g] = jnp.exp2(acs[CHUNK - 1:CHUNK, gs]) * st + _dot_tn(bg, xd)
        y = jnp.concatenate(ys, axis=-1) + x * d_rep
        y = y * _silu(z_ref[b, rows, :])
        yn = [y[:, g * gw:(g + 1) * gw] * lax.rsqrt(
            jnp.mean(y[:, g * gw:(g + 1) * gw] * y[:, g * gw:(g + 1) * gw], axis=-1, keepdims=True) + EPS)
            for g in range(SSD_GROUPS)]
        o_ref[b, rows, :] = (jnp.concatenate(yn, axis=-1) * nw).astype(o_ref.dtype)

    def chunk_step(c, carry):
        r0 = pl.multiple_of(c * CHUNK, CHUNK)
        for b in range(nb):
            chunk_one(b, r0)
        return carry

    lax.fori_loop(0, tl // CHUNK, chunk_step, 0, unroll=2)
    ext_ref[:, 0:pad, :] = xbc_ref[:, tl - pad:tl, :]


def _ssd(xbc3, z3, dt3, consts, cw, cb, dtb, a_rep, d_rep, nw, tl):
    b, l, _ = xbc3.shape
    tok = lambda w: pl.BlockSpec((b, tl, w), lambda i: (0, i, 0))
    params = (cw, cb, dtb, a_rep, d_rep, nw) + tuple(consts)
    return pl.pallas_call(
        functools.partial(_ssd_body, tl=tl),
        grid=(l // tl,),
        in_specs=[tok(D_XBC), tok(D_SSD), tok(LANE)] + [_full(a) for a in params],
        out_specs=tok(D_SSD),
        out_shape=jax.ShapeDtypeStruct((b, l, D_SSD), BF16),
        scratch_shapes=[
            pltpu.VMEM((b, SSD_GROUPS, SSD_STATE, D_SSD // SSD_GROUPS), F32),
            pltpu.VMEM((b, tl + SUBLANE, D_XBC), F32),
        ],
        compiler_params=_params(("arbitrary",)),
        name="ssd",
    )(xbc3, z3, dt3, *params)


def _s5_expand(r, rep, row_group, col_group):
    big = jnp.dot(r, rep, preferred_element_type=BF16)
    rows, cols = big.shape[-2:]
    rg = row_group(lax.broadcasted_iota(jnp.int32, (rows, cols), 0))
    cg = col_group(lax.broadcasted_iota(jnp.int32, (rows, cols), 1))
    return jnp.where(rg == cg, big, jnp.zeros_like(big))


def _s5_consts(a_re, a_im, b_re, b_im, c_re, c_im, log_dt):
    t_, gc, gp = S5_T, S5_GROUP_CH, S5_STATE
    ar, ai = a_re.astype(F32), a_im.astype(F32)
    delta = jnp.exp(log_dt.astype(F32))[:, None]
    mag = jnp.exp(ar * delta)
    ab_re, ab_im = mag * jnp.cos(ai * delta), mag * jnp.sin(ai * delta)
    den = ar * ar + ai * ai
    nr, ni = ab_re - 1.0, ab_im
    fr = (nr * ar + ni * ai) / den
    fi = (ni * ar - nr * ai) / den
    br, bi = b_re.astype(F32).transpose(0, 2, 1), b_im.astype(F32).transpose(0, 2, 1)
    bb_re = fr[:, None, :] * br - fi[:, None, :] * bi
    bb_im = fr[:, None, :] * bi + fi[:, None, :] * br
    lam_r, lam_i = ar * delta, ai * delta
    tau = jnp.arange(t_ + 1, dtype=F32)
    pr = jnp.exp(tau[:, None, None] * lam_r) * jnp.cos(tau[:, None, None] * lam_i)
    pi = jnp.exp(tau[:, None, None] * lam_r) * jnp.sin(tau[:, None, None] * lam_i)
    qr = jnp.exp(lam_r[..., None] * tau) * jnp.cos(lam_i[..., None] * tau)
    qi = jnp.exp(lam_r[..., None] * tau) * jnp.sin(lam_i[..., None] * tau)
    abr = pr[:, :, None, :] * bb_re - pi[:, :, None, :] * bb_im
    abi = pr[:, :, None, :] * bb_im + pi[:, :, None, :] * bb_re
    cr, ci = c_re.astype(F32), c_im.astype(F32)
    hp = lax.Precision.HIGHEST
    km = (jnp.einsum('tgcp,gdp->gctd', abr[:t_], cr, precision=hp)
          - jnp.einsum('tgcp,gdp->gctd', abi[:t_], ci, precision=hp))
    tt, jj, ii = np.arange(t_)[:, None, None], np.arange(t_)[None, :, None], np.arange(t_)[None, None, :]
    shift = np.einsum('tji,de->tdjie', (tt == ii - jj).astype(np.float32), np.eye(gc, dtype=np.float32))
    shift = jnp.asarray(shift.reshape(t_ * gc, t_ * t_ * gc), BF16)
    r_t = jnp.dot(km.astype(BF16).reshape(S5_GROUPS * gc, t_ * gc), shift, preferred_element_type=BF16)
    r_t = r_t.reshape(S5_LT, S5_GPT, gc, t_, t_ * gc).transpose(0, 3, 1, 2, 4).reshape(S5_LT, t_ * LANE, t_ * gc)
    rev = np.arange(t_ - 1, -1, -1)
    r_w = jnp.concatenate([abr[rev], abi[rev]], axis=-1).astype(BF16)
    r_w = r_w.reshape(t_, S5_LT, S5_GPT, gc, 2 * gp).transpose(1, 0, 2, 3, 4).reshape(S5_LT, t_ * LANE, 2 * gp)
    crt, cit = cr.transpose(0, 2, 1)[:, :, None, :], ci.transpose(0, 2, 1)[:, :, None, :]
    q1r, q1i = qr[:, :, 1:, None], qi[:, :, 1:, None]
    vr = (crt * q1r - cit * q1i).astype(BF16).reshape(S5_LT, S5_SW, t_ * gc)
    vi = (-(crt * q1i + cit * q1r)).astype(BF16).reshape(S5_LT, S5_SW, t_ * gc)
    r_v = jnp.concatenate([vr, vi], axis=1)
    src, dst = np.arange(t_ * gc), np.arange(t_ * LANE)
    rep = jnp.asarray((src[:, None] // gc == dst[None, :] // LANE) & (src[:, None] % gc == dst[None, :] % gc), BF16)
    src, dst = np.arange(2 * gp), np.arange(2 * S5_SW)
    repw = jnp.asarray((src[:, None] // gp == dst[None, :] // S5_SW) & (src[:, None] % gp == dst[None, :] % gp), BF16)
    io_group = lambda x: (x // gc) % S5_GPT
    st_group = lambda x: (x % S5_SW) // gp
    toep = _s5_expand(r_t, rep, io_group, io_group)
    wz = _s5_expand(r_w, repw, io_group, st_group)
    vv = _s5_expand(r_v, rep, st_group, io_group)
    a_t = jnp.stack([pr[t_].reshape(S5_LT, S5_SW), pi[t_].reshape(S5_LT, S5_SW)], axis=1)
    return toep, wz, vv, a_t


def _s5_body(u_ref, toep_ref, wz_ref, vv_ref, at_ref, o_ref, z_ref, xp_ref, st_ref, *, n_rows):
    @pl.when(pl.program_id(2) == 0)
    def _():
        st_ref[...] = jnp.zeros_like(st_ref)

    fold = lambda j: pl.ds(j, n_rows, stride=S5_T)
    ub = jnp.concatenate([u_ref[fold(j), :] for j in range(S5_T)], axis=-1).astype(BF16)
    z_ref[...] = _dot(ub, wz_ref[...])
    ar, ai = at_ref[0:1, :], at_ref[1:2, :]
    re, im = slice(0, S5_SW), slice(S5_SW, 2 * S5_SW)

    def step(c, carry):
        xr, xi = carry
        row = pl.ds(c, 1)
        xp_ref[row, re] = xr
        xp_ref[row, im] = xi
        return (xr * ar - xi * ai + z_ref[row, re], xr * ai + xi * ar + z_ref[row, im])

    xr, xi = lax.fori_loop(0, n_rows, step, (st_ref[0:1, re], st_ref[0:1, im]), unroll=8)
    st_ref[0:1, re] = xr
    st_ref[0:1, im] = xi
    y = _dot(ub, toep_ref[...]) + _dot(xp_ref[...].astype(BF16), vv_ref[...])
    for i in range(S5_T):
        o_ref[fold(i), :] = y[:, i * LANE:(i + 1) * LANE]


def _s5(u4, consts, layer, batch, n_rows):
    toep, wz, vv, a_t = consts
    _, n, w = u4.shape
    per_batch = n // batch // (n_rows * S5_T)
    blk = pl.BlockSpec((None, n_rows * S5_T, w), lambda k, bi, i: (k, bi * per_batch + i, 0))
    wspec = lambda a: pl.BlockSpec((None, None) + a.shape[2:], lambda k, bi, i: (layer, k, 0, 0))
    return pl.pallas_call(
        functools.partial(_s5_body, n_rows=n_rows),
        grid=(S5_LT, batch, per_batch),
        in_specs=[blk, wspec(toep), wspec(wz), wspec(vv), wspec(a_t)],
        out_specs=blk,
        out_shape=jax.ShapeDtypeStruct(u4.shape, F32),
        scratch_shapes=[pltpu.VMEM((n_rows, 2 * S5_SW), F32), pltpu.VMEM((n_rows, 2 * S5_SW), F32),
                        pltpu.VMEM((SUBLANE, 2 * S5_SW), F32)],
        compiler_params=_params(("arbitrary", "arbitrary", "arbitrary")),
        name="s5",
    )(u4, toep, wz, vv, a_t)


def _outproj_body(h_ref, oa_ref, ob_ref, *rest):
    y_refs, u_refs = rest[0:S5_LT], rest[S5_LT:2 * S5_LT]
    d_ref, gw_ref, gb_ref, wo_ref, g_ref, o_ref = rest[2 * S5_LT:]
    yc = jnp.concatenate([r[...] for r in y_refs], axis=-1)
    u = jnp.concatenate([r[...] for r in u_refs], axis=-1)
    y = jax.nn.gelu(yc + d_ref[...] * u)
    oc = y * jax.nn.sigmoid(_dot(y.astype(BF16), gw_ref[...]) + gb_ref[...])
    mix = (_dot(oa_ref[...], wo_ref[0:D_HG, :])
           + _dot(ob_ref[...], wo_ref[D_HG:D_HG + D_SSD, :])
           + _dot(oc.astype(BF16), wo_ref[D_HG + D_SSD:D_MIX, :]))
    o_ref[...] = h_ref[...] + _rms(mix, g_ref[...])


def _outproj(h, oa, ob, y4, u4, s5_d, glu_w, glu_b, w_out, ng, layer, tm):
    n, d = h.shape
    tok = lambda w: pl.BlockSpec((tm, w), lambda i: (i, 0))
    tile = lambda k: pl.BlockSpec((None, tm, LANE), lambda i: (k, i, 0))
    tiles = [tile(k) for k in range(S5_LT)]
    return pl.pallas_call(
        _outproj_body,
        grid=(n // tm,),
        in_specs=[tok(d), tok(D_HG), tok(D_SSD)] + tiles + tiles + [
            pl.BlockSpec((None, 1, D_S5), lambda i: (layer, 0, 0)),
            pl.BlockSpec((None, D_S5, D_S5), lambda i: (layer, 0, 0)),
            pl.BlockSpec((None, 1, D_S5), lambda i: (layer, 0, 0)),
            pl.BlockSpec((None, D_MIX, d), lambda i: (layer, 0, 0)),
            pl.BlockSpec((None, 1, d), lambda i: (layer * 6 + 3, 0, 0)),
        ],
        out_specs=tok(d),
        out_shape=jax.ShapeDtypeStruct((n, d), F32),
        compiler_params=_params(("arbitrary",)),
        name="outproj",
    )(h, oa, ob, *([y4] * S5_LT), *([u4] * S5_LT), s5_d, glu_w, glu_b, w_out, ng)


def _tile(n, pref):
    return pref if n % pref == 0 else n


def kernel(x, norm_g, ffn_w_gate, ffn_w_up, ffn_w_down, w_in, w_out, hg_lb_logits, hg_gnorm, ssd_conv_w, ssd_conv_b,
           ssd_dt_bias, ssd_A_log, ssd_D, ssd_norm, s5_A_re, s5_A_im, s5_B_re, s5_B_im, s5_C_re, s5_C_im, s5_D,
           s5_log_dt, s5_glu_w, s5_glu_b):
    bsz, seq, d = x.shape
    depth = norm_g.shape[0]
    n = bsz * seq
    assert d == D_MODEL and seq % (CHUNK * S5_T) == 0 and w_in.shape[-1] == COL_DT + SSD_HEADS

    ng = norm_g.astype(F32).reshape(depth * 6, 1, d)
    wg, wu, wd = ffn_w_gate.astype(BF16), ffn_w_up.astype(BF16), ffn_w_down.astype(BF16)
    o = 4 * D_HG
    w_in_p = jnp.concatenate([
        w_in[..., 0:o], w_in[..., o + D_SSD:o + D_SSD + D_XBC], w_in[..., o:o + D_SSD],
        w_in[..., o + D_SSD + D_XBC + SSD_HEADS:], w_in[..., o + D_SSD + D_XBC:o + D_SSD + D_XBC + SSD_HEADS],
        jnp.zeros(w_in.shape[:-1] + (LANE - SSD_HEADS,), w_in.dtype)], axis=-1).astype(BF16)
    w_out_b = w_out.astype(BF16)
    glu_w_b = s5_glu_w.astype(BF16)
    lb = jnp.cumsum(jax.nn.softmax(hg_lb_logits.astype(F32), axis=0), axis=0)
    lb = (lb - lb[:1])[:, None, :]
    loglb, l1m, omlb = jnp.log(lb), jnp.log1p(-lb), 1.0 - lb
    gn = hg_gnorm.astype(F32)[:, None, :]
    a_rep = jnp.repeat(-jnp.exp(ssd_A_log.astype(F32)) * LOG2E, SSD_HEADDIM, axis=-1)[:, None, :]
    d_rep = jnp.repeat(ssd_D.astype(F32), SSD_HEADDIM, axis=-1)[:, None, :]
    dtb = jnp.pad(ssd_dt_bias.astype(F32), ((0, 0), (0, LANE - SSD_HEADS)))[:, None, :]
    cw, cb, nw = ssd_conv_w.astype(F32), ssd_conv_b.astype(F32)[:, None, :], ssd_norm.astype(F32)[:, None, :]
    s5_c = jax.vmap(_s5_consts)(s5_A_re, s5_A_im, s5_B_re, s5_B_im, s5_C_re, s5_C_im, s5_log_dt)
    s5_d, glu_b = s5_D.astype(F32)[:, None, :], s5_glu_b.astype(F32)[:, None, :]
    hg_c = _hgrn_consts()
    ssd_c = _ssd_consts()

    tm = _tile(n, 512)
    tl_h = _tile(seq, 256)
    tl_s = _tile(seq, 256)
    s5_rows = _tile(seq // S5_T, 512)

    h = x.astype(F32).reshape(n, d)
    for l in range(depth):
        h = _ffn(h, ng, wg, wu, wd, l, 0, tm)
        qfig, xbc, z, u4, dt = _inproj(h, ng, w_in_p, l, tm)
        tok3 = lambda a: a.reshape(bsz, seq, a.shape[-1])
        oa = _hgrn(tok3(qfig), hg_c, loglb[l], l1m[l], omlb[l], gn[l], tl_h)
        ob = _ssd(tok3(xbc), tok3(z), tok3(dt), ssd_c, cw[l], cb[l], dtb[l], a_rep[l], d_rep[l], nw[l], tl_s)
        y4 = _s5(u4, s5_c, l, bsz, s5_rows)
        h = _outproj(h, oa.reshape(n, D_HG), ob.reshape(n, D_SSD), y4, u4, s5_d, glu_w_b, glu_b, w_out_b, ng, l, tm)
        h = _ffn(h, ng, wg, wu, wd, l, 1, tm)
    return h.reshape(bsz, seq, d).astype(x.dtype)
```

```python
import functools

import numpy as np
import jax
import jax.numpy as jnp
from jax import lax
from jax.experimental import pallas as pl
from jax.experimental.pallas import tpu as pltpu

F32 = jnp.float32
BF16 = jnp.bfloat16
EPS = 1e-6
LOG2E = 1.4426950408889634

D_MODEL = 1024
D_FF = 2816
HG_HEADS, HG_DK = 4, 128
D_HG = HG_HEADS * HG_DK
SSD_HEADS, SSD_HEADDIM, SSD_GROUPS, SSD_STATE, SSD_CONV = 8, 64, 2, 128, 4
D_SSD = SSD_HEADS * SSD_HEADDIM
D_XBC = D_SSD + 2 * SSD_GROUPS * SSD_STATE
S5_GROUPS, S5_GROUP_CH, S5_STATE = 32, 16, 64
D_S5 = S5_GROUPS * S5_GROUP_CH
D_MIX = D_HG + D_SSD + D_S5
CHUNK = 64
LANE = 128
SUBLANE = 8
S5_T = 8
S5_LT = D_S5 // LANE
S5_GPT = LANE // S5_GROUP_CH
S5_SW = S5_GPT * S5_STATE

COL_XBC, COL_Z, COL_U, COL_DT = 4 * D_HG, 4 * D_HG + D_XBC, 4 * D_HG + D_XBC + D_SSD, 4 * D_HG + D_XBC + D_SSD + D_S5
D_PROJ = COL_DT + LANE
VMEM_LIMIT = 56 * 1024 * 1024


def _dot(a, b):
    return jnp.dot(a, b, preferred_element_type=F32)


def _dot_nt(a, b):
    return lax.dot_general(a, b, (((1,), (1,)), ((), ())), preferred_element_type=F32)


def _dot_tn(a, b):
    return lax.dot_general(a, b, (((0,), (0,)), ((), ())), preferred_element_type=F32)


def _split(x):
    hi = x.astype(BF16)
    return hi, (x - hi.astype(F32)).astype(BF16)


def _dot_split(m2, x):
    hi, lo = _split(x)
    return _dot(m2, jnp.concatenate([hi, lo], axis=0))


def _dot_split_r(x, m2):
    hi, lo = _split(x)
    return _dot(jnp.concatenate([hi, lo], axis=1), m2)


def _rms(x, g):
    return x * lax.rsqrt(jnp.mean(x * x, axis=-1, keepdims=True) + EPS) * g


def _silu(x):
    return x * jax.nn.sigmoid(x)


def _params(sem):
    return pltpu.CompilerParams(dimension_semantics=sem, vmem_limit_bytes=VMEM_LIMIT)


def _full(a):
    return pl.BlockSpec(a.shape, lambda *_: (0,) * a.ndim)


def _ffn_apply(x, gpre, wg_ref, wu_ref, wd_ref, gpost):
    xn = _rms(x, gpre).astype(BF16)
    act = (_silu(_dot(xn, wg_ref[...])) * _dot(xn, wu_ref[...])).astype(BF16)
    return x + 0.5 * _rms(_dot(act, wd_ref[...]), gpost)


def _ffn_body(x_ref, gpre_ref, wg_ref, wu_ref, wd_ref, gpost_ref, o_ref):
    o_ref[...] = _ffn_apply(x_ref[...], gpre_ref[...], wg_ref, wu_ref, wd_ref, gpost_ref[...])


def _ffn(h, ng, wg, wu, wd, layer, tm):
    n, d = h.shape
    weight = lambda r, c: pl.BlockSpec((None, None, r, c), lambda i: (layer, 0, 0, 0),
                                       pipeline_mode=pl.Buffered(1))
    return pl.pallas_call(
        _ffn_body,
        grid=(n // tm,),
        in_specs=[
            pl.BlockSpec((tm, d), lambda i: (i, 0)),
            pl.BlockSpec((None, 1, d), lambda i: (layer * 6, 0, 0)),
            weight(d, D_FF), weight(d, D_FF), weight(D_FF, d),
            pl.BlockSpec((None, 1, d), lambda i: (layer * 6 + 1, 0, 0)),
        ],
        out_specs=pl.BlockSpec((tm, d), lambda i: (i, 0)),
        out_shape=jax.ShapeDtypeStruct((n, d), F32),
        compiler_params=_params(("arbitrary",)),
        name="ffn",
    )(h, ng, wg, wu, wd, ng)


def _inproj_body(x_ref, g_ref, w_ref, qfig_ref, xbc_ref, z_ref, u_ref, dt_ref):
    xn = _rms(x_ref[...], g_ref[...]).astype(BF16)
    qfig_ref[...] = _dot(xn, w_ref[:, 0:COL_XBC])
    xbc_ref[...] = _dot(xn, w_ref[:, COL_XBC:COL_Z])
    z_ref[...] = _dot(xn, w_ref[:, COL_Z:COL_U])
    u = _dot(xn, w_ref[:, COL_U:COL_DT])
    for k in range(S5_LT):
        u_ref[k] = u[:, k * LANE:(k + 1) * LANE]
    dt_ref[...] = _dot(xn, w_ref[:, COL_DT:D_PROJ])


def _inproj(h, ng, w_in_p, layer, tm):
    n, d = h.shape
    tok = lambda w: pl.BlockSpec((tm, w), lambda i: (i, 0))
    out_w = (4 * D_HG, D_XBC, D_SSD)
    return pl.pallas_call(
        _inproj_body,
        grid=(n // tm,),
        in_specs=[
            tok(d),
            pl.BlockSpec((None, 1, d), lambda i: (layer * 6 + 2, 0, 0)),
            pl.BlockSpec((None, d, D_PROJ), lambda i: (layer, 0, 0)),
        ],
        out_specs=[tok(w) for w in out_w] + [pl.BlockSpec((S5_LT, tm, LANE), lambda i: (0, i, 0)), tok(LANE)],
        out_shape=[jax.ShapeDtypeStruct((n, w), F32) for w in out_w]
        + [jax.ShapeDtypeStruct((S5_LT, n, LANE), F32), jax.ShapeDtypeStruct((n, LANE), F32)],
        compiler_params=_params(("arbitrary",)),
        name="inproj",
    )(h, ng, w_in_p)


HG_LEVELS = (32, 16, 8, 4, 2)


def _block_tri_up(m, n=CHUNK):
    t = np.arange(n)
    same = (t[:, None] // m) == (t[None, :] // m)
    tri = same & (t[None, :] <= t[:, None])
    up = same & (t[None, :] > t[:, None])
    return tri.astype(np.float32), up.astype(np.float32)


def _hgrn_consts():
    mats, masks = [], []
    for m in (CHUNK,) + HG_LEVELS:
        mats += list(_block_tri_up(m))
    t = np.arange(CHUNK)
    for m in HG_LEVELS + (1,):
        bt, bs = t[:, None] // m, t[None, :] // m
        masks.append(((bt % 2 == 1) & (bs == bt - 1)).astype(np.float32))
    masks.append(np.eye(CHUNK, dtype=np.float32))
    cm = np.concatenate(mats, 0)
    return jnp.asarray(np.concatenate([cm, cm], 1), BF16), jnp.asarray(np.stack(masks, 0), F32)


def _hgrn_body(p_ref, cm_ref, mask_ref, loglb_ref, l1m_ref, omlb_ref, gn_ref, o_ref, st_ref, *, n_chunks):
    @pl.when(pl.program_id(0) == 0)
    def _():
        st_ref[...] = jnp.zeros_like(st_ref)

    cm = cm_ref[...]
    loglb, l1m, omlb, gn = loglb_ref[...], l1m_ref[...], omlb_ref[...], gn_ref[...]
    nl = len(HG_LEVELS)

    def chunk_one(b, rows):
        q = _silu(p_ref[b, rows, 0:D_HG])
        fr = p_ref[b, rows, D_HG:2 * D_HG]
        v = p_ref[b, rows, 2 * D_HG:3 * D_HG].astype(BF16)
        gr = p_ref[b, rows, 3 * D_HG:4 * D_HG]
        ls = jnp.minimum(fr, 0.0) - jnp.log1p(jnp.exp(-jnp.abs(fr)))
        bb = l1m + ls
        lf = jnp.maximum(loglb, bb) + jnp.log1p(jnp.exp(-jnp.abs(loglb - bb)))
        kk = omlb * jnp.exp(ls - fr)
        lf2 = lf * LOG2E
        cs = _dot_split(cm, lf2)
        qb, kb = q.astype(BF16), kk.astype(BF16)
        qf = (q * jnp.exp2(lf2)).astype(BF16)
        outs = []
        for h in range(HG_HEADS):
            hs = slice(h * HG_DK, (h + 1) * HG_DK)
            qh, kh = q[:, hs], kk[:, hs]
            st = st_ref[b, h]
            c64, r64 = cs[0:CHUNK, hs], cs[CHUNK:2 * CHUNK, hs]
            scores = mask_ref[nl + 1] * _dot_nt(qb[:, hs], kb[:, hs]) + mask_ref[nl] * _dot_nt(qf[:, hs], kb[:, hs])
            for li in range(nl):
                base = 2 * CHUNK * (li + 1)
                cmm, rmm = cs[base:base + CHUNK, hs], cs[base + CHUNK:base + 2 * CHUNK, hs]
                p = _dot_nt((qh * jnp.exp2(cmm)).astype(BF16), (kh * jnp.exp2(rmm)).astype(BF16))
                scores = scores + mask_ref[li] * p
            o = _dot_nt((qh * jnp.exp2(c64)).astype(BF16), st.astype(BF16)) + _dot(scores.astype(BF16), v[:, hs])
            ke = (kh * jnp.exp2(r64)).astype(BF16)
            st_ref[b, h] = jnp.exp2(c64[CHUNK - 1:CHUNK, :]) * st + _dot_tn(v[:, hs], ke)
            outs.append(o * lax.rsqrt(jnp.mean(o * o, axis=-1, keepdims=True) + EPS) * gn)
        o_ref[b, rows, :] = (jnp.concatenate(outs, axis=-1) * _silu(gr)).astype(o_ref.dtype)

    def chunk_step(c, carry):
        rows = pl.ds(pl.multiple_of(c * CHUNK, CHUNK), CHUNK)
        for b in range(p_ref.shape[0]):
            chunk_one(b, rows)
        return carry

    lax.fori_loop(0, n_chunks, chunk_step, 0, unroll=2)


def _hgrn(qfig3, consts, loglb, l1m, omlb, gn, tl):
    b, l, _ = qfig3.shape
    cm, masks = consts
    return pl.pallas_call(
        functools.partial(_hgrn_body, n_chunks=tl // CHUNK),
        grid=(l // tl,),
        in_specs=[pl.BlockSpec((b, tl, 4 * D_HG), lambda i: (0, i, 0)),
                  _full(cm), _full(masks), _full(loglb), _full(l1m), _full(omlb), _full(gn)],
        out_specs=pl.BlockSpec((b, tl, D_HG), lambda i: (0, i, 0)),
        out_shape=jax.ShapeDtypeStruct((b, l, D_HG), BF16),
        scratch_shapes=[pltpu.VMEM((b, HG_HEADS, HG_DK, HG_DK), F32)],
        compiler_params=_params(("arbitrary",)),
        name="hgrn2",
    )(qfig3, cm, masks, loglb, l1m, omlb, gn)


def _ssd_consts():
    tri, up = _block_tri_up(CHUNK)
    t = np.arange(CHUNK)
    su = (t[:, None] > t[None, :]).astype(np.float32)
    causal = (t[:, None] >= t[None, :]).astype(np.float32)
    expand = np.zeros((LANE, D_SSD), np.float32)
    for h in range(SSD_HEADS):
        expand[h, h * SSD_HEADDIM:(h + 1) * SSD_HEADDIM] = 1.0
    tu = np.concatenate([tri, up], 0)
    return (jnp.asarray(np.concatenate([tu, tu], 1), BF16), jnp.asarray(np.tile(su, (1, SSD_HEADS)), F32),
            jnp.asarray(causal, F32), jnp.asarray(np.concatenate([expand, expand], 0), BF16))


def _ssd_body(xbc_ref, z_ref, dt_ref, cw_ref, cb_ref, dtb_ref, a_ref, d_ref, nw_ref, tu_ref, su_ref, causal_ref,
              ex_ref, o_ref, st_ref, ext_ref, *, tl):
    pad = SUBLANE
    nb = xbc_ref.shape[0]

    @pl.when(pl.program_id(0) == 0)
    def _():
        st_ref[...] = jnp.zeros_like(st_ref)
        ext_ref[:, 0:pad, :] = jnp.zeros((nb, pad, D_XBC), F32)

    ext_ref[:, pad:pad + tl, :] = xbc_ref[...]

    tu, su, causal, expand = tu_ref[...], su_ref[...], causal_ref[...], ex_ref[...]
    a_rep, d_rep, nw, cb, dtb = a_ref[...], d_ref[...], nw_ref[...], cb_ref[...], dtb_ref[...]
    gw = D_SSD // SSD_GROUPS
    hpg = SSD_HEADS // SSD_GROUPS

    def chunk_one(b, r0):
        rows = pl.ds(r0, CHUNK)
        win = ext_ref[b, pl.ds(r0, CHUNK + pad), :]
        conv = cb + cw_ref[SSD_CONV - 1:SSD_CONV, :] * win[pad:pad + CHUNK, :]
        for j in range(SSD_CONV - 1):
            conv = conv + cw_ref[j:j + 1, :] * pltpu.roll(win, SSD_CONV - 1 - j, 0)[pad:pad + CHUNK, :]
        xs = _silu(conv)
        x = xs[:, 0:D_SSD]
        dtin = dt_ref[b, rows, :] + dtb
        sp = jnp.maximum(dtin, 0.0) + jnp.log1p(jnp.exp(-jnp.abs(dtin)))
        dtr = _dot_split_r(sp, expand)
        adt = dtr * a_rep
        r = _dot_split(tu, jnp.concatenate([adt, adt * su], axis=-1))
        acs, rs = r[0:CHUNK, 0:D_SSD], r[CHUNK:2 * CHUNK, 0:D_SSD]
        dm = r[0:CHUNK, D_SSD:2 * D_SSD]
        xdt = x * dtr
        xdt_b = xdt.astype(BF16)
        ys = []
        for g in range(SSD_GROUPS):
            gs = slice(g * gw, (g + 1) * gw)
            bg = xs[:, D_SSD + g * SSD_STATE:D_SSD + (g + 1) * SSD_STATE].astype(BF16)
            cg = xs[:, D_SSD + (SSD_GROUPS + g) * SSD_STATE:D_SSD + (SSD_GROUPS + g + 1) * SSD_STATE].astype(BF16)
            st = st_ref[b, g]
            scores = _dot_nt(cg, bg) * causal
            y_off = _dot(cg, st.astype(BF16)) * jnp.exp2(acs[:, gs])
            yd = []
            for hh in range(hpg):
                sl = slice((g * hpg + hh) * SSD_HEADDIM, (g * hpg + hh + 1) * SSD_HEADDIM)
                yd.append(_dot((scores * jnp.exp2(dm[:, sl])).astype(BF16), xdt_b[:, sl]))
            ys.append(jnp.concatenate(yd, axis=-1) + y_off)
            xd = (xdt[:, gs] * jnp.exp2(rs[:, gs])).astype(BF16)
            st_ref[b, g] = jnp.exp2(acs[CHUNK - 1:CHUNK, gs]) * st + _dot_tn(bg, xd)
        y = jnp.concatenate(ys, axis=-1) + x * d_rep
        y = y * _silu(z_ref[b, rows, :])
        yn = [y[:, g * gw:(g + 1) * gw] * lax.rsqrt(
            jnp.mean(y[:, g * gw:(g + 1) * gw] * y[:, g * gw:(g + 1) * gw], axis=-1, keepdims=True) + EPS)
            for g in range(SSD_GROUPS)]
        o_ref[b, rows, :] = (jnp.concatenate(yn, axis=-1) * nw).astype(o_ref.dtype)

    def chunk_step(c, carry):
        r0 = pl.multiple_of(c * CHUNK, CHUNK)
        for b in range(nb):
            chunk_one(b, r0)
        return carry

    lax.fori_loop(0, tl // CHUNK, chunk_step, 0, unroll=2)
    ext_ref[:, 0:pad, :] = xbc_ref[:, tl - pad:tl, :]


def _ssd(xbc3, z3, dt3, consts, cw, cb, dtb, a_rep, d_rep, nw, tl):
    b, l, _ = xbc3.shape
    tok = lambda w: pl.BlockSpec((b, tl, w), lambda i: (0, i, 0))
    params = (cw, cb, dtb, a_rep, d_rep, nw) + tuple(consts)
    return pl.pallas_call(
        functools.partial(_ssd_body, tl=tl),
        grid=(l // tl,),
        in_specs=[tok(D_XBC), tok(D_SSD), tok(LANE)] + [_full(a) for a in params],
        out_specs=tok(D_SSD),
        out_shape=jax.ShapeDtypeStruct((b, l, D_SSD), BF16),
        scratch_shapes=[
            pltpu.VMEM((b, SSD_GROUPS, SSD_STATE, D_SSD // SSD_GROUPS), F32),
            pltpu.VMEM((b, tl + SUBLANE, D_XBC), F32),
        ],
        compiler_params=_params(("arbitrary",)),
        name="ssd",
    )(xbc3, z3, dt3, *params)


def _s5_expand(r, rep, row_group, col_group):
    big = jnp.dot(r, rep, preferred_element_type=BF16)
    rows, cols = big.shape[-2:]
    rg = row_group(lax.broadcasted_iota(jnp.int32, (rows, cols), 0))
    cg = col_group(lax.broadcasted_iota(jnp.int32, (rows, cols), 1))
    return jnp.where(rg == cg, big, jnp.zeros_like(big))


def _s5_consts(a_re, a_im, b_re, b_im, c_re, c_im, log_dt):
    t_, gc, gp = S5_T, S5_GROUP_CH, S5_STATE
    ar, ai = a_re.astype(F32), a_im.astype(F32)
    delta = jnp.exp(log_dt.astype(F32))[:, None]
    mag = jnp.exp(ar * delta)
    ab_re, ab_im = mag * jnp.cos(ai * delta), mag * jnp.sin(ai * delta)
    den = ar * ar + ai * ai
    nr, ni = ab_re - 1.0, ab_im
    fr = (nr * ar + ni * ai) / den
    fi = (ni * ar - nr * ai) / den
    br, bi = b_re.astype(F32).transpose(0, 2, 1), b_im.astype(F32).transpose(0, 2, 1)
    bb_re = fr[:, None, :] * br - fi[:, None, :] * bi
    bb_im = fr[:, None, :] * bi + fi[:, None, :] * br
    lam_r, lam_i = ar * delta, ai * delta
    tau = jnp.arange(t_ + 1, dtype=F32)
    pr = jnp.exp(tau[:, None, None] * lam_r) * jnp.cos(tau[:, None, None] * lam_i)
    pi = jnp.exp(tau[:, None, None] * lam_r) * jnp.sin(tau[:, None, None] * lam_i)
    qr = jnp.exp(lam_r[..., None] * tau) * jnp.cos(lam_i[..., None] * tau)
    qi = jnp.exp(lam_r[..., None] * tau) * jnp.sin(lam_i[..., None] * tau)
    abr = pr[:, :, None, :] * bb_re - pi[:, :, None, :] * bb_im
    abi = pr[:, :, None, :] * bb_im + pi[:, :, None, :] * bb_re
    cr, ci = c_re.astype(F32), c_im.astype(F32)
    hp = lax.Precision.HIGHEST
    km = (jnp.einsum('tgcp,gdp->gctd', abr[:t_], cr, precision=hp)
          - jnp.einsum('tgcp,gdp->gctd', abi[:t_], ci, precision=hp))
    tt, jj, ii = np.arange(t_)[:, None, None], np.arange(t_)[None, :, None], np.arange(t_)[None, None, :]
    shift = np.einsum('tji,de->tdjie', (tt == ii - jj).astype(np.float32), np.eye(gc, dtype=np.float32))
    shift = jnp.asarray(shift.reshape(t_ * gc, t_ * t_ * gc), BF16)
    r_t = jnp.dot(km.astype(BF16).reshape(S5_GROUPS * gc, t_ * gc), shift, preferred_element_type=BF16)
    r_t = r_t.reshape(S5_LT, S5_GPT, gc, t_, t_ * gc).transpose(0, 3, 1, 2, 4).reshape(S5_LT, t_ * LANE, t_ * gc)
    rev = np.arange(t_ - 1, -1, -1)
    r_w = jnp.concatenate([abr[rev], abi[rev]], axis=-1).astype(BF16)
    r_w = r_w.reshape(t_, S5_LT, S5_GPT, gc, 2 * gp).transpose(1, 0, 2, 3, 4).reshape(S5_LT, t_ * LANE, 2 * gp)
    crt, cit = cr.transpose(0, 2, 1)[:, :, None, :], ci.transpose(0, 2, 1)[:, :, None, :]
    q1r, q1i = qr[:, :, 1:, None], qi[:, :, 1:, None]
    vr = (crt * q1r - cit * q1i).astype(BF16).reshape(S5_LT, S5_SW, t_ * gc)
    vi = (-(crt * q1i + cit * q1r)).astype(BF16).reshape(S5_LT, S5_SW, t_ * gc)
    r_v = jnp.concatenate([vr, vi], axis=1)
    src, dst = np.arange(t_ * gc), np.arange(t_ * LANE)
    rep = jnp.asarray((src[:, None] // gc == dst[None, :] // LANE) & (src[:, None] % gc == dst[None, :] % gc), BF16)
    src, dst = np.arange(2 * gp), np.arange(2 * S5_SW)
    repw = jnp.asarray((src[:, None] // gp == dst[None, :] // S5_SW) & (src[:, None] % gp == dst[None, :] % gp), BF16)
    io_group = lambda x: (x // gc) % S5_GPT
    st_group = lambda x: (x % S5_SW) // gp
    toep = _s5_expand(r_t, rep, io_group, io_group)
    wz = _s5_expand(r_w, repw, io_group, st_group)
    vv = _s5_expand(r_v, rep, st_group, io_group)
    a_t = jnp.stack([pr[t_].reshape(S5_LT, S5_SW), pi[t_].reshape(S5_LT, S5_SW)], axis=1)
    return toep, wz, vv, a_t


def _s5_body(u_ref, toep_ref, wz_ref, vv_ref, at_ref, o_ref, z_ref, xp_ref, st_ref, *, n_rows):
    @pl.when(pl.program_id(2) == 0)
    def _():
        st_ref[...] = jnp.zeros_like(st_ref)

    fold = lambda j: pl.ds(j, n_rows, stride=S5_T)
    ub = jnp.concatenate([u_ref[fold(j), :] for j in range(S5_T)], axis=-1).astype(BF16)
    z_ref[...] = _dot(ub, wz_ref[...])
    ar, ai = at_ref[0:1, :], at_ref[1:2, :]
    re, im = slice(0, S5_SW), slice(S5_SW, 2 * S5_SW)

    def step(c, carry):
        xr, xi = carry
        row = pl.ds(c, 1)
        xp_ref[row, re] = xr
        xp_ref[row, im] = xi
        return (xr * ar - xi * ai + z_ref[row, re], xr * ai + xi * ar + z_ref[row, im])

    xr, xi = lax.fori_loop(0, n_rows, step, (st_ref[0:1, re], st_ref[0:1, im]), unroll=8)
    st_ref[0:1, re] = xr
    st_ref[0:1, im] = xi
    y = _dot(ub, toep_ref[...]) + _dot(xp_ref[...].astype(BF16), vv_ref[...])
    for i in range(S5_T):
        o_ref[fold(i), :] = y[:, i * LANE:(i + 1) * LANE]


def _s5(u4, consts, layer, batch, n_rows):
    toep, wz, vv, a_t = consts
    _, n, w = u4.shape
    per_batch = n // batch // (n_rows * S5_T)
    blk = pl.BlockSpec((None, n_rows * S5_T, w), lambda k, bi, i: (k, bi * per_batch + i, 0))
    wspec = lambda a: pl.BlockSpec((None, None) + a.shape[2:], lambda k, bi, i: (layer, k, 0, 0))
    return pl.pallas_call(
        functools.partial(_s5_body, n_rows=n_rows),
        grid=(S5_LT, batch, per_batch),
        in_specs=[blk, wspec(toep), wspec(wz), wspec(vv), wspec(a_t)],
        out_specs=blk,
        out_shape=jax.ShapeDtypeStruct(u4.shape, F32),
        scratch_shapes=[pltpu.VMEM((n_rows, 2 * S5_SW), F32), pltpu.VMEM((n_rows, 2 * S5_SW), F32),
                        pltpu.VMEM((SUBLANE, 2 * S5_SW), F32)],
        compiler_params=_params(("arbitrary", "arbitrary", "arbitrary")),
        name="s5",
    )(u4, toep, wz, vv, a_t)


def _outproj_ffn_body(h_ref, oa_ref, ob_ref, *rest):
    y_refs, u_refs = rest[0:S5_LT], rest[S5_LT:2 * S5_LT]
    d_ref, gw_ref, gb_ref, wo_ref, g_ref, gpre_ref, wg_ref, wu_ref, wd_ref, gpost_ref, o_ref = rest[2 * S5_LT:]
    yc = jnp.concatenate([r[...] for r in y_refs], axis=-1)
    u = jnp.concatenate([r[...] for r in u_refs], axis=-1)
    y = jax.nn.gelu(yc + d_ref[...] * u)
    oc = y * jax.nn.sigmoid(_dot(y.astype(BF16), gw_ref[...]) + gb_ref[...])
    mix = (_dot(oa_ref[...], wo_ref[0:D_HG, :])
           + _dot(ob_ref[...], wo_ref[D_HG:D_HG + D_SSD, :])
           + _dot(oc.astype(BF16), wo_ref[D_HG + D_SSD:D_MIX, :]))
    x = h_ref[...] + _rms(mix, g_ref[...])
    o_ref[...] = _ffn_apply(x, gpre_ref[...], wg_ref, wu_ref, wd_ref, gpost_ref[...])


def _outproj_ffn(h, oa, ob, y4, u4, s5_d, glu_w, glu_b, w_out, ng, wg, wu, wd, layer, tm):
    n, d = h.shape
    tok = lambda w: pl.BlockSpec((tm, w), lambda i: (i, 0))
    tile = lambda k: pl.BlockSpec((None, tm, LANE), lambda i: (k, i, 0))
    tiles = [tile(k) for k in range(S5_LT)]
    once = dict(pipeline_mode=pl.Buffered(1))
    norm = lambda k: pl.BlockSpec((None, 1, d), lambda i: (layer * 6 + k, 0, 0))
    weight = lambda r, c: pl.BlockSpec((None, None, r, c), lambda i: (layer, 1, 0, 0), **once)
    return pl.pallas_call(
        _outproj_ffn_body,
        grid=(n // tm,),
        in_specs=[tok(d), tok(D_HG), tok(D_SSD)] + tiles + tiles + [
            pl.BlockSpec((None, 1, D_S5), lambda i: (layer, 0, 0)),
            pl.BlockSpec((None, D_S5, D_S5), lambda i: (layer, 0, 0), **once),
            pl.BlockSpec((None, 1, D_S5), lambda i: (layer, 0, 0)),
            pl.BlockSpec((None, D_MIX, d), lambda i: (layer, 0, 0), **once),
            norm(3), norm(4), weight(d, D_FF), weight(d, D_FF), weight(D_FF, d), norm(5),
        ],
        out_specs=tok(d),
        out_shape=jax.ShapeDtypeStruct((n, d), F32),
        compiler_params=_params(("arbitrary",)),
        name="outproj_ffn",
    )(h, oa, ob, *([y4] * S5_LT), *([u4] * S5_LT), s5_d, glu_w, glu_b, w_out, ng, ng, wg, wu, wd, ng)


def _tile(n, pref):
    return pref if n % pref == 0 else n


def kernel(x, norm_g, ffn_w_gate, ffn_w_up, ffn_w_down, w_in, w_out, hg_lb_logits, hg_gnorm, ssd_conv_w, ssd_conv_b,
           ssd_dt_bias, ssd_A_log, ssd_D, ssd_norm, s5_A_re, s5_A_im, s5_B_re, s5_B_im, s5_C_re, s5_C_im, s5_D,
           s5_log_dt, s5_glu_w, s5_glu_b):
    bsz, seq, d = x.shape
    depth = norm_g.shape[0]
    n = bsz * seq
    assert d == D_MODEL and seq % (CHUNK * S5_T) == 0 and w_in.shape[-1] == COL_DT + SSD_HEADS

    ng = norm_g.astype(F32).reshape(depth * 6, 1, d)
    wg, wu, wd = ffn_w_gate.astype(BF16), ffn_w_up.astype(BF16), ffn_w_down.astype(BF16)
    o = 4 * D_HG
    w_in_p = jnp.concatenate([
        w_in[..., 0:o], w_in[..., o + D_SSD:o + D_SSD + D_XBC], w_in[..., o:o + D_SSD],
        w_in[..., o + D_SSD + D_XBC + SSD_HEADS:], w_in[..., o + D_SSD + D_XBC:o + D_SSD + D_XBC + SSD_HEADS],
        jnp.zeros(w_in.shape[:-1] + (LANE - SSD_HEADS,), w_in.dtype)], axis=-1).astype(BF16)
    w_out_b = w_out.astype(BF16)
    glu_w_b = s5_glu_w.astype(BF16)
    lb = jnp.cumsum(jax.nn.softmax(hg_lb_logits.astype(F32), axis=0), axis=0)
    lb = (lb - lb[:1])[:, None, :]
    loglb, l1m, omlb = jnp.log(lb), jnp.log1p(-lb), 1.0 - lb
    gn = hg_gnorm.astype(F32)[:, None, :]
    a_rep = jnp.repeat(-jnp.exp(ssd_A_log.astype(F32)) * LOG2E, SSD_HEADDIM, axis=-1)[:, None, :]
    d_rep = jnp.repeat(ssd_D.astype(F32), SSD_HEADDIM, axis=-1)[:, None, :]
    dtb = jnp.pad(ssd_dt_bias.astype(F32), ((0, 0), (0, LANE - SSD_HEADS)))[:, None, :]
    cw, cb, nw = ssd_conv_w.astype(F32), ssd_conv_b.astype(F32)[:, None, :], ssd_norm.astype(F32)[:, None, :]
    s5_c = jax.vmap(_s5_consts)(s5_A_re, s5_A_im, s5_B_re, s5_B_im, s5_C_re, s5_C_im, s5_log_dt)
    s5_d, glu_b = s5_D.astype(F32)[:, None, :], s5_glu_b.astype(F32)[:, None, :]
    hg_c = _hgrn_consts()
    ssd_c = _ssd_consts()

    tm = _tile(n, 512)
    tl_h = _tile(seq, 256)
    tl_s = _tile(seq, 256)
    s5_rows = _tile(seq // S5_T, 512)

    h = x.astype(F32).reshape(n, d)
    for l in range(depth):
        h = _ffn(h, ng, wg, wu, wd, l, tm)
        qfig, xbc, z, u4, dt = _inproj(h, ng, w_in_p, l, tm)
        tok3 = lambda a: a.reshape(bsz, seq, a.shape[-1])
        oa = _hgrn(tok3(qfig), hg_c, loglb[l], l1m[l], omlb[l], gn[l], tl_h)
        ob = _ssd(tok3(xbc), tok3(z), tok3(dt), ssd_c, cw[l], cb[l], dtb[l], a_rep[l], d_rep[l], nw[l], tl_s)
        y4 = _s5(u4, s5_c, l, bsz, s5_rows)
        h = _outproj_ffn(h, oa.reshape(n, D_HG), ob.reshape(n, D_SSD), y4, u4, s5_d, glu_w_b, glu_b, w_out_b, ng,
                         wg, wu, wd, l, tm)
    return h.reshape(bsz, seq, d).astype(x.dtype)
```

```python
import functools

import numpy as np
import jax
import jax.numpy as jnp
from jax import lax
from jax.experimental import pallas as pl
from jax.experimental.pallas import tpu as pltpu

F32 = jnp.float32
BF16 = jnp.bfloat16
EPS = 1e-6
LOG2E = 1.4426950408889634

D_MODEL = 1024
D_FF = 2816
HG_HEADS, HG_DK = 4, 128
D_HG = HG_HEADS * HG_DK
SSD_HEADS, SSD_HEADDIM, SSD_GROUPS, SSD_STATE, SSD_CONV = 8, 64, 2, 128, 4
D_SSD = SSD_HEADS * SSD_HEADDIM
D_XBC = D_SSD + 2 * SSD_GROUPS * SSD_STATE
S5_GROUPS, S5_GROUP_CH, S5_STATE = 32, 16, 64
D_S5 = S5_GROUPS * S5_GROUP_CH
D_MIX = D_HG + D_SSD + D_S5
CHUNK = 64
LANE = 128
SUBLANE = 8
S5_T = 8
S5_LT = D_S5 // LANE
S5_GPT = LANE // S5_GROUP_CH
S5_SW = S5_GPT * S5_STATE

COL_Z, COL_XBC, COL_TAIL = 4 * D_HG, 4 * D_HG + D_SSD, 4 * D_HG + D_SSD + D_XBC
D_TAIL = D_S5 + LANE
VMEM_LIMIT = 56 * 1024 * 1024


def _dot(a, b):
    return jnp.dot(a, b, preferred_element_type=F32)


def _dot_nt(a, b):
    return lax.dot_general(a, b, (((1,), (1,)), ((), ())), preferred_element_type=F32)


def _dot_tn(a, b):
    return lax.dot_general(a, b, (((0,), (0,)), ((), ())), preferred_element_type=F32)


def _split(x):
    hi = x.astype(BF16)
    return hi, (x - hi.astype(F32)).astype(BF16)


def _dot_split(m2, x):
    hi, lo = _split(x)
    return _dot(m2, jnp.concatenate([hi, lo], axis=0))


def _dot_split_r(x, m2):
    hi, lo = _split(x)
    return _dot(jnp.concatenate([hi, lo], axis=1), m2)


def _rms(x, g):
    return x * lax.rsqrt(jnp.mean(x * x, axis=-1, keepdims=True) + EPS) * g


def _silu(x):
    return x * jax.nn.sigmoid(x)


def _params(sem):
    return pltpu.CompilerParams(dimension_semantics=sem, vmem_limit_bytes=VMEM_LIMIT)


def _full(a):
    return pl.BlockSpec(a.shape, lambda *_: (0,) * a.ndim)


def _ffn_apply(x, gpre, wg_ref, wu_ref, wd_ref, gpost):
    xn = _rms(x, gpre).astype(BF16)
    act = (_silu(_dot(xn, wg_ref[...])) * _dot(xn, wu_ref[...])).astype(BF16)
    return x + 0.5 * _rms(_dot(act, wd_ref[...]), gpost)


def _ffn_body(x_ref, gpre_ref, wg_ref, wu_ref, wd_ref, gpost_ref, o_ref):
    o_ref[...] = _ffn_apply(x_ref[...], gpre_ref[...], wg_ref, wu_ref, wd_ref, gpost_ref[...])


def _ffn(h, ng, wg, wu, wd, layer, tm):
    n, d = h.shape
    weight = lambda r, c: pl.BlockSpec((None, None, r, c), lambda i: (layer, 0, 0, 0),
                                       pipeline_mode=pl.Buffered(1))
    return pl.pallas_call(
        _ffn_body,
        grid=(n // tm,),
        in_specs=[
            pl.BlockSpec((tm, d), lambda i: (i, 0)),
            pl.BlockSpec((None, 1, d), lambda i: (layer * 6, 0, 0)),
            weight(d, D_FF), weight(d, D_FF), weight(D_FF, d),
            pl.BlockSpec((None, 1, d), lambda i: (layer * 6 + 1, 0, 0)),
        ],
        out_specs=pl.BlockSpec((tm, d), lambda i: (i, 0)),
        out_shape=jax.ShapeDtypeStruct((n, d), F32),
        compiler_params=_params(("arbitrary",)),
        name="ffn",
    )(h, ng, wg, wu, wd, ng)


def _inproj_body(x_ref, g_ref, w_ref, wt_ref, qfig_ref, xbc_ref, z_ref, u_ref, dt_ref):
    xn = _rms(x_ref[...], g_ref[...]).astype(BF16)
    qfig_ref[...] = _dot(xn, w_ref[:, 0:COL_Z])
    z_ref[...] = _dot(xn, w_ref[:, COL_Z:COL_XBC])
    xbc_ref[...] = _dot(xn, w_ref[:, COL_XBC:COL_TAIL])
    u = _dot(xn, wt_ref[:, 0:D_S5])
    for k in range(S5_LT):
        u_ref[k] = u[:, k * LANE:(k + 1) * LANE]
    dt_ref[...] = _dot(xn, wt_ref[:, D_S5:D_TAIL])


def _inproj(h, ng, w_in_b, w_tail, layer, tm):
    n, d = h.shape
    tok = lambda w: pl.BlockSpec((tm, w), lambda i: (i, 0))
    out_w = (4 * D_HG, D_XBC, D_SSD)
    return pl.pallas_call(
        _inproj_body,
        grid=(n // tm,),
        in_specs=[
            tok(d),
            pl.BlockSpec((None, 1, d), lambda i: (layer * 6 + 2, 0, 0)),
            pl.BlockSpec((None, d, w_in_b.shape[-1]), lambda i: (layer, 0, 0)),
            pl.BlockSpec((None, d, D_TAIL), lambda i: (layer, 0, 0)),
        ],
        out_specs=[tok(w) for w in out_w] + [pl.BlockSpec((S5_LT, tm, LANE), lambda i: (0, i, 0)), tok(LANE)],
        out_shape=[jax.ShapeDtypeStruct((n, w), F32) for w in out_w]
        + [jax.ShapeDtypeStruct((S5_LT, n, LANE), F32), jax.ShapeDtypeStruct((n, LANE), F32)],
        compiler_params=_params(("arbitrary",)),
        name="inproj",
    )(h, ng, w_in_b, w_tail)


HG_LEVELS = (32, 16, 8, 4, 2)


def _block_tri_up(m, n=CHUNK):
    t = np.arange(n)
    same = (t[:, None] // m) == (t[None, :] // m)
    tri = same & (t[None, :] <= t[:, None])
    up = same & (t[None, :] > t[:, None])
    return tri.astype(np.float32), up.astype(np.float32)


def _hgrn_consts():
    mats, masks = [], []
    for m in (CHUNK,) + HG_LEVELS:
        mats += list(_block_tri_up(m))
    t = np.arange(CHUNK)
    for m in HG_LEVELS + (1,):
        bt, bs = t[:, None] // m, t[None, :] // m
        masks.append(((bt % 2 == 1) & (bs == bt - 1)).astype(np.float32))
    masks.append(np.eye(CHUNK, dtype=np.float32))
    cm = np.concatenate(mats, 0)
    return jnp.asarray(np.concatenate([cm, cm], 1), BF16), jnp.asarray(np.stack(masks, 0), F32)


def _hgrn_body(p_ref, cm_ref, mask_ref, loglb_ref, l1m_ref, omlb_ref, gn_ref, o_ref, st_ref, *, n_chunks):
    @pl.when(pl.program_id(0) == 0)
    def _():
        st_ref[...] = jnp.zeros_like(st_ref)

    cm = cm_ref[...]
    loglb, l1m, omlb, gn = loglb_ref[...], l1m_ref[...], omlb_ref[...], gn_ref[...]
    nl = len(HG_LEVELS)

    def chunk_one(b, rows):
        q = _silu(p_ref[b, rows, 0:D_HG])
        fr = p_ref[b, rows, D_HG:2 * D_HG]
        v = p_ref[b, rows, 2 * D_HG:3 * D_HG].astype(BF16)
        gr = p_ref[b, rows, 3 * D_HG:4 * D_HG]
        ls = jnp.minimum(fr, 0.0) - jnp.log1p(jnp.exp(-jnp.abs(fr)))
        bb = l1m + ls
        lf = jnp.maximum(loglb, bb) + jnp.log1p(jnp.exp(-jnp.abs(loglb - bb)))
        kk = omlb * jnp.exp(ls - fr)
        lf2 = lf * LOG2E
        cs = _dot_split(cm, lf2)
        qb, kb = q.astype(BF16), kk.astype(BF16)
        qf = (q * jnp.exp2(lf2)).astype(BF16)
        outs = []
        for h in range(HG_HEADS):
            hs = slice(h * HG_DK, (h + 1) * HG_DK)
            qh, kh = q[:, hs], kk[:, hs]
            st = st_ref[b, h]
            c64, r64 = cs[0:CHUNK, hs], cs[CHUNK:2 * CHUNK, hs]
            scores = mask_ref[nl + 1] * _dot_nt(qb[:, hs], kb[:, hs]) + mask_ref[nl] * _dot_nt(qf[:, hs], kb[:, hs])
            for li in range(nl):
                base = 2 * CHUNK * (li + 1)
                cmm, rmm = cs[base:base + CHUNK, hs], cs[base + CHUNK:base + 2 * CHUNK, hs]
                p = _dot_nt((qh * jnp.exp2(cmm)).astype(BF16), (kh * jnp.exp2(rmm)).astype(BF16))
                scores = scores + mask_ref[li] * p
            o = _dot_nt((qh * jnp.exp2(c64)).astype(BF16), st.astype(BF16)) + _dot(scores.astype(BF16), v[:, hs])
            ke = (kh * jnp.exp2(r64)).astype(BF16)
            st_ref[b, h] = jnp.exp2(c64[CHUNK - 1:CHUNK, :]) * st + _dot_tn(v[:, hs], ke)
            outs.append(o * lax.rsqrt(jnp.mean(o * o, axis=-1, keepdims=True) + EPS) * gn)
        o_ref[b, rows, :] = (jnp.concatenate(outs, axis=-1) * _silu(gr)).astype(o_ref.dtype)

    def chunk_step(c, carry):
        rows = pl.ds(pl.multiple_of(c * CHUNK, CHUNK), CHUNK)
        for b in range(p_ref.shape[0]):
            chunk_one(b, rows)
        return carry

    lax.fori_loop(0, n_chunks, chunk_step, 0, unroll=2)


def _hgrn(qfig3, consts, loglb, l1m, omlb, gn, tl):
    b, l, _ = qfig3.shape
    cm, masks = consts
    return pl.pallas_call(
        functools.partial(_hgrn_body, n_chunks=tl // CHUNK),
        grid=(l // tl,),
        in_specs=[pl.BlockSpec((b, tl, 4 * D_HG), lambda i: (0, i, 0)),
                  _full(cm), _full(masks), _full(loglb), _full(l1m), _full(omlb), _full(gn)],
        out_specs=pl.BlockSpec((b, tl, D_HG), lambda i: (0, i, 0)),
        out_shape=jax.ShapeDtypeStruct((b, l, D_HG), BF16),
        scratch_shapes=[pltpu.VMEM((b, HG_HEADS, HG_DK, HG_DK), F32)],
        compiler_params=_params(("arbitrary",)),
        name="hgrn2",
    )(qfig3, cm, masks, loglb, l1m, omlb, gn)


def _ssd_consts():
    tri, up = _block_tri_up(CHUNK)
    t = np.arange(CHUNK)
    su = (t[:, None] > t[None, :]).astype(np.float32)
    causal = (t[:, None] >= t[None, :]).astype(np.float32)
    expand = np.zeros((LANE, D_SSD), np.float32)
    for h in range(SSD_HEADS):
        expand[h, h * SSD_HEADDIM:(h + 1) * SSD_HEADDIM] = 1.0
    tu = np.concatenate([tri, up], 0)
    return (jnp.asarray(np.concatenate([tu, tu], 1), BF16), jnp.asarray(np.tile(su, (1, SSD_HEADS)), F32),
            jnp.asarray(causal, F32), jnp.asarray(np.concatenate([expand, expand], 0), BF16))


def _ssd_body(xbc_ref, z_ref, dt_ref, cw_ref, cb_ref, dtb_ref, a_ref, d_ref, nw_ref, tu_ref, su_ref, causal_ref,
              ex_ref, o_ref, st_ref, ext_ref, *, tl):
    pad = SUBLANE
    nb = xbc_ref.shape[0]

    @pl.when(pl.program_id(0) == 0)
    def _():
        st_ref[...] = jnp.zeros_like(st_ref)
        ext_ref[:, 0:pad, :] = jnp.zeros((nb, pad, D_XBC), F32)

    ext_ref[:, pad:pad + tl, :] = xbc_ref[...]

    tu, su, causal, expand = tu_ref[...], su_ref[...], causal_ref[...], ex_ref[...]
    a_rep, d_rep, nw, cb, dtb = a_ref[...], d_ref[...], nw_ref[...], cb_ref[...], dtb_ref[...]
    gw = D_SSD // SSD_GROUPS
    hpg = SSD_HEADS // SSD_GROUPS

    def chunk_one(b, r0):
        rows = pl.ds(r0, CHUNK)
        win = ext_ref[b, pl.ds(r0, CHUNK + pad), :]
        conv = cb + cw_ref[SSD_CONV - 1:SSD_CONV, :] * win[pad:pad + CHUNK, :]
        for j in range(SSD_CONV - 1):
            conv = conv + cw_ref[j:j + 1, :] * pltpu.roll(win, SSD_CONV - 1 - j, 0)[pad:pad + CHUNK, :]
        xs = _silu(conv)
        x = xs[:, 0:D_SSD]
        dtin = dt_ref[b, rows, :] + dtb
        sp = jnp.maximum(dtin, 0.0) + jnp.log1p(jnp.exp(-jnp.abs(dtin)))
        dtr = _dot_split_r(sp, expand)
        adt = dtr * a_rep
        r = _dot_split(tu, jnp.concatenate([adt, adt * su], axis=-1))
        acs, rs = r[0:CHUNK, 0:D_SSD], r[CHUNK:2 * CHUNK, 0:D_SSD]
        dm = r[0:CHUNK, D_SSD:2 * D_SSD]
        xdt = x * dtr
        xdt_b = xdt.astype(BF16)
        ys = []
        for g in range(SSD_GROUPS):
            gs = slice(g * gw, (g + 1) * gw)
            bg = xs[:, D_SSD + g * SSD_STATE:D_SSD + (g + 1) * SSD_STATE].astype(BF16)
            cg = xs[:, D_SSD + (SSD_GROUPS + g) * SSD_STATE:D_SSD + (SSD_GROUPS + g + 1) * SSD_STATE].astype(BF16)
            st = st_ref[b, g]
            scores = _dot_nt(cg, bg) * causal
            y_off = _dot(cg, st.astype(BF16)) * jnp.exp2(acs[:, gs])
            yd = []
            for hh in range(hpg):
                sl = slice((g * hpg + hh) * SSD_HEADDIM, (g * hpg + hh + 1) * SSD_HEADDIM)
                yd.append(_dot((scores * jnp.exp2(dm[:, sl])).astype(BF16), xdt_b[:, sl]))
            ys.append(jnp.concatenate(yd, axis=-1) + y_off)
            xd = (xdt[:, gs] * jnp.exp2(rs[:, gs])).astype(BF16)
            st_ref[b, g] = jnp.exp2(acs[CHUNK - 1:CHUNK, gs]) * st + _dot_tn(bg, xd)
        y = jnp.concatenate(ys, axis=-1) + x * d_rep
        y = y * _silu(z_ref[b, rows, :])
        yn = [y[:, g * gw:(g + 1) * gw] * lax.rsqrt(
            jnp.mean(y[:, g * gw:(g + 1) * gw] * y[:, g * gw:(g + 1) * gw], axis=-1, keepdims=True) + EPS)
            for g in range(SSD_GROUPS)]
        o_ref[b, rows, :] = (jnp.concatenate(yn, axis=-1) * nw).astype(o_ref.dtype)

    def chunk_step(c, carry):
        r0 = pl.multiple_of(c * CHUNK, CHUNK)
        for b in range(nb):
            chunk_one(b, r0)
        return carry

    lax.fori_loop(0, tl // CHUNK, chunk_step, 0, unroll=2)
    ext_ref[:, 0:pad, :] = xbc_ref[:, tl - pad:tl, :]


def _ssd(xbc3, z3, dt3, consts, cw, cb, dtb, a_rep, d_rep, nw, tl):
    b, l, _ = xbc3.shape
    tok = lambda w: pl.BlockSpec((b, tl, w), lambda i: (0, i, 0))
    params = (cw, cb, dtb, a_rep, d_rep, nw) + tuple(consts)
    return pl.pallas_call(
        functools.partial(_ssd_body, tl=tl),
        grid=(l // tl,),
        in_specs=[tok(D_XBC), tok(D_SSD), tok(LANE)] + [_full(a) for a in params],
        out_specs=tok(D_SSD),
        out_shape=jax.ShapeDtypeStruct((b, l, D_SSD), BF16),
        scratch_shapes=[
            pltpu.VMEM((b, SSD_GROUPS, SSD_STATE, D_SSD // SSD_GROUPS), F32),
            pltpu.VMEM((b, tl + SUBLANE, D_XBC), F32),
        ],
        compiler_params=_params(("arbitrary",)),
        name="ssd",
    )(xbc3, z3, dt3, *params)


def _s5_expand(r, rep, row_group, col_group):
    big = jnp.dot(r, rep, preferred_element_type=BF16)
    rows, cols = big.shape[-2:]
    rg = row_group(lax.broadcasted_iota(jnp.int32, (rows, cols), 0))
    cg = col_group(lax.broadcasted_iota(jnp.int32, (rows, cols), 1))
    return jnp.where(rg == cg, big, jnp.zeros_like(big))


def _s5_consts(a_re, a_im, b_re, b_im, c_re, c_im, log_dt):
    t_, gc, gp = S5_T, S5_GROUP_CH, S5_STATE
    ar, ai = a_re.astype(F32), a_im.astype(F32)
    delta = jnp.exp(log_dt.astype(F32))[:, None]
    mag = jnp.exp(ar * delta)
    ab_re, ab_im = mag * jnp.cos(ai * delta), mag * jnp.sin(ai * delta)
    den = ar * ar + ai * ai
    nr, ni = ab_re - 1.0, ab_im
    fr = (nr * ar + ni * ai) / den
    fi = (ni * ar - nr * ai) / den
    br, bi = b_re.astype(F32).transpose(0, 2, 1), b_im.astype(F32).transpose(0, 2, 1)
    bb_re = fr[:, None, :] * br - fi[:, None, :] * bi
    bb_im = fr[:, None, :] * bi + fi[:, None, :] * br
    lam_r, lam_i = ar * delta, ai * delta
    tau = jnp.arange(t_ + 1, dtype=F32)
    pr = jnp.exp(tau[:, None, None] * lam_r) * jnp.cos(tau[:, None, None] * lam_i)
    pi = jnp.exp(tau[:, None, None] * lam_r) * jnp.sin(tau[:, None, None] * lam_i)
    qr = jnp.exp(lam_r[..., None] * tau) * jnp.cos(lam_i[..., None] * tau)
    qi = jnp.exp(lam_r[..., None] * tau) * jnp.sin(lam_i[..., None] * tau)
    abr = pr[:, :, None, :] * bb_re - pi[:, :, None, :] * bb_im
    abi = pr[:, :, None, :] * bb_im + pi[:, :, None, :] * bb_re
    cr, ci = c_re.astype(F32), c_im.astype(F32)
    hp = lax.Precision.HIGHEST
    km = (jnp.einsum('tgcp,gdp->gctd', abr[:t_], cr, precision=hp)
          - jnp.einsum('tgcp,gdp->gctd', abi[:t_], ci, precision=hp))
    tt, jj, ii = np.arange(t_)[:, None, None], np.arange(t_)[None, :, None], np.arange(t_)[None, None, :]
    shift = np.einsum('tji,de->tdjie', (tt == ii - jj).astype(np.float32), np.eye(gc, dtype=np.float32))
    shift = jnp.asarray(shift.reshape(t_ * gc, t_ * t_ * gc), BF16)
    r_t = jnp.dot(km.astype(BF16).reshape(S5_GROUPS * gc, t_ * gc), shift, preferred_element_type=BF16)
    r_t = r_t.reshape(S5_LT, S5_GPT, gc, t_, t_ * gc).transpose(0, 3, 1, 2, 4).reshape(S5_LT, t_ * LANE, t_ * gc)
    rev = np.arange(t_ - 1, -1, -1)
    r_w = jnp.concatenate([abr[rev], abi[rev]], axis=-1).astype(BF16)
    r_w = r_w.reshape(t_, S5_LT, S5_GPT, gc, 2 * gp).transpose(1, 0, 2, 3, 4).reshape(S5_LT, t_ * LANE, 2 * gp)
    crt, cit = cr.transpose(0, 2, 1)[:, :, None, :], ci.transpose(0, 2, 1)[:, :, None, :]
    q1r, q1i = qr[:, :, 1:, None], qi[:, :, 1:, None]
    vr = (crt * q1r - cit * q1i).astype(BF16).reshape(S5_LT, S5_SW, t_ * gc)
    vi = (-(crt * q1i + cit * q1r)).astype(BF16).reshape(S5_LT, S5_SW, t_ * gc)
    r_v = jnp.concatenate([vr, vi], axis=1)
    src, dst = np.arange(t_ * gc), np.arange(t_ * LANE)
    rep = jnp.asarray((src[:, None] // gc == dst[None, :] // LANE) & (src[:, None] % gc == dst[None, :] % gc), BF16)
    src, dst = np.arange(2 * gp), np.arange(2 * S5_SW)
    repw = jnp.asarray((src[:, None] // gp == dst[None, :] // S5_SW) & (src[:, None] % gp == dst[None, :] % gp), BF16)
    io_group = lambda x: (x // gc) % S5_GPT
    st_group = lambda x: (x % S5_SW) // gp
    toep = _s5_expand(r_t, rep, io_group, io_group)
    wz = _s5_expand(r_w, repw, io_group, st_group)
    vv = _s5_expand(r_v, rep, st_group, io_group)
    a_t = jnp.stack([pr[t_].reshape(S5_LT, S5_SW), pi[t_].reshape(S5_LT, S5_SW)], axis=1)
    return toep, wz, vv, a_t


def _s5_body(u_ref, toep_ref, wz_ref, vv_ref, at_ref, o_ref, z_ref, xp_ref, st_ref, *, n_rows):
    @pl.when(pl.program_id(2) == 0)
    def _():
        st_ref[...] = jnp.zeros_like(st_ref)

    fold = lambda j: pl.ds(j, n_rows, stride=S5_T)
    ub = jnp.concatenate([u_ref[fold(j), :] for j in range(S5_T)], axis=-1).astype(BF16)
    z_ref[...] = _dot(ub, wz_ref[...])
    ar, ai = at_ref[0:1, :], at_ref[1:2, :]
    re, im = slice(0, S5_SW), slice(S5_SW, 2 * S5_SW)

    def step(c, carry):
        xr, xi = carry
        row = pl.ds(c, 1)
        xp_ref[row, re] = xr
        xp_ref[row, im] = xi
        return (xr * ar - xi * ai + z_ref[row, re], xr * ai + xi * ar + z_ref[row, im])

    xr, xi = lax.fori_loop(0, n_rows, step, (st_ref[0:1, re], st_ref[0:1, im]), unroll=8)
    st_ref[0:1, re] = xr
    st_ref[0:1, im] = xi
    y = _dot(ub, toep_ref[...]) + _dot(xp_ref[...].astype(BF16), vv_ref[...])
    for i in range(S5_T):
        o_ref[fold(i), :] = y[:, i * LANE:(i + 1) * LANE]


def _s5(u4, consts, layer, batch, n_rows):
    toep, wz, vv, a_t = consts
    _, n, w = u4.shape
    per_batch = n // batch // (n_rows * S5_T)
    blk = pl.BlockSpec((None, n_rows * S5_T, w), lambda k, bi, i: (k, bi * per_batch + i, 0))
    wspec = lambda a: pl.BlockSpec((None, None) + a.shape[2:], lambda k, bi, i: (layer, k, 0, 0))
    return pl.pallas_call(
        functools.partial(_s5_body, n_rows=n_rows),
        grid=(S5_LT, batch, per_batch),
        in_specs=[blk, wspec(toep), wspec(wz), wspec(vv), wspec(a_t)],
        out_specs=blk,
        out_shape=jax.ShapeDtypeStruct(u4.shape, F32),
        scratch_shapes=[pltpu.VMEM((n_rows, 2 * S5_SW), F32), pltpu.VMEM((n_rows, 2 * S5_SW), F32),
                        pltpu.VMEM((SUBLANE, 2 * S5_SW), F32)],
        compiler_params=_params(("arbitrary", "arbitrary", "arbitrary")),
        name="s5",
    )(u4, toep, wz, vv, a_t)


def _outproj_ffn_body(h_ref, oa_ref, ob_ref, *rest):
    y_refs, u_refs = rest[0:S5_LT], rest[S5_LT:2 * S5_LT]
    d_ref, gw_ref, gb_ref, wo_ref, g_ref, gpre_ref, wg_ref, wu_ref, wd_ref, gpost_ref, o_ref = rest[2 * S5_LT:]
    yc = jnp.concatenate([r[...] for r in y_refs], axis=-1)
    u = jnp.concatenate([r[...] for r in u_refs], axis=-1)
    y = jax.nn.gelu(yc + d_ref[...] * u)
    oc = y * jax.nn.sigmoid(_dot(y.astype(BF16), gw_ref[...]) + gb_ref[...])
    mix = (_dot(oa_ref[...], wo_ref[0:D_HG, :])
           + _dot(ob_ref[...], wo_ref[D_HG:D_HG + D_SSD, :])
           + _dot(oc.astype(BF16), wo_ref[D_HG + D_SSD:D_MIX, :]))
    x = h_ref[...] + _rms(mix, g_ref[...])
    o_ref[...] = _ffn_apply(x, gpre_ref[...], wg_ref, wu_ref, wd_ref, gpost_ref[...])


def _outproj_ffn(h, oa, ob, y4, u4, s5_d, glu_w, glu_b, w_out, ng, wg, wu, wd, layer, tm):
    n, d = h.shape
    tok = lambda w: pl.BlockSpec((tm, w), lambda i: (i, 0))
    tile = lambda k: pl.BlockSpec((None, tm, LANE), lambda i: (k, i, 0))
    tiles = [tile(k) for k in range(S5_LT)]
    once = dict(pipeline_mode=pl.Buffered(1))
    norm = lambda k: pl.BlockSpec((None, 1, d), lambda i: (layer * 6 + k, 0, 0))
    weight = lambda r, c: pl.BlockSpec((None, None, r, c), lambda i: (layer, 1, 0, 0), **once)
    return pl.pallas_call(
        _outproj_ffn_body,
        grid=(n // tm,),
        in_specs=[tok(d), tok(D_HG), tok(D_SSD)] + tiles + tiles + [
            pl.BlockSpec((None, 1, D_S5), lambda i: (layer, 0, 0)),
            pl.BlockSpec((None, D_S5, D_S5), lambda i: (layer, 0, 0), **once),
            pl.BlockSpec((None, 1, D_S5), lambda i: (layer, 0, 0)),
            pl.BlockSpec((None, D_MIX, d), lambda i: (layer, 0, 0), **once),
            norm(3), norm(4), weight(d, D_FF), weight(d, D_FF), weight(D_FF, d), norm(5),
        ],
        out_specs=tok(d),
        out_shape=jax.ShapeDtypeStruct((n, d), F32),
        compiler_params=_params(("arbitrary",)),
        name="outproj_ffn",
    )(h, oa, ob, *([y4] * S5_LT), *([u4] * S5_LT), s5_d, glu_w, glu_b, w_out, ng, ng, wg, wu, wd, ng)


def _tile(n, pref):
    return pref if n % pref == 0 else n


def kernel(x, norm_g, ffn_w_gate, ffn_w_up, ffn_w_down, w_in, w_out, hg_lb_logits, hg_gnorm, ssd_conv_w, ssd_conv_b,
           ssd_dt_bias, ssd_A_log, ssd_D, ssd_norm, s5_A_re, s5_A_im, s5_B_re, s5_B_im, s5_C_re, s5_C_im, s5_D,
           s5_log_dt, s5_glu_w, s5_glu_b):
    bsz, seq, d = x.shape
    depth = norm_g.shape[0]
    n = bsz * seq
    assert d == D_MODEL and seq % (CHUNK * S5_T) == 0 and w_in.shape[-1] == COL_TAIL + SSD_HEADS + D_S5

    ng = norm_g.astype(F32).reshape(depth * 6, 1, d)
    wg, wu, wd = ffn_w_gate.astype(BF16), ffn_w_up.astype(BF16), ffn_w_down.astype(BF16)
    w_in_b = w_in.astype(BF16)
    w_tail = jnp.concatenate([
        w_in_b[..., COL_TAIL + SSD_HEADS:], w_in_b[..., COL_TAIL:COL_TAIL + SSD_HEADS],
        jnp.zeros(w_in.shape[:-1] + (LANE - SSD_HEADS,), BF16)], axis=-1)
    w_out_b = w_out.astype(BF16)
    glu_w_b = s5_glu_w.astype(BF16)
    lb = jnp.cumsum(jax.nn.softmax(hg_lb_logits.astype(F32), axis=0), axis=0)
    lb = (lb - lb[:1])[:, None, :]
    loglb, l1m, omlb = jnp.log(lb), jnp.log1p(-lb), 1.0 - lb
    gn = hg_gnorm.astype(F32)[:, None, :]
    a_rep = jnp.repeat(-jnp.exp(ssd_A_log.astype(F32)) * LOG2E, SSD_HEADDIM, axis=-1)[:, None, :]
    d_rep = jnp.repeat(ssd_D.astype(F32), SSD_HEADDIM, axis=-1)[:, None, :]
    dtb = jnp.pad(ssd_dt_bias.astype(F32), ((0, 0), (0, LANE - SSD_HEADS)))[:, None, :]
    cw, cb, nw = ssd_conv_w.astype(F32), ssd_conv_b.astype(F32)[:, None, :], ssd_norm.astype(F32)[:, None, :]
    s5_c = jax.vmap(_s5_consts)(s5_A_re, s5_A_im, s5_B_re, s5_B_im, s5_C_re, s5_C_im, s5_log_dt)
    s5_d, glu_b = s5_D.astype(F32)[:, None, :], s5_glu_b.astype(F32)[:, None, :]
    hg_c = _hgrn_consts()
    ssd_c = _ssd_consts()

    tm = _tile(n, 512)
    tl_h = _tile(seq, 512)
    tl_s = _tile(seq, 512)
    s5_rows = _tile(seq // S5_T, 512)

    h = x.astype(F32).reshape(n, d)
    for l in range(depth):
        h = _ffn(h, ng, wg, wu, wd, l, tm)
        qfig, xbc, z, u4, dt = _inproj(h, ng, w_in_b, w_tail, l, tm)
        tok3 = lambda a: a.reshape(bsz, seq, a.shape[-1])
        oa = _hgrn(tok3(qfig), hg_c, loglb[l], l1m[l], omlb[l], gn[l], tl_h)
        ob = _ssd(tok3(xbc), tok3(z), tok3(dt), ssd_c, cw[l], cb[l], dtb[l], a_rep[l], d_rep[l], nw[l], tl_s)
        y4 = _s5(u4, s5_c, l, bsz, s5_rows)
        h = _outproj_ffn(h, oa.reshape(n, D_HG), ob.reshape(n, D_SSD), y4, u4, s5_d, glu_w_b, glu_b, w_out_b, ng,
                         wg, wu, wd, l, tm)
    return h.reshape(bsz, seq, d).astype(x.dtype)
```

```python
import functools

import numpy as np
import jax
import jax.numpy as jnp
from jax import lax
from jax.experimental import pallas as pl
from jax.experimental.pallas import tpu as pltpu

F32 = jnp.float32
BF16 = jnp.bfloat16
EPS = 1e-6
LOG2E = 1.4426950408889634

D_MODEL = 1024
D_FF = 2816
HG_HEADS, HG_DK = 4, 128
D_HG = HG_HEADS * HG_DK
SSD_HEADS, SSD_HEADDIM, SSD_GROUPS, SSD_STATE, SSD_CONV = 8, 64, 2, 128, 4
D_SSD = SSD_HEADS * SSD_HEADDIM
D_XBC = D_SSD + 2 * SSD_GROUPS * SSD_STATE
S5_GROUPS, S5_GROUP_CH, S5_STATE = 32, 16, 64
D_S5 = S5_GROUPS * S5_GROUP_CH
D_MIX = D_HG + D_SSD + D_S5
CHUNK = 64
LANE = 128
SUBLANE = 8
S5_T = 8
S5_LT = D_S5 // LANE
S5_GPT = LANE // S5_GROUP_CH
S5_SW = S5_GPT * S5_STATE

COL_Z, COL_XBC, COL_TAIL = 4 * D_HG, 4 * D_HG + D_SSD, 4 * D_HG + D_SSD + D_XBC
D_TAIL = D_S5 + LANE
VMEM_LIMIT = 56 * 1024 * 1024


def _dot(a, b):
    return jnp.dot(a, b, preferred_element_type=F32)


def _dot_nt(a, b):
    return lax.dot_general(a, b, (((1,), (1,)), ((), ())), preferred_element_type=F32)


def _dot_tn(a, b):
    return lax.dot_general(a, b, (((0,), (0,)), ((), ())), preferred_element_type=F32)


def _split(x):
    hi = x.astype(BF16)
    return hi, (x - hi.astype(F32)).astype(BF16)


def _dot_split(m2, x):
    hi, lo = _split(x)
    return _dot(m2, jnp.concatenate([hi, lo], axis=0))


def _dot_split_r(x, m2):
    hi, lo = _split(x)
    return _dot(jnp.concatenate([hi, lo], axis=1), m2)


def _rms(x, g):
    return x * lax.rsqrt(jnp.mean(x * x, axis=-1, keepdims=True) + EPS) * g


def _silu(x):
    return x * jax.nn.sigmoid(x)


def _params(sem):
    return pltpu.CompilerParams(dimension_semantics=sem, vmem_limit_bytes=VMEM_LIMIT)


def _full(a):
    return pl.BlockSpec(a.shape, lambda *_: (0,) * a.ndim)


def _ffn_apply(x, gpre, wg_ref, wu_ref, wd_ref, gpost):
    xn = _rms(x, gpre).astype(BF16)
    act = (_silu(_dot(xn, wg_ref[...])) * _dot(xn, wu_ref[...])).astype(BF16)
    return x + 0.5 * _rms(_dot(act, wd_ref[...]), gpost)


def _ffn_body(x_ref, gpre_ref, wg_ref, wu_ref, wd_ref, gpost_ref, o_ref):
    o_ref[...] = _ffn_apply(x_ref[...], gpre_ref[...], wg_ref, wu_ref, wd_ref, gpost_ref[...])


def _ffn(h, ng, wg, wu, wd, layer, tm):
    n, d = h.shape
    weight = lambda r, c: pl.BlockSpec((None, None, r, c), lambda i: (layer, 0, 0, 0),
                                       pipeline_mode=pl.Buffered(1))
    return pl.pallas_call(
        _ffn_body,
        grid=(n // tm,),
        in_specs=[
            pl.BlockSpec((tm, d), lambda i: (i, 0)),
            pl.BlockSpec((None, 1, d), lambda i: (layer * 6, 0, 0)),
            weight(d, D_FF), weight(d, D_FF), weight(D_FF, d),
            pl.BlockSpec((None, 1, d), lambda i: (layer * 6 + 1, 0, 0)),
        ],
        out_specs=pl.BlockSpec((tm, d), lambda i: (i, 0)),
        out_shape=jax.ShapeDtypeStruct((n, d), F32),
        compiler_params=_params(("arbitrary",)),
        name="ffn",
    )(h, ng, wg, wu, wd, ng)


def _inproj_body(x_ref, g_ref, w_ref, wt_ref, qfig_ref, xbc_ref, z_ref, u_ref, dt_ref):
    xn = _rms(x_ref[...], g_ref[...]).astype(BF16)
    qfig_ref[...] = _dot(xn, w_ref[:, 0:COL_Z])
    z_ref[...] = _dot(xn, w_ref[:, COL_Z:COL_XBC])
    xbc_ref[...] = _dot(xn, w_ref[:, COL_XBC:COL_TAIL])
    u = _dot(xn, wt_ref[:, 0:D_S5])
    for k in range(S5_LT):
        u_ref[k] = u[:, k * LANE:(k + 1) * LANE]
    dt_ref[...] = _dot(xn, wt_ref[:, D_S5:D_TAIL])


def _inproj(h, ng, w_in_b, w_tail, layer, tm):
    n, d = h.shape
    tok = lambda w: pl.BlockSpec((tm, w), lambda i: (i, 0))
    out_w = (4 * D_HG, D_XBC, D_SSD)
    return pl.pallas_call(
        _inproj_body,
        grid=(n // tm,),
        in_specs=[
            tok(d),
            pl.BlockSpec((None, 1, d), lambda i: (layer * 6 + 2, 0, 0)),
            pl.BlockSpec((None, d, w_in_b.shape[-1]), lambda i: (layer, 0, 0)),
            pl.BlockSpec((None, d, D_TAIL), lambda i: (layer, 0, 0)),
        ],
        out_specs=[tok(w) for w in out_w] + [pl.BlockSpec((S5_LT, tm, LANE), lambda i: (0, i, 0)), tok(LANE)],
        out_shape=[jax.ShapeDtypeStruct((n, w), F32) for w in out_w]
        + [jax.ShapeDtypeStruct((S5_LT, n, LANE), F32), jax.ShapeDtypeStruct((n, LANE), F32)],
        compiler_params=_params(("arbitrary",)),
        name="inproj",
    )(h, ng, w_in_b, w_tail)


HG_LEVELS = (32, 16, 8, 4, 2)


def _block_tri_up(m, n=CHUNK):
    t = np.arange(n)
    same = (t[:, None] // m) == (t[None, :] // m)
    tri = same & (t[None, :] <= t[:, None])
    up = same & (t[None, :] > t[:, None])
    return tri.astype(np.float32), up.astype(np.float32)


def _hgrn_consts():
    mats, masks = [], []
    for m in (CHUNK,) + HG_LEVELS:
        mats += list(_block_tri_up(m))
    t = np.arange(CHUNK)
    for m in HG_LEVELS + (1,):
        bt, bs = t[:, None] // m, t[None, :] // m
        masks.append(((bt % 2 == 1) & (bs == bt - 1)).astype(np.float32))
    masks.append(np.eye(CHUNK, dtype=np.float32))
    cm = np.concatenate(mats, 0)
    return jnp.asarray(np.concatenate([cm, cm], 1), BF16), jnp.asarray(np.stack(masks, 0), F32)


def _hgrn_body(p_ref, cm_ref, mask_ref, loglb_ref, l1m_ref, omlb_ref, gn_ref, o_ref, st_ref, *, n_chunks):
    @pl.when(pl.program_id(0) == 0)
    def _():
        st_ref[...] = jnp.zeros_like(st_ref)

    cm = cm_ref[...]
    loglb, l1m, omlb, gn = loglb_ref[...], l1m_ref[...], omlb_ref[...], gn_ref[...]
    nl = len(HG_LEVELS)

    def chunk_one(b, rows):
        q = _silu(p_ref[b, rows, 0:D_HG])
        fr = p_ref[b, rows, D_HG:2 * D_HG]
        v = p_ref[b, rows, 2 * D_HG:3 * D_HG].astype(BF16)
        gr = p_ref[b, rows, 3 * D_HG:4 * D_HG]
        ls = jnp.minimum(fr, 0.0) - jnp.log1p(jnp.exp(-jnp.abs(fr)))
        bb = l1m + ls
        lf = jnp.maximum(loglb, bb) + jnp.log1p(jnp.exp(-jnp.abs(loglb - bb)))
        kk = omlb * jnp.exp(ls - fr)
        lf2 = lf * LOG2E
        cs = _dot_split(cm, lf2)
        kb = kk.astype(BF16)
        q01 = jnp.concatenate([q, q * jnp.exp2(lf2)], axis=0).astype(BF16)
        outs = []
        for h in range(HG_HEADS):
            hs = slice(h * HG_DK, (h + 1) * HG_DK)
            qh, kh = q[:, hs], kk[:, hs]
            st = st_ref[b, h]
            c64, r64 = cs[0:CHUNK, hs], cs[CHUNK:2 * CHUNK, hs]
            p01 = _dot_nt(q01[:, hs], kb[:, hs])
            scores = mask_ref[nl + 1] * p01[0:CHUNK] + mask_ref[nl] * p01[CHUNK:2 * CHUNK]
            for li in range(nl):
                base = 2 * CHUNK * (li + 1)
                cmm, rmm = cs[base:base + CHUNK, hs], cs[base + CHUNK:base + 2 * CHUNK, hs]
                p = _dot_nt((qh * jnp.exp2(cmm)).astype(BF16), (kh * jnp.exp2(rmm)).astype(BF16))
                scores = scores + mask_ref[li] * p
            o = _dot_nt((qh * jnp.exp2(c64)).astype(BF16), st.astype(BF16)) + _dot(scores.astype(BF16), v[:, hs])
            ke = (kh * jnp.exp2(r64)).astype(BF16)
            st_ref[b, h] = jnp.exp2(c64[CHUNK - 1:CHUNK, :]) * st + _dot_tn(v[:, hs], ke)
            outs.append(o * lax.rsqrt(jnp.mean(o * o, axis=-1, keepdims=True) + EPS) * gn)
        o_ref[b, rows, :] = (jnp.concatenate(outs, axis=-1) * _silu(gr)).astype(o_ref.dtype)

    def chunk_step(c, carry):
        rows = pl.ds(pl.multiple_of(c * CHUNK, CHUNK), CHUNK)
        for b in range(p_ref.shape[0]):
            chunk_one(b, rows)
        return carry

    lax.fori_loop(0, n_chunks, chunk_step, 0, unroll=2)


def _hgrn(qfig3, consts, loglb, l1m, omlb, gn, tl):
    b, l, _ = qfig3.shape
    cm, masks = consts
    return pl.pallas_call(
        functools.partial(_hgrn_body, n_chunks=tl // CHUNK),
        grid=(l // tl,),
        in_specs=[pl.BlockSpec((b, tl, 4 * D_HG), lambda i: (0, i, 0)),
                  _full(cm), _full(masks), _full(loglb), _full(l1m), _full(omlb), _full(gn)],
        out_specs=pl.BlockSpec((b, tl, D_HG), lambda i: (0, i, 0)),
        out_shape=jax.ShapeDtypeStruct((b, l, D_HG), BF16),
        scratch_shapes=[pltpu.VMEM((b, HG_HEADS, HG_DK, HG_DK), F32)],
        compiler_params=_params(("arbitrary",)),
        name="hgrn2",
    )(qfig3, cm, masks, loglb, l1m, omlb, gn)


def _ssd_consts():
    tri, up = _block_tri_up(CHUNK)
    t = np.arange(CHUNK)
    su = (t[:, None] > t[None, :]).astype(np.float32)
    causal = (t[:, None] >= t[None, :]).astype(np.float32)
    expand = np.zeros((LANE, D_SSD), np.float32)
    for h in range(SSD_HEADS):
        expand[h, h * SSD_HEADDIM:(h + 1) * SSD_HEADDIM] = 1.0
    tu = np.concatenate([tri, up], 0)
    return (jnp.asarray(np.concatenate([tu, tu], 1), BF16), jnp.asarray(np.tile(su, (1, SSD_HEADS)), F32),
            jnp.asarray(causal, F32), jnp.asarray(np.concatenate([expand, expand], 0), BF16))


def _ssd_body(xbc_ref, z_ref, dt_ref, cw_ref, cb_ref, dtb_ref, a_ref, d_ref, nw_ref, tu_ref, su_ref, causal_ref,
              ex_ref, o_ref, st_ref, ext_ref, *, tl):
    pad = SUBLANE
    nb = xbc_ref.shape[0]

    @pl.when(pl.program_id(0) == 0)
    def _():
        st_ref[...] = jnp.zeros_like(st_ref)
        ext_ref[:, 0:pad, :] = jnp.zeros((nb, pad, D_XBC), F32)

    ext_ref[:, pad:pad + tl, :] = xbc_ref[...]

    tu, su, causal, expand = tu_ref[...], su_ref[...], causal_ref[...], ex_ref[...]
    a_rep, d_rep, nw, cb, dtb = a_ref[...], d_ref[...], nw_ref[...], cb_ref[...], dtb_ref[...]
    gw = D_SSD // SSD_GROUPS
    hpg = SSD_HEADS // SSD_GROUPS

    def chunk_one(b, r0):
        rows = pl.ds(r0, CHUNK)
        win = ext_ref[b, pl.ds(r0, CHUNK + pad), :]
        conv = cb + cw_ref[SSD_CONV - 1:SSD_CONV, :] * win[pad:pad + CHUNK, :]
        for j in range(SSD_CONV - 1):
            conv = conv + cw_ref[j:j + 1, :] * pltpu.roll(win, SSD_CONV - 1 - j, 0)[pad:pad + CHUNK, :]
        xs = _silu(conv)
        x = xs[:, 0:D_SSD]
        dtin = dt_ref[b, rows, :] + dtb
        sp = jnp.maximum(dtin, 0.0) + jnp.log1p(jnp.exp(-jnp.abs(dtin)))
        dtr = _dot_split_r(sp, expand)
        adt = dtr * a_rep
        r = _dot_split(tu, jnp.concatenate([adt, adt * su], axis=-1))
        acs, rs = r[0:CHUNK, 0:D_SSD], r[CHUNK:2 * CHUNK, 0:D_SSD]
        dm = r[0:CHUNK, D_SSD:2 * D_SSD]
        xdt = x * dtr
        xdt_b = xdt.astype(BF16)
        ys = []
        for g in range(SSD_GROUPS):
            gs = slice(g * gw, (g + 1) * gw)
            bg = xs[:, D_SSD + g * SSD_STATE:D_SSD + (g + 1) * SSD_STATE].astype(BF16)
            cg = xs[:, D_SSD + (SSD_GROUPS + g) * SSD_STATE:D_SSD + (SSD_GROUPS + g + 1) * SSD_STATE].astype(BF16)
            st = st_ref[b, g]
            scores = _dot_nt(cg, bg) * causal
            y_off = _dot(cg, st.astype(BF16)) * jnp.exp2(acs[:, gs])
            yd = []
            for hh in range(hpg):
                sl = slice((g * hpg + hh) * SSD_HEADDIM, (g * hpg + hh + 1) * SSD_HEADDIM)
                yd.append(_dot((scores * jnp.exp2(dm[:, sl])).astype(BF16), xdt_b[:, sl]))
            ys.append(jnp.concatenate(yd, axis=-1) + y_off)
            xd = (xdt[:, gs] * jnp.exp2(rs[:, gs])).astype(BF16)
            st_ref[b, g] = jnp.exp2(acs[CHUNK - 1:CHUNK, gs]) * st + _dot_tn(bg, xd)
        y = jnp.concatenate(ys, axis=-1) + x * d_rep
        y = y * _silu(z_ref[b, rows, :])
        yn = [y[:, g * gw:(g + 1) * gw] * lax.rsqrt(
            jnp.mean(y[:, g * gw:(g + 1) * gw] * y[:, g * gw:(g + 1) * gw], axis=-1, keepdims=True) + EPS)
            for g in range(SSD_GROUPS)]
        o_ref[b, rows, :] = (jnp.concatenate(yn, axis=-1) * nw).astype(o_ref.dtype)

    def chunk_step(c, carry):
        r0 = pl.multiple_of(c * CHUNK, CHUNK)
        for b in range(nb):
            chunk_one(b, r0)
        return carry

    lax.fori_loop(0, tl // CHUNK, chunk_step, 0, unroll=2)
    ext_ref[:, 0:pad, :] = xbc_ref[:, tl - pad:tl, :]


def _ssd(xbc3, z3, dt3, consts, cw, cb, dtb, a_rep, d_rep, nw, tl):
    b, l, _ = xbc3.shape
    tok = lambda w: pl.BlockSpec((b, tl, w), lambda i: (0, i, 0))
    params = (cw, cb, dtb, a_rep, d_rep, nw) + tuple(consts)
    return pl.pallas_call(
        functools.partial(_ssd_body, tl=tl),
        grid=(l // tl,),
        in_specs=[tok(D_XBC), tok(D_SSD), tok(LANE)] + [_full(a) for a in params],
        out_specs=tok(D_SSD),
        out_shape=jax.ShapeDtypeStruct((b, l, D_SSD), BF16),
        scratch_shapes=[
            pltpu.VMEM((b, SSD_GROUPS, SSD_STATE, D_SSD // SSD_GROUPS), F32),
            pltpu.VMEM((b, tl + SUBLANE, D_XBC), F32),
        ],
        compiler_params=_params(("arbitrary",)),
        name="ssd",
    )(xbc3, z3, dt3, *params)


def _s5_expand(r, rep, row_group, col_group):
    big = jnp.dot(r, rep, preferred_element_type=BF16)
    rows, cols = big.shape[-2:]
    rg = row_group(lax.broadcasted_iota(jnp.int32, (rows, cols), 0))
    cg = col_group(lax.broadcasted_iota(jnp.int32, (rows, cols), 1))
    return jnp.where(rg == cg, big, jnp.zeros_like(big))


def _s5_consts(a_re, a_im, b_re, b_im, c_re, c_im, log_dt):
    t_, gc, gp = S5_T, S5_GROUP_CH, S5_STATE
    ar, ai = a_re.astype(F32), a_im.astype(F32)
    delta = jnp.exp(log_dt.astype(F32))[:, None]
    mag = jnp.exp(ar * delta)
    ab_re, ab_im = mag * jnp.cos(ai * delta), mag * jnp.sin(ai * delta)
    den = ar * ar + ai * ai
    nr, ni = ab_re - 1.0, ab_im
    fr = (nr * ar + ni * ai) / den
    fi = (ni * ar - nr * ai) / den
    br, bi = b_re.astype(F32).transpose(0, 2, 1), b_im.astype(F32).transpose(0, 2, 1)
    bb_re = fr[:, None, :] * br - fi[:, None, :] * bi
    bb_im = fr[:, None, :] * bi + fi[:, None, :] * br
    lam_r, lam_i = ar * delta, ai * delta
    tau = jnp.arange(t_ + 1, dtype=F32)
    pr = jnp.exp(tau[:, None, None] * lam_r) * jnp.cos(tau[:, None, None] * lam_i)
    pi = jnp.exp(tau[:, None, None] * lam_r) * jnp.sin(tau[:, None, None] * lam_i)
    qr = jnp.exp(lam_r[..., None] * tau) * jnp.cos(lam_i[..., None] * tau)
    qi = jnp.exp(lam_r[..., None] * tau) * jnp.sin(lam_i[..., None] * tau)
    abr = pr[:, :, None, :] * bb_re - pi[:, :, None, :] * bb_im
    abi = pr[:, :, None, :] * bb_im + pi[:, :, None, :] * bb_re
    cr, ci = c_re.astype(F32), c_im.astype(F32)
    hp = lax.Precision.HIGHEST
    km = (jnp.einsum('tgcp,gdp->gctd', abr[:t_], cr, precision=hp)
          - jnp.einsum('tgcp,gdp->gctd', abi[:t_], ci, precision=hp))
    tt, jj, ii = np.arange(t_)[:, None, None], np.arange(t_)[None, :, None], np.arange(t_)[None, None, :]
    shift = np.einsum('tji,de->tdjie', (tt == ii - jj).astype(np.float32), np.eye(gc, dtype=np.float32))
    shift = jnp.asarray(shift.reshape(t_ * gc, t_ * t_ * gc), BF16)
    r_t = jnp.dot(km.astype(BF16).reshape(S5_GROUPS * gc, t_ * gc), shift, preferred_element_type=BF16)
    r_t = r_t.reshape(S5_LT, S5_GPT, gc, t_, t_ * gc).transpose(0, 3, 1, 2, 4).reshape(S5_LT, t_ * LANE, t_ * gc)
    rev = np.arange(t_ - 1, -1, -1)
    r_w = jnp.concatenate([abr[rev], abi[rev]], axis=-1).astype(BF16)
    r_w = r_w.reshape(t_, S5_LT, S5_GPT, gc, 2 * gp).transpose(1, 0, 2, 3, 4).reshape(S5_LT, t_ * LANE, 2 * gp)
    crt, cit = cr.transpose(0, 2, 1)[:, :, None, :], ci.transpose(0, 2, 1)[:, :, None, :]
    q1r, q1i = qr[:, :, 1:, None], qi[:, :, 1:, None]
    vr = (crt * q1r - cit * q1i).astype(BF16).reshape(S5_LT, S5_SW, t_ * gc)
    vi = (-(crt * q1i + cit * q1r)).astype(BF16).reshape(S5_LT, S5_SW, t_ * gc)
    r_v = jnp.concatenate([vr, vi], axis=1)
    src, dst = np.arange(t_ * gc), np.arange(t_ * LANE)
    rep = jnp.asarray((src[:, None] // gc == dst[None, :] // LANE) & (src[:, None] % gc == dst[None, :] % gc), BF16)
    src, dst = np.arange(2 * gp), np.arange(2 * S5_SW)
    repw = jnp.asarray((src[:, None] // gp == dst[None, :] // S5_SW) & (src[:, None] % gp == dst[None, :] % gp), BF16)
    io_group = lambda x: (x // gc) % S5_GPT
    st_group = lambda x: (x % S5_SW) // gp
    toep = _s5_expand(r_t, rep, io_group, io_group)
    wz = _s5_expand(r_w, repw, io_group, st_group)
    vv = _s5_expand(r_v, rep, st_group, io_group)
    a_t = jnp.stack([pr[t_].reshape(S5_LT, S5_SW), pi[t_].reshape(S5_LT, S5_SW)], axis=1)
    return toep, wz, vv, a_t


def _s5_body(u_ref, toep_ref, wz_ref, vv_ref, at_ref, o_ref, z_ref, xp_ref, st_ref, *, n_rows):
    @pl.when(pl.program_id(2) == 0)
    def _():
        st_ref[...] = jnp.zeros_like(st_ref)

    fold = lambda j: pl.ds(j, n_rows, stride=S5_T)
    ub = jnp.concatenate([u_ref[fold(j), :] for j in range(S5_T)], axis=-1).astype(BF16)
    z_ref[...] = _dot(ub, wz_ref[...])
    ar, ai = at_ref[0:1, :], at_ref[1:2, :]
    re, im = slice(0, S5_SW), slice(S5_SW, 2 * S5_SW)

    def step(c, carry):
        xr, xi = carry
        row = pl.ds(c, 1)
        xp_ref[row, re] = xr
        xp_ref[row, im] = xi
        return (xr * ar - xi * ai + z_ref[row, re], xr * ai + xi * ar + z_ref[row, im])

    xr, xi = lax.fori_loop(0, n_rows, step, (st_ref[0:1, re], st_ref[0:1, im]), unroll=8)
    st_ref[0:1, re] = xr
    st_ref[0:1, im] = xi
    y = _dot(ub, toep_ref[...]) + _dot(xp_ref[...].astype(BF16), vv_ref[...])
    for i in range(S5_T):
        o_ref[fold(i), :] = y[:, i * LANE:(i + 1) * LANE]


def _s5(u4, consts, layer, batch, n_rows):
    toep, wz, vv, a_t = consts
    _, n, w = u4.shape
    per_batch = n // batch // (n_rows * S5_T)
    blk = pl.BlockSpec((None, n_rows * S5_T, w), lambda k, bi, i: (k, bi * per_batch + i, 0))
    wspec = lambda a: pl.BlockSpec((None, None) + a.shape[2:], lambda k, bi, i: (layer, k, 0, 0))
    return pl.pallas_call(
        functools.partial(_s5_body, n_rows=n_rows),
        grid=(S5_LT, batch, per_batch),
        in_specs=[blk, wspec(toep), wspec(wz), wspec(vv), wspec(a_t)],
        out_specs=blk,
        out_shape=jax.ShapeDtypeStruct(u4.shape, F32),
        scratch_shapes=[pltpu.VMEM((n_rows, 2 * S5_SW), F32), pltpu.VMEM((n_rows, 2 * S5_SW), F32),
                        pltpu.VMEM((SUBLANE, 2 * S5_SW), F32)],
        compiler_params=_params(("arbitrary", "arbitrary", "arbitrary")),
        name="s5",
    )(u4, toep, wz, vv, a_t)


def _outproj_ffn_body(h_ref, oa_ref, ob_ref, *rest):
    y_refs, u_refs = rest[0:S5_LT], rest[S5_LT:2 * S5_LT]
    d_ref, gw_ref, gb_ref, wo_ref, g_ref, gpre_ref, wg_ref, wu_ref, wd_ref, gpost_ref, o_ref = rest[2 * S5_LT:]
    yc = jnp.concatenate([r[...] for r in y_refs], axis=-1)
    u = jnp.concatenate([r[...] for r in u_refs], axis=-1)
    y = jax.nn.gelu(yc + d_ref[...] * u)
    oc = y * jax.nn.sigmoid(_dot(y.astype(BF16), gw_ref[...]) + gb_ref[...])
    mix = (_dot(oa_ref[...], wo_ref[0:D_HG, :])
           + _dot(ob_ref[...], wo_ref[D_HG:D_HG + D_SSD, :])
           + _dot(oc.astype(BF16), wo_ref[D_HG + D_SSD:D_MIX, :]))
    x = h_ref[...] + _rms(mix, g_ref[...])
    o_ref[...] = _ffn_apply(x, gpre_ref[...], wg_ref, wu_ref, wd_ref, gpost_ref[...])


def _outproj_ffn(h, oa, ob, y4, u4, s5_d, glu_w, glu_b, w_out, ng, wg, wu, wd, layer, tm):
    n, d = h.shape
    tok = lambda w: pl.BlockSpec((tm, w), lambda i: (i, 0))
    tile = lambda k: pl.BlockSpec((None, tm, LANE), lambda i: (k, i, 0))
    tiles = [tile(k) for k in range(S5_LT)]
    once = dict(pipeline_mode=pl.Buffered(1))
    norm = lambda k: pl.BlockSpec((None, 1, d), lambda i: (layer * 6 + k, 0, 0))
    weight = lambda r, c: pl.BlockSpec((None, None, r, c), lambda i: (layer, 1, 0, 0), **once)
    return pl.pallas_call(
        _outproj_ffn_body,
        grid=(n // tm,),
        in_specs=[tok(d), tok(D_HG), tok(D_SSD)] + tiles + tiles + [
            pl.BlockSpec((None, 1, D_S5), lambda i: (layer, 0, 0)),
            pl.BlockSpec((None, D_S5, D_S5), lambda i: (layer, 0, 0), **once),
            pl.BlockSpec((None, 1, D_S5), lambda i: (layer, 0, 0)),
            pl.BlockSpec((None, D_MIX, d), lambda i: (layer, 0, 0), **once),
            norm(3), norm(4), weight(d, D_FF), weight(d, D_FF), weight(D_FF, d), norm(5),
        ],
        out_specs=tok(d),
        out_shape=jax.ShapeDtypeStruct((n, d), F32),
        compiler_params=_params(("arbitrary",)),
        name="outproj_ffn",
    )(h, oa, ob, *([y4] * S5_LT), *([u4] * S5_LT), s5_d, glu_w, glu_b, w_out, ng, ng, wg, wu, wd, ng)


def _tile(n, pref):
    return pref if n % pref == 0 else n


def kernel(x, norm_g, ffn_w_gate, ffn_w_up, ffn_w_down, w_in, w_out, hg_lb_logits, hg_gnorm, ssd_conv_w, ssd_conv_b,
           ssd_dt_bias, ssd_A_log, ssd_D, ssd_norm, s5_A_re, s5_A_im, s5_B_re, s5_B_im, s5_C_re, s5_C_im, s5_D,
           s5_log_dt, s5_glu_w, s5_glu_b):
    bsz, seq, d = x.shape
    depth = norm_g.shape[0]
    n = bsz * seq
    assert d == D_MODEL and seq % (CHUNK * S5_T) == 0 and w_in.shape[-1] == COL_TAIL + SSD_HEADS + D_S5

    ng = norm_g.astype(F32).reshape(depth * 6, 1, d)
    wg, wu, wd = ffn_w_gate.astype(BF16), ffn_w_up.astype(BF16), ffn_w_down.astype(BF16)
    w_in_b = w_in.astype(BF16)
    w_tail = jnp.concatenate([
        w_in_b[..., COL_TAIL + SSD_HEADS:], w_in_b[..., COL_TAIL:COL_TAIL + SSD_HEADS],
        jnp.zeros(w_in.shape[:-1] + (LANE - SSD_HEADS,), BF16)], axis=-1)
    w_out_b = w_out.astype(BF16)
    glu_w_b = s5_glu_w.astype(BF16)
    lb = jnp.cumsum(jax.nn.softmax(hg_lb_logits.astype(F32), axis=0), axis=0)
    lb = (lb - lb[:1])[:, None, :]
    loglb, l1m, omlb = jnp.log(lb), jnp.log1p(-lb), 1.0 - lb
    gn = hg_gnorm.astype(F32)[:, None, :]
    a_rep = jnp.repeat(-jnp.exp(ssd_A_log.astype(F32)) * LOG2E, SSD_HEADDIM, axis=-1)[:, None, :]
    d_rep = jnp.repeat(ssd_D.astype(F32), SSD_HEADDIM, axis=-1)[:, None, :]
    dtb = jnp.pad(ssd_dt_bias.astype(F32), ((0, 0), (0, LANE - SSD_HEADS)))[:, None, :]
    cw, cb, nw = ssd_conv_w.astype(F32), ssd_conv_b.astype(F32)[:, None, :], ssd_norm.astype(F32)[:, None, :]
    s5_c = jax.vmap(_s5_consts)(s5_A_re, s5_A_im, s5_B_re, s5_B_im, s5_C_re, s5_C_im, s5_log_dt)
    s5_d, glu_b = s5_D.astype(F32)[:, None, :], s5_glu_b.astype(F32)[:, None, :]
    hg_c = _hgrn_consts()
    ssd_c = _ssd_consts()

    tm = _tile(n, 512)
    tl_h = _tile(seq, 512)
    tl_s = _tile(seq, 512)
    s5_rows = _tile(seq // S5_T, 512)

    h = x.astype(F32).reshape(n, d)
    for l in range(depth):
        h = _ffn(h, ng, wg, wu, wd, l, tm)
        qfig, xbc, z, u4, dt = _inproj(h, ng, w_in_b, w_tail, l, tm)
        tok3 = lambda a: a.reshape(bsz, seq, a.shape[-1])
        oa = _hgrn(tok3(qfig), hg_c, loglb[l], l1m[l], omlb[l], gn[l], tl_h)
        ob = _ssd(tok3(xbc), tok3(z), tok3(dt), ssd_c, cw[l], cb[l], dtb[l], a_rep[l], d_rep[l], nw[l], tl_s)
        y4 = _s5(u4, s5_c, l, bsz, s5_rows)
        h = _outproj_ffn(h, oa.reshape(n, D_HG), ob.reshape(n, D_SSD), y4, u4, s5_d, glu_w_b, glu_b, w_out_b, ng,
                         wg, wu, wd, l, tm)
    return h.reshape(bsz, seq, d).astype(x.dtype)
```

```python
import functools

import numpy as np
import jax
import jax.numpy as jnp
from jax import lax
from jax.experimental import pallas as pl
from jax.experimental.pallas import tpu as pltpu

F32 = jnp.float32
BF16 = jnp.bfloat16
EPS = 1e-6
LOG2E = 1.4426950408889634

D_MODEL = 1024
D_FF = 2816
HG_HEADS, HG_DK = 4, 128
D_HG = HG_HEADS * HG_DK
SSD_HEADS, SSD_HEADDIM, SSD_GROUPS, SSD_STATE, SSD_CONV = 8, 64, 2, 128, 4
D_SSD = SSD_HEADS * SSD_HEADDIM
D_XBC = D_SSD + 2 * SSD_GROUPS * SSD_STATE
S5_GROUPS, S5_GROUP_CH, S5_STATE = 32, 16, 64
D_S5 = S5_GROUPS * S5_GROUP_CH
D_MIX = D_HG + D_SSD + D_S5
CHUNK = 64
LANE = 128
SUBLANE = 8
S5_T = 8
S5_LT = D_S5 // LANE
S5_GPT = LANE // S5_GROUP_CH
S5_SW = S5_GPT * S5_STATE

COL_Z, COL_XBC, COL_TAIL = 4 * D_HG, 4 * D_HG + D_SSD, 4 * D_HG + D_SSD + D_XBC
D_TAIL = D_S5 + LANE
VMEM_LIMIT = 56 * 1024 * 1024


def _dot(a, b):
    return jnp.dot(a, b, preferred_element_type=F32)


def _dot_nt(a, b):
    return lax.dot_general(a, b, (((1,), (1,)), ((), ())), preferred_element_type=F32)


def _dot_tn(a, b):
    return lax.dot_general(a, b, (((0,), (0,)), ((), ())), preferred_element_type=F32)


def _split(x):
    hi = x.astype(BF16)
    return hi, (x - hi.astype(F32)).astype(BF16)


def _dot_split(m2, x):
    hi, lo = _split(x)
    return _dot(m2, jnp.concatenate([hi, lo], axis=0))


def _dot_split_r(x, m2):
    hi, lo = _split(x)
    return _dot(jnp.concatenate([hi, lo], axis=1), m2)


def _rms(x, g):
    return x * lax.rsqrt(jnp.mean(x * x, axis=-1, keepdims=True) + EPS) * g


def _silu(x):
    return x * jax.nn.sigmoid(x)


def _params(sem):
    return pltpu.CompilerParams(dimension_semantics=sem, vmem_limit_bytes=VMEM_LIMIT)


def _full(a):
    return pl.BlockSpec(a.shape, lambda *_: (0,) * a.ndim)


def _ffn_apply(x, gpre, wg_ref, wu_ref, wd_ref, gpost):
    xn = _rms(x, gpre).astype(BF16)
    act = (_silu(_dot(xn, wg_ref[...])) * _dot(xn, wu_ref[...])).astype(BF16)
    return x + 0.5 * _rms(_dot(act, wd_ref[...]), gpost)


def _ffn_body(x_ref, gpre_ref, wg_ref, wu_ref, wd_ref, gpost_ref, o_ref):
    o_ref[...] = _ffn_apply(x_ref[...], gpre_ref[...], wg_ref, wu_ref, wd_ref, gpost_ref[...])


def _ffn(h, ng, wg, wu, wd, layer, tm):
    n, d = h.shape
    weight = lambda r, c: pl.BlockSpec((None, None, r, c), lambda i: (layer, 0, 0, 0),
                                       pipeline_mode=pl.Buffered(1))
    return pl.pallas_call(
        _ffn_body,
        grid=(n // tm,),
        in_specs=[
            pl.BlockSpec((tm, d), lambda i: (i, 0)),
            pl.BlockSpec((None, 1, d), lambda i: (layer * 6, 0, 0)),
            weight(d, D_FF), weight(d, D_FF), weight(D_FF, d),
            pl.BlockSpec((None, 1, d), lambda i: (layer * 6 + 1, 0, 0)),
        ],
        out_specs=pl.BlockSpec((tm, d), lambda i: (i, 0)),
        out_shape=jax.ShapeDtypeStruct((n, d), F32),
        compiler_params=_params(("arbitrary",)),
        name="ffn",
    )(h, ng, wg, wu, wd, ng)


def _inproj_body(x_ref, g_ref, w_ref, wt_ref, qfig_ref, xbc_ref, z_ref, u_ref, dt_ref):
    xn = _rms(x_ref[...], g_ref[...]).astype(BF16)
    qfig_ref[...] = _dot(xn, w_ref[:, 0:COL_Z])
    z_ref[...] = _dot(xn, w_ref[:, COL_Z:COL_XBC])
    xbc_ref[...] = _dot(xn, w_ref[:, COL_XBC:COL_TAIL])
    u = _dot(xn, wt_ref[:, 0:D_S5])
    for k in range(S5_LT):
        u_ref[k] = u[:, k * LANE:(k + 1) * LANE]
    dt_ref[...] = _dot(xn, wt_ref[:, D_S5:D_TAIL])


def _inproj(h, ng, w_in_b, w_tail, layer, tm):
    n, d = h.shape
    tok = lambda w: pl.BlockSpec((tm, w), lambda i: (i, 0))
    out_w = (4 * D_HG, D_XBC, D_SSD)
    return pl.pallas_call(
        _inproj_body,
        grid=(n // tm,),
        in_specs=[
            tok(d),
            pl.BlockSpec((None, 1, d), lambda i: (layer * 6 + 2, 0, 0)),
            pl.BlockSpec((None, d, w_in_b.shape[-1]), lambda i: (layer, 0, 0)),
            pl.BlockSpec((None, d, D_TAIL), lambda i: (layer, 0, 0)),
        ],
        out_specs=[tok(w) for w in out_w] + [pl.BlockSpec((S5_LT, tm, LANE), lambda i: (0, i, 0)), tok(LANE)],
        out_shape=[jax.ShapeDtypeStruct((n, w), F32) for w in out_w]
        + [jax.ShapeDtypeStruct((S5_LT, n, LANE), F32), jax.ShapeDtypeStruct((n, LANE), F32)],
        compiler_params=_params(("arbitrary",)),
        name="inproj",
    )(h, ng, w_in_b, w_tail)


HG_LEVELS = (32, 16, 8, 4, 2)


def _block_tri_up(m, n=CHUNK):
    t = np.arange(n)
    same = (t[:, None] // m) == (t[None, :] // m)
    tri = same & (t[None, :] <= t[:, None])
    up = same & (t[None, :] > t[:, None])
    return tri.astype(np.float32), up.astype(np.float32)


def _hgrn_consts():
    mats, masks = [], []
    for m in (CHUNK,) + HG_LEVELS:
        mats += list(_block_tri_up(m))
    t = np.arange(CHUNK)
    for m in HG_LEVELS + (1,):
        bt, bs = t[:, None] // m, t[None, :] // m
        masks.append(((bt % 2 == 1) & (bs == bt - 1)).astype(np.float32))
    masks.append(np.eye(CHUNK, dtype=np.float32))
    cm = np.concatenate(mats, 0)
    return jnp.asarray(np.concatenate([cm, cm], 1), BF16), jnp.asarray(np.stack(masks, 0), F32)


def _hgrn_body(p_ref, cm_ref, mask_ref, loglb_ref, l1m_ref, omlb_ref, gn_ref, o_ref, st_ref, *, n_chunks):
    @pl.when(pl.program_id(0) == 0)
    def _():
        st_ref[...] = jnp.zeros_like(st_ref)

    cm = cm_ref[...]
    loglb, l1m, omlb, gn = loglb_ref[...], l1m_ref[...], omlb_ref[...], gn_ref[...]
    nl = len(HG_LEVELS)

    def chunk_one(b, rows):
        q = _silu(p_ref[b, rows, 0:D_HG])
        fr = p_ref[b, rows, D_HG:2 * D_HG]
        v = p_ref[b, rows, 2 * D_HG:3 * D_HG].astype(BF16)
        gr = p_ref[b, rows, 3 * D_HG:4 * D_HG]
        ls = jnp.minimum(fr, 0.0) - jnp.log1p(jnp.exp(-jnp.abs(fr)))
        bb = l1m + ls
        lf = jnp.maximum(loglb, bb) + jnp.log1p(jnp.exp(-jnp.abs(loglb - bb)))
        kk = omlb * jnp.exp(ls - fr)
        lf2 = lf * LOG2E
        cs = _dot_split(cm, lf2)
        kb = kk.astype(BF16)
        q01 = jnp.concatenate([q, q * jnp.exp2(lf2)], axis=0).astype(BF16)
        outs = []
        for h in range(HG_HEADS):
            hs = slice(h * HG_DK, (h + 1) * HG_DK)
            qh, kh = q[:, hs], kk[:, hs]
            st = st_ref[b, h]
            c64, r64 = cs[0:CHUNK, hs], cs[CHUNK:2 * CHUNK, hs]
            p01 = _dot_nt(q01[:, hs], kb[:, hs])
            scores = mask_ref[nl + 1] * p01[0:CHUNK] + mask_ref[nl] * p01[CHUNK:2 * CHUNK]
            for li in range(nl):
                base = 2 * CHUNK * (li + 1)
                cmm, rmm = cs[base:base + CHUNK, hs], cs[base + CHUNK:base + 2 * CHUNK, hs]
                p = _dot_nt((qh * jnp.exp2(cmm)).astype(BF16), (kh * jnp.exp2(rmm)).astype(BF16))
                scores = scores + mask_ref[li] * p
            o = _dot_nt((qh * jnp.exp2(c64)).astype(BF16), st.astype(BF16)) + _dot(scores.astype(BF16), v[:, hs])
            ke = (kh * jnp.exp2(r64)).astype(BF16)
            st_ref[b, h] = jnp.exp2(c64[CHUNK - 1:CHUNK, :]) * st + _dot_tn(v[:, hs], ke)
            outs.append(o * lax.rsqrt(jnp.mean(o * o, axis=-1, keepdims=True) + EPS) * gn)
        o_ref[b, rows, :] = (jnp.concatenate(outs, axis=-1) * _silu(gr)).astype(o_ref.dtype)

    def chunk_step(c, carry):
        rows = pl.ds(pl.multiple_of(c * CHUNK, CHUNK), CHUNK)
        for b in range(p_ref.shape[0]):
            chunk_one(b, rows)
        return carry

    lax.fori_loop(0, n_chunks, chunk_step, 0, unroll=2)


def _hgrn(qfig3, consts, loglb, l1m, omlb, gn, tl):
    b, l, _ = qfig3.shape
    cm, masks = consts
    return pl.pallas_call(
        functools.partial(_hgrn_body, n_chunks=tl // CHUNK),
        grid=(l // tl,),
        in_specs=[pl.BlockSpec((b, tl, 4 * D_HG), lambda i: (0, i, 0)),
                  _full(cm), _full(masks), _full(loglb), _full(l1m), _full(omlb), _full(gn)],
        out_specs=pl.BlockSpec((b, tl, D_HG), lambda i: (0, i, 0)),
        out_shape=jax.ShapeDtypeStruct((b, l, D_HG), BF16),
        scratch_shapes=[pltpu.VMEM((b, HG_HEADS, HG_DK, HG_DK), F32)],
        compiler_params=_params(("arbitrary",)),
        name="hgrn2",
    )(qfig3, cm, masks, loglb, l1m, omlb, gn)


def _ssd_consts():
    tri, up = _block_tri_up(CHUNK)
    t = np.arange(CHUNK)
    su = (t[:, None] > t[None, :]).astype(np.float32)
    causal = (t[:, None] >= t[None, :]).astype(np.float32)
    expand = np.zeros((LANE, D_SSD), np.float32)
    for h in range(SSD_HEADS):
        expand[h, h * SSD_HEADDIM:(h + 1) * SSD_HEADDIM] = 1.0
    tu = np.concatenate([tri, up], 0)
    return (jnp.asarray(np.concatenate([tu, tu], 1), BF16), jnp.asarray(np.tile(su, (1, SSD_HEADS)), F32),
            jnp.asarray(causal, F32), jnp.asarray(np.concatenate([expand, expand], 0), BF16))


def _ssd_body(xbc_ref, z_ref, dt_ref, cw_ref, cb_ref, dtb_ref, a_ref, d_ref, nw_ref, tu_ref, su_ref, causal_ref,
              ex_ref, o_ref, st_ref, ext_ref, *, tl):
    pad = SUBLANE
    nb = xbc_ref.shape[0]

    @pl.when(pl.program_id(0) == 0)
    def _():
        st_ref[...] = jnp.zeros_like(st_ref)
        ext_ref[:, 0:pad, :] = jnp.zeros((nb, pad, D_XBC), F32)

    ext_ref[:, pad:pad + tl, :] = xbc_ref[...]

    tu, su, causal, expand = tu_ref[...], su_ref[...], causal_ref[...], ex_ref[...]
    a_rep, d_rep, nw, cb, dtb = a_ref[...], d_ref[...], nw_ref[...], cb_ref[...], dtb_ref[...]
    gw = D_SSD // SSD_GROUPS
    hpg = SSD_HEADS // SSD_GROUPS

    def chunk_one(b, r0):
        rows = pl.ds(r0, CHUNK)
        win = ext_ref[b, pl.ds(r0, CHUNK + pad), :]
        conv = cb + cw_ref[SSD_CONV - 1:SSD_CONV, :] * win[pad:pad + CHUNK, :]
        for j in range(SSD_CONV - 1):
            conv = conv + cw_ref[j:j + 1, :] * pltpu.roll(win, SSD_CONV - 1 - j, 0)[pad:pad + CHUNK, :]
        xs = _silu(conv)
        x = xs[:, 0:D_SSD]
        dtin = dt_ref[b, rows, :] + dtb
        sp = jnp.maximum(dtin, 0.0) + jnp.log1p(jnp.exp(-jnp.abs(dtin)))
        dtr = _dot_split_r(sp, expand)
        adt = dtr * a_rep
        r = _dot_split(tu, jnp.concatenate([adt, adt * su], axis=-1))
        acs, rs = r[0:CHUNK, 0:D_SSD], r[CHUNK:2 * CHUNK, 0:D_SSD]
        dm = r[0:CHUNK, D_SSD:2 * D_SSD]
        xdt = x * dtr
        xdt_b = xdt.astype(BF16)
        ys = []
        for g in range(SSD_GROUPS):
            gs = slice(g * gw, (g + 1) * gw)
            bg = xs[:, D_SSD + g * SSD_STATE:D_SSD + (g + 1) * SSD_STATE].astype(BF16)
            cg = xs[:, D_SSD + (SSD_GROUPS + g) * SSD_STATE:D_SSD + (SSD_GROUPS + g + 1) * SSD_STATE].astype(BF16)
            st = st_ref[b, g]
            scores = _dot_nt(cg, bg) * causal
            y_off = _dot(cg, st.astype(BF16)) * jnp.exp2(acs[:, gs])
            yd = []
            for hh in range(hpg):
                sl = slice((g * hpg + hh) * SSD_HEADDIM, (g * hpg + hh + 1) * SSD_HEADDIM)
                yd.append(_dot((scores * jnp.exp2(dm[:, sl])).astype(BF16), xdt_b[:, sl]))
            ys.append(jnp.concatenate(yd, axis=-1) + y_off)
            xd = (xdt[:, gs] * jnp.exp2(rs[:, gs])).astype(BF16)
            st_ref[b, g] = jnp.exp2(acs[CHUNK - 1:CHUNK, gs]) * st + _dot_tn(bg, xd)
        y = jnp.concatenate(ys, axis=-1) + x * d_rep
        y = y * _silu(z_ref[b, rows, :])
        yn = [y[:, g * gw:(g + 1) * gw] * lax.rsqrt(
            jnp.mean(y[:, g * gw:(g + 1) * gw] * y[:, g * gw:(g + 1) * gw], axis=-1, keepdims=True) + EPS)
            for g in range(SSD_GROUPS)]
        o_ref[b, rows, :] = (jnp.concatenate(yn, axis=-1) * nw).astype(o_ref.dtype)

    def chunk_step(c, carry):
        r0 = pl.multiple_of(c * CHUNK, CHUNK)
        for b in range(nb):
            chunk_one(b, r0)
        return carry

    lax.fori_loop(0, tl // CHUNK, chunk_step, 0, unroll=2)
    ext_ref[:, 0:pad, :] = xbc_ref[:, tl - pad:tl, :]


def _ssd(xbc3, z3, dt3, consts, cw, cb, dtb, a_rep, d_rep, nw, tl):
    b, l, _ = xbc3.shape
    tok = lambda w: pl.BlockSpec((b, tl, w), lambda i: (0, i, 0))
    params = (cw, cb, dtb, a_rep, d_rep, nw) + tuple(consts)
    return pl.pallas_call(
        functools.partial(_ssd_body, tl=tl),
        grid=(l // tl,),
        in_specs=[tok(D_XBC), tok(D_SSD), tok(LANE)] + [_full(a) for a in params],
        out_specs=tok(D_SSD),
        out_shape=jax.ShapeDtypeStruct((b, l, D_SSD), BF16),
        scratch_shapes=[
            pltpu.VMEM((b, SSD_GROUPS, SSD_STATE, D_SSD // SSD_GROUPS), F32),
            pltpu.VMEM((b, tl + SUBLANE, D_XBC), F32),
        ],
        compiler_params=_params(("arbitrary",)),
        name="ssd",
    )(xbc3, z3, dt3, *params)


def _s5_expand(r, rep, row_group, col_group):
    big = jnp.dot(r, rep, preferred_element_type=BF16)
    rows, cols = big.shape[-2:]
    rg = row_group(lax.broadcasted_iota(jnp.int32, (rows, cols), 0))
    cg = col_group(lax.broadcasted_iota(jnp.int32, (rows, cols), 1))
    return jnp.where(rg == cg, big, jnp.zeros_like(big))


def _s5_consts(a_re, a_im, b_re, b_im, c_re, c_im, log_dt):
    t_, gc, gp = S5_T, S5_GROUP_CH, S5_STATE
    ar, ai = a_re.astype(F32), a_im.astype(F32)
    delta = jnp.exp(log_dt.astype(F32))[:, None]
    mag = jnp.exp(ar * delta)
    ab_re, ab_im = mag * jnp.cos(ai * delta), mag * jnp.sin(ai * delta)
    den = ar * ar + ai * ai
    nr, ni = ab_re - 1.0, ab_im
    fr = (nr * ar + ni * ai) / den
    fi = (ni * ar - nr * ai) / den
    br, bi = b_re.astype(F32).transpose(0, 2, 1), b_im.astype(F32).transpose(0, 2, 1)
    bb_re = fr[:, None, :] * br - fi[:, None, :] * bi
    bb_im = fr[:, None, :] * bi + fi[:, None, :] * br
    lam_r, lam_i = ar * delta, ai * delta
    tau = jnp.arange(t_ + 1, dtype=F32)
    pr = jnp.exp(tau[:, None, None] * lam_r) * jnp.cos(tau[:, None, None] * lam_i)
    pi = jnp.exp(tau[:, None, None] * lam_r) * jnp.sin(tau[:, None, None] * lam_i)
    qr = jnp.exp(lam_r[..., None] * tau) * jnp.cos(lam_i[..., None] * tau)
    qi = jnp.exp(lam_r[..., None] * tau) * jnp.sin(lam_i[..., None] * tau)
    abr = pr[:, :, None, :] * bb_re - pi[:, :, None, :] * bb_im
    abi = pr[:, :, None, :] * bb_im + pi[:, :, None, :] * bb_re
    cr, ci = c_re.astype(F32), c_im.astype(F32)
    hp = lax.Precision.HIGHEST
    km = (jnp.einsum('tgcp,gdp->gctd', abr[:t_], cr, precision=hp)
          - jnp.einsum('tgcp,gdp->gctd', abi[:t_], ci, precision=hp))
    tt, jj, ii = np.arange(t_)[:, None, None], np.arange(t_)[None, :, None], np.arange(t_)[None, None, :]
    shift = np.einsum('tji,de->jtdie', (tt == ii - jj).astype(np.float32), np.eye(gc, dtype=np.float32))
    shift = shift.reshape(t_, t_ * gc, t_ * gc)
    km2 = km.astype(BF16).reshape(S5_GROUPS * gc, t_ * gc)
    r_t = jnp.stack([jnp.dot(km2, jnp.asarray(shift[j], BF16), preferred_element_type=BF16) for j in range(t_)], 0)
    r_t = r_t.reshape(t_, S5_LT, LANE, t_ * gc).transpose(1, 0, 2, 3).reshape(S5_LT, t_ * LANE, t_ * gc)
    rev = np.arange(t_ - 1, -1, -1)
    r_w = jnp.concatenate([abr[rev], abi[rev]], axis=-1).astype(BF16)
    r_w = r_w.reshape(t_, S5_LT, S5_GPT, gc, 2 * gp).transpose(1, 0, 2, 3, 4).reshape(S5_LT, t_ * LANE, 2 * gp)
    crt, cit = cr.transpose(0, 2, 1)[:, :, None, :], ci.transpose(0, 2, 1)[:, :, None, :]
    q1r, q1i = qr[:, :, 1:, None], qi[:, :, 1:, None]
    vr = (crt * q1r - cit * q1i).astype(BF16).reshape(S5_LT, S5_SW, t_ * gc)
    vi = (-(crt * q1i + cit * q1r)).astype(BF16).reshape(S5_LT, S5_SW, t_ * gc)
    r_v = jnp.concatenate([vr, vi], axis=1)
    src, dst = np.arange(t_ * gc), np.arange(t_ * LANE)
    rep = jnp.asarray((src[:, None] // gc == dst[None, :] // LANE) & (src[:, None] % gc == dst[None, :] % gc), BF16)
    src, dst = np.arange(2 * gp), np.arange(2 * S5_SW)
    repw = jnp.asarray((src[:, None] // gp == dst[None, :] // S5_SW) & (src[:, None] % gp == dst[None, :] % gp), BF16)
    io_group = lambda x: (x // gc) % S5_GPT
    st_group = lambda x: (x % S5_SW) // gp
    toep = _s5_expand(r_t, rep, io_group, io_group)
    wz = _s5_expand(r_w, repw, io_group, st_group)
    vv = _s5_expand(r_v, rep, st_group, io_group)
    a_t = jnp.stack([pr[t_].reshape(S5_LT, S5_SW), pi[t_].reshape(S5_LT, S5_SW)], axis=1)
    return toep, wz, vv, a_t


def _s5_body(u_ref, toep_ref, wz_ref, vv_ref, at_ref, o_ref, z_ref, xp_ref, st_ref, *, n_rows):
    @pl.when(pl.program_id(2) == 0)
    def _():
        st_ref[...] = jnp.zeros_like(st_ref)

    fold = lambda j: pl.ds(j, n_rows, stride=S5_T)
    ub = jnp.concatenate([u_ref[fold(j), :] for j in range(S5_T)], axis=-1).astype(BF16)
    z_ref[...] = _dot(ub, wz_ref[...])
    ar, ai = at_ref[0:1, :], at_ref[1:2, :]
    re, im = slice(0, S5_SW), slice(S5_SW, 2 * S5_SW)

    def step(c, carry):
        xr, xi = carry
        row = pl.ds(c, 1)
        xp_ref[row, re] = xr
        xp_ref[row, im] = xi
        return (xr * ar - xi * ai + z_ref[row, re], xr * ai + xi * ar + z_ref[row, im])

    xr, xi = lax.fori_loop(0, n_rows, step, (st_ref[0:1, re], st_ref[0:1, im]), unroll=8)
    st_ref[0:1, re] = xr
    st_ref[0:1, im] = xi
    half = (S5_T // 2) * LANE
    y = jnp.concatenate([_dot(ub[:, 0:half], toep_ref[0:half, 0:half]), _dot(ub, toep_ref[:, half:])], axis=-1)
    y = y + _dot(xp_ref[...].astype(BF16), vv_ref[...])
    for i in range(S5_T):
        o_ref[fold(i), :] = y[:, i * LANE:(i + 1) * LANE]


def _s5(u4, consts, layer, batch, n_rows):
    toep, wz, vv, a_t = consts
    _, n, w = u4.shape
    per_batch = n // batch // (n_rows * S5_T)
    blk = pl.BlockSpec((None, n_rows * S5_T, w), lambda k, bi, i: (k, bi * per_batch + i, 0))
    wspec = lambda a: pl.BlockSpec((None, None) + a.shape[2:], lambda k, bi, i: (layer, k, 0, 0))
    return pl.pallas_call(
        functools.partial(_s5_body, n_rows=n_rows),
        grid=(S5_LT, batch, per_batch),
        in_specs=[blk, wspec(toep), wspec(wz), wspec(vv), wspec(a_t)],
        out_specs=blk,
        out_shape=jax.ShapeDtypeStruct(u4.shape, F32),
        scratch_shapes=[pltpu.VMEM((n_rows, 2 * S5_SW), F32), pltpu.VMEM((n_rows, 2 * S5_SW), F32),
                        pltpu.VMEM((SUBLANE, 2 * S5_SW), F32)],
        compiler_params=_params(("arbitrary", "arbitrary", "arbitrary")),
        name="s5",
    )(u4, toep, wz, vv, a_t)


def _outproj_ffn_body(h_ref, oa_ref, ob_ref, *rest):
    y_refs, u_refs = rest[0:S5_LT], rest[S5_LT:2 * S5_LT]
    d_ref, gw_ref, gb_ref, wo_ref, g_ref, gpre_ref, wg_ref, wu_ref, wd_ref, gpost_ref, o_ref = rest[2 * S5_LT:]
    yc = jnp.concatenate([r[...] for r in y_refs], axis=-1)
    u = jnp.concatenate([r[...] for r in u_refs], axis=-1)
    y = jax.nn.gelu(yc + d_ref[...] * u)
    oc = y * jax.nn.sigmoid(_dot(y.astype(BF16), gw_ref[...]) + gb_ref[...])
    mix = (_dot(oa_ref[...], wo_ref[0:D_HG, :])
           + _dot(ob_ref[...], wo_ref[D_HG:D_HG + D_SSD, :])
           + _dot(oc.astype(BF16), wo_ref[D_HG + D_SSD:D_MIX, :]))
    x = h_ref[...] + _rms(mix, g_ref[...])
    o_ref[...] = _ffn_apply(x, gpre_ref[...], wg_ref, wu_ref, wd_ref, gpost_ref[...])


def _outproj_ffn(h, oa, ob, y4, u4, s5_d, glu_w, glu_b, w_out, ng, wg, wu, wd, layer, tm):
    n, d = h.shape
    tok = lambda w: pl.BlockSpec((tm, w), lambda i: (i, 0))
    tile = lambda k: pl.BlockSpec((None, tm, LANE), lambda i: (k, i, 0))
    tiles = [tile(k) for k in range(S5_LT)]
    once = dict(pipeline_mode=pl.Buffered(1))
    norm = lambda k: pl.BlockSpec((None, 1, d), lambda i: (layer * 6 + k, 0, 0))
    weight = lambda r, c: pl.BlockSpec((None, None, r, c), lambda i: (layer, 1, 0, 0), **once)
    return pl.pallas_call(
        _outproj_ffn_body,
        grid=(n // tm,),
        in_specs=[tok(d), tok(D_HG), tok(D_SSD)] + tiles + tiles + [
            pl.BlockSpec((None, 1, D_S5), lambda i: (layer, 0, 0)),
            pl.BlockSpec((None, D_S5, D_S5), lambda i: (layer, 0, 0), **once),
            pl.BlockSpec((None, 1, D_S5), lambda i: (layer, 0, 0)),
            pl.BlockSpec((None, D_MIX, d), lambda i: (layer, 0, 0), **once),
            norm(3), norm(4), weight(d, D_FF), weight(d, D_FF), weight(D_FF, d), norm(5),
        ],
        out_specs=tok(d),
        out_shape=jax.ShapeDtypeStruct((n, d), F32),
        compiler_params=_params(("arbitrary",)),
        name="outproj_ffn",
    )(h, oa, ob, *([y4] * S5_LT), *([u4] * S5_LT), s5_d, glu_w, glu_b, w_out, ng, ng, wg, wu, wd, ng)


def _tile(n, pref):
    return pref if n % pref == 0 else n


def kernel(x, norm_g, ffn_w_gate, ffn_w_up, ffn_w_down, w_in, w_out, hg_lb_logits, hg_gnorm, ssd_conv_w, ssd_conv_b,
           ssd_dt_bias, ssd_A_log, ssd_D, ssd_norm, s5_A_re, s5_A_im, s5_B_re, s5_B_im, s5_C_re, s5_C_im, s5_D,
           s5_log_dt, s5_glu_w, s5_glu_b):
    bsz, seq, d = x.shape
    depth = norm_g.shape[0]
    n = bsz * seq
    assert d == D_MODEL and seq % (CHUNK * S5_T) == 0 and w_in.shape[-1] == COL_TAIL + SSD_HEADS + D_S5

    ng = norm_g.astype(F32).reshape(depth * 6, 1, d)
    wg, wu, wd = ffn_w_gate.astype(BF16), ffn_w_up.astype(BF16), ffn_w_down.astype(BF16)
    w_in_b = w_in.astype(BF16)
    w_tail = jnp.concatenate([
        w_in_b[..., COL_TAIL + SSD_HEADS:], w_in_b[..., COL_TAIL:COL_TAIL + SSD_HEADS],
        jnp.zeros(w_in.shape[:-1] + (LANE - SSD_HEADS,), BF16)], axis=-1)
    w_out_b = w_out.astype(BF16)
    glu_w_b = s5_glu_w.astype(BF16)
    lb = jnp.cumsum(jax.nn.softmax(hg_lb_logits.astype(F32), axis=0), axis=0)
    lb = (lb - lb[:1])[:, None, :]
    loglb, l1m, omlb = jnp.log(lb), jnp.log1p(-lb), 1.0 - lb
    gn = hg_gnorm.astype(F32)[:, None, :]
    a_rep = jnp.repeat(-jnp.exp(ssd_A_log.astype(F32)) * LOG2E, SSD_HEADDIM, axis=-1)[:, None, :]
    d_rep = jnp.repeat(ssd_D.astype(F32), SSD_HEADDIM, axis=-1)[:, None, :]
    dtb = jnp.pad(ssd_dt_bias.astype(F32), ((0, 0), (0, LANE - SSD_HEADS)))[:, None, :]
    cw, cb, nw = ssd_conv_w.astype(F32), ssd_conv_b.astype(F32)[:, None, :], ssd_norm.astype(F32)[:, None, :]
    s5_c = jax.vmap(_s5_consts)(s5_A_re, s5_A_im, s5_B_re, s5_B_im, s5_C_re, s5_C_im, s5_log_dt)
    s5_d, glu_b = s5_D.astype(F32)[:, None, :], s5_glu_b.astype(F32)[:, None, :]
    hg_c = _hgrn_consts()
    ssd_c = _ssd_consts()

    tm = _tile(n, 512)
    tl_h = _tile(seq, 512)
    tl_s = _tile(seq, 512)
    s5_rows = _tile(seq // S5_T, 512)

    h = x.astype(F32).reshape(n, d)
    for l in range(depth):
        h = _ffn(h, ng, wg, wu, wd, l, tm)
        qfig, xbc, z, u4, dt = _inproj(h, ng, w_in_b, w_tail, l, tm)
        tok3 = lambda a: a.reshape(bsz, seq, a.shape[-1])
        oa = _hgrn(tok3(qfig), hg_c, loglb[l], l1m[l], omlb[l], gn[l], tl_h)
        ob = _ssd(tok3(xbc), tok3(z), tok3(dt), ssd_c, cw[l], cb[l], dtb[l], a_rep[l], d_rep[l], nw[l], tl_s)
        y4 = _s5(u4, s5_c, l, bsz, s5_rows)
        h = _outproj_ffn(h, oa.reshape(n, D_HG), ob.reshape(n, D_SSD), y4, u4, s5_d, glu_w_b, glu_b, w_out_b, ng,
                         wg, wu, wd, l, tm)
    return h.reshape(bsz, seq, d).astype(x.dtype)
```

```python
import functools

import numpy as np
import jax
import jax.numpy as jnp
from jax import lax
from jax.experimental import pallas as pl
from jax.experimental.pallas import tpu as pltpu

F32 = jnp.float32
BF16 = jnp.bfloat16
EPS = 1e-6
LOG2E = 1.4426950408889634

D_MODEL = 1024
D_FF = 2816
HG_HEADS, HG_DK = 4, 128
D_HG = HG_HEADS * HG_DK
SSD_HEADS, SSD_HEADDIM, SSD_GROUPS, SSD_STATE, SSD_CONV = 8, 64, 2, 128, 4
D_SSD = SSD_HEADS * SSD_HEADDIM
D_XBC = D_SSD + 2 * SSD_GROUPS * SSD_STATE
S5_GROUPS, S5_GROUP_CH, S5_STATE = 32, 16, 64
D_S5 = S5_GROUPS * S5_GROUP_CH
D_MIX = D_HG + D_SSD + D_S5
CHUNK = 64
LANE = 128
SUBLANE = 8
S5_T = 8
S5_LT = D_S5 // LANE
S5_GPT = LANE // S5_GROUP_CH
S5_SW = S5_GPT * S5_STATE

COL_Z, COL_XBC, COL_TAIL = 4 * D_HG, 4 * D_HG + D_SSD, 4 * D_HG + D_SSD + D_XBC
D_TAIL = D_S5 + LANE
VMEM_LIMIT = 56 * 1024 * 1024


def _dot(a, b):
    return jnp.dot(a, b, preferred_element_type=F32)


def _dot_nt(a, b):
    return lax.dot_general(a, b, (((1,), (1,)), ((), ())), preferred_element_type=F32)


def _dot_tn(a, b):
    return lax.dot_general(a, b, (((0,), (0,)), ((), ())), preferred_element_type=F32)


def _split(x):
    hi = x.astype(BF16)
    return hi, (x - hi.astype(F32)).astype(BF16)


def _dot_split(m2, x):
    hi, lo = _split(x)
    return _dot(m2, jnp.concatenate([hi, lo], axis=0))


def _dot_split_r(x, m2):
    hi, lo = _split(x)
    return _dot(jnp.concatenate([hi, lo], axis=1), m2)


def _rms(x, g):
    return x * lax.rsqrt(jnp.mean(x * x, axis=-1, keepdims=True) + EPS) * g


def _silu(x):
    return x * jax.nn.sigmoid(x)


def _params(sem):
    return pltpu.CompilerParams(dimension_semantics=sem, vmem_limit_bytes=VMEM_LIMIT)


def _full(a):
    return pl.BlockSpec(a.shape, lambda *_: (0,) * a.ndim)


def _ffn_apply(x, gpre, wg_ref, wu_ref, wd_ref, gpost):
    xn = _rms(x, gpre).astype(BF16)
    act = (_silu(_dot(xn, wg_ref[...])) * _dot(xn, wu_ref[...])).astype(BF16)
    return x + 0.5 * _rms(_dot(act, wd_ref[...]), gpost)


def _ffn_body(x_ref, gpre_ref, wg_ref, wu_ref, wd_ref, gpost_ref, o_ref):
    o_ref[...] = _ffn_apply(x_ref[...], gpre_ref[...], wg_ref, wu_ref, wd_ref, gpost_ref[...])


def _ffn(h, ng, wg, wu, wd, layer, tm):
    n, d = h.shape
    weight = lambda r, c: pl.BlockSpec((None, None, r, c), lambda i: (layer, 0, 0, 0),
                                       pipeline_mode=pl.Buffered(1))
    return pl.pallas_call(
        _ffn_body,
        grid=(n // tm,),
        in_specs=[
            pl.BlockSpec((tm, d), lambda i: (i, 0)),
            pl.BlockSpec((None, 1, d), lambda i: (layer * 6, 0, 0)),
            weight(d, D_FF), weight(d, D_FF), weight(D_FF, d),
            pl.BlockSpec((None, 1, d), lambda i: (layer * 6 + 1, 0, 0)),
        ],
        out_specs=pl.BlockSpec((tm, d), lambda i: (i, 0)),
        out_shape=jax.ShapeDtypeStruct((n, d), F32),
        compiler_params=_params(("arbitrary",)),
        name="ffn",
    )(h, ng, wg, wu, wd, ng)


def _inproj_body(x_ref, g_ref, w_ref, wt_ref, qfig_ref, xbc_ref, z_ref, u_ref, dt_ref):
    xn = _rms(x_ref[...], g_ref[...]).astype(BF16)
    qfig_ref[...] = _dot(xn, w_ref[:, 0:COL_Z])
    z_ref[...] = _dot(xn, w_ref[:, COL_Z:COL_XBC])
    xbc_ref[...] = _dot(xn, w_ref[:, COL_XBC:COL_TAIL])
    u = _dot(xn, wt_ref[:, 0:D_S5])
    for k in range(S5_LT):
        u_ref[k] = u[:, k * LANE:(k + 1) * LANE]
    dt_ref[...] = _dot(xn, wt_ref[:, D_S5:D_TAIL])


def _inproj(h, ng, w_in_b, w_tail, layer, tm):
    n, d = h.shape
    tok = lambda w: pl.BlockSpec((tm, w), lambda i: (i, 0))
    out_w = (4 * D_HG, D_XBC, D_SSD)
    return pl.pallas_call(
        _inproj_body,
        grid=(n // tm,),
        in_specs=[
            tok(d),
            pl.BlockSpec((None, 1, d), lambda i: (layer * 6 + 2, 0, 0)),
            pl.BlockSpec((None, d, w_in_b.shape[-1]), lambda i: (layer, 0, 0)),
            pl.BlockSpec((None, d, D_TAIL), lambda i: (layer, 0, 0)),
        ],
        out_specs=[tok(w) for w in out_w] + [pl.BlockSpec((S5_LT, tm, LANE), lambda i: (0, i, 0)), tok(LANE)],
        out_shape=[jax.ShapeDtypeStruct((n, w), F32) for w in out_w]
        + [jax.ShapeDtypeStruct((S5_LT, n, LANE), F32), jax.ShapeDtypeStruct((n, LANE), F32)],
        compiler_params=_params(("arbitrary",)),
        name="inproj",
    )(h, ng, w_in_b, w_tail)


HG_LEVELS = (32, 16, 8, 4, 2)


def _block_tri_up(m, n=CHUNK):
    t = np.arange(n)
    same = (t[:, None] // m) == (t[None, :] // m)
    tri = same & (t[None, :] <= t[:, None])
    up = same & (t[None, :] > t[:, None])
    return tri.astype(np.float32), up.astype(np.float32)


def _hgrn_consts():
    mats, masks = [], []
    for m in (CHUNK,) + HG_LEVELS:
        mats += list(_block_tri_up(m))
    t = np.arange(CHUNK)
    for m in HG_LEVELS + (1,):
        bt, bs = t[:, None] // m, t[None, :] // m
        masks.append(((bt % 2 == 1) & (bs == bt - 1)).astype(np.float32))
    masks.append(np.eye(CHUNK, dtype=np.float32))
    cm = np.concatenate(mats, 0)
    return jnp.asarray(np.concatenate([cm, cm], 1), BF16), jnp.asarray(np.stack(masks, 0), F32)


def _hgrn_chunk(p_ref, cm_ref, mask_ref, loglb_ref, l1m_ref, omlb_ref, gn_ref, o_ref, st_ref):
    @pl.when(pl.program_id(0) == 0)
    def _():
        st_ref[...] = jnp.zeros_like(st_ref)

    cm = cm_ref[...]
    loglb, l1m, omlb, gn = loglb_ref[...], l1m_ref[...], omlb_ref[...], gn_ref[...]
    nl = len(HG_LEVELS)

    def chunk_one(b, rows):
        q = _silu(p_ref[b, rows, 0:D_HG])
        fr = p_ref[b, rows, D_HG:2 * D_HG]
        v = p_ref[b, rows, 2 * D_HG:3 * D_HG].astype(BF16)
        gr = p_ref[b, rows, 3 * D_HG:4 * D_HG]
        ls = jnp.minimum(fr, 0.0) - jnp.log1p(jnp.exp(-jnp.abs(fr)))
        bb = l1m + ls
        lf = jnp.maximum(loglb, bb) + jnp.log1p(jnp.exp(-jnp.abs(loglb - bb)))
        kk = omlb * jnp.exp(ls - fr)
        lf2 = lf * LOG2E
        cs = _dot_split(cm, lf2)
        kb = kk.astype(BF16)
        q01 = jnp.concatenate([q, q * jnp.exp2(lf2)], axis=0).astype(BF16)
        outs = []
        for h in range(HG_HEADS):
            hs = slice(h * HG_DK, (h + 1) * HG_DK)
            qh, kh = q[:, hs], kk[:, hs]
            st = st_ref[b, h]
            c64, r64 = cs[0:CHUNK, hs], cs[CHUNK:2 * CHUNK, hs]
            p01 = _dot_nt(q01[:, hs], kb[:, hs])
            scores = mask_ref[nl + 1] * p01[0:CHUNK] + mask_ref[nl] * p01[CHUNK:2 * CHUNK]
            for li in range(nl):
                base = 2 * CHUNK * (li + 1)
                cmm, rmm = cs[base:base + CHUNK, hs], cs[base + CHUNK:base + 2 * CHUNK, hs]
                p = _dot_nt((qh * jnp.exp2(cmm)).astype(BF16), (kh * jnp.exp2(rmm)).astype(BF16))
                scores = scores + mask_ref[li] * p
            o = _dot_nt((qh * jnp.exp2(c64)).astype(BF16), st.astype(BF16)) + _dot(scores.astype(BF16), v[:, hs])
            ke = (kh * jnp.exp2(r64)).astype(BF16)
            st_ref[b, h] = jnp.exp2(c64[CHUNK - 1:CHUNK, :]) * st + _dot_tn(v[:, hs], ke)
            outs.append(o * lax.rsqrt(jnp.mean(o * o, axis=-1, keepdims=True) + EPS) * gn)
        o_ref[b, rows, :] = (jnp.concatenate(outs, axis=-1) * _silu(gr)).astype(o_ref.dtype)

    return chunk_one


def _ssd_consts():
    tri, up = _block_tri_up(CHUNK)
    t = np.arange(CHUNK)
    su = (t[:, None] > t[None, :]).astype(np.float32)
    causal = (t[:, None] >= t[None, :]).astype(np.float32)
    expand = np.zeros((LANE, D_SSD), np.float32)
    for h in range(SSD_HEADS):
        expand[h, h * SSD_HEADDIM:(h + 1) * SSD_HEADDIM] = 1.0
    tu = np.concatenate([tri, up], 0)
    return (jnp.asarray(np.concatenate([tu, tu], 1), BF16), jnp.asarray(np.tile(su, (1, SSD_HEADS)), F32),
            jnp.asarray(causal, F32), jnp.asarray(np.concatenate([expand, expand], 0), BF16))


def _ssd_chunk(xbc_ref, z_ref, dt_ref, cw_ref, cb_ref, dtb_ref, a_ref, d_ref, nw_ref, tu_ref, su_ref, causal_ref,
               ex_ref, o_ref, st_ref, ext_ref, *, tl):
    pad = SUBLANE
    nb = xbc_ref.shape[0]

    @pl.when(pl.program_id(0) == 0)
    def _():
        st_ref[...] = jnp.zeros_like(st_ref)
        ext_ref[:, 0:pad, :] = jnp.zeros((nb, pad, D_XBC), F32)

    ext_ref[:, pad:pad + tl, :] = xbc_ref[...]

    tu, su, causal, expand = tu_ref[...], su_ref[...], causal_ref[...], ex_ref[...]
    a_rep, d_rep, nw, cb, dtb = a_ref[...], d_ref[...], nw_ref[...], cb_ref[...], dtb_ref[...]
    gw = D_SSD // SSD_GROUPS
    hpg = SSD_HEADS // SSD_GROUPS

    def chunk_one(b, r0):
        rows = pl.ds(r0, CHUNK)
        win = ext_ref[b, pl.ds(r0, CHUNK + pad), :]
        conv = cb + cw_ref[SSD_CONV - 1:SSD_CONV, :] * win[pad:pad + CHUNK, :]
        for j in range(SSD_CONV - 1):
            conv = conv + cw_ref[j:j + 1, :] * pltpu.roll(win, SSD_CONV - 1 - j, 0)[pad:pad + CHUNK, :]
        xs = _silu(conv)
        x = xs[:, 0:D_SSD]
        dtin = dt_ref[b, rows, :] + dtb
        sp = jnp.maximum(dtin, 0.0) + jnp.log1p(jnp.exp(-jnp.abs(dtin)))
        dtr = _dot_split_r(sp, expand)
        adt = dtr * a_rep
        r = _dot_split(tu, jnp.concatenate([adt, adt * su], axis=-1))
        acs, rs = r[0:CHUNK, 0:D_SSD], r[CHUNK:2 * CHUNK, 0:D_SSD]
        dm = r[0:CHUNK, D_SSD:2 * D_SSD]
        xdt = x * dtr
        xdt_b = xdt.astype(BF16)
        ys = []
        for g in range(SSD_GROUPS):
            gs = slice(g * gw, (g + 1) * gw)
            bg = xs[:, D_SSD + g * SSD_STATE:D_SSD + (g + 1) * SSD_STATE].astype(BF16)
            cg = xs[:, D_SSD + (SSD_GROUPS + g) * SSD_STATE:D_SSD + (SSD_GROUPS + g + 1) * SSD_STATE].astype(BF16)
            st = st_ref[b, g]
            scores = _dot_nt(cg, bg) * causal
            y_off = _dot(cg, st.astype(BF16)) * jnp.exp2(acs[:, gs])
            yd = []
            for hh in range(hpg):
                sl = slice((g * hpg + hh) * SSD_HEADDIM, (g * hpg + hh + 1) * SSD_HEADDIM)
                yd.append(_dot((scores * jnp.exp2(dm[:, sl])).astype(BF16), xdt_b[:, sl]))
            ys.append(jnp.concatenate(yd, axis=-1) + y_off)
            xd = (xdt[:, gs] * jnp.exp2(rs[:, gs])).astype(BF16)
            st_ref[b, g] = jnp.exp2(acs[CHUNK - 1:CHUNK, gs]) * st + _dot_tn(bg, xd)
        y = jnp.concatenate(ys, axis=-1) + x * d_rep
        y = y * _silu(z_ref[b, rows, :])
        yn = [y[:, g * gw:(g + 1) * gw] * lax.rsqrt(
            jnp.mean(y[:, g * gw:(g + 1) * gw] * y[:, g * gw:(g + 1) * gw], axis=-1, keepdims=True) + EPS)
            for g in range(SSD_GROUPS)]
        o_ref[b, rows, :] = (jnp.concatenate(yn, axis=-1) * nw).astype(o_ref.dtype)

    def end_of_tile():
        ext_ref[:, 0:pad, :] = xbc_ref[:, tl - pad:tl, :]

    return chunk_one, end_of_tile


N_HG_IN, N_SSD_IN = 7, 13


def _mixers_body(*refs, tl):
    hg_in, ssd_in = refs[0:N_HG_IN], refs[N_HG_IN:N_HG_IN + N_SSD_IN]
    oa_ref, ob_ref, hst_ref, sst_ref, ext_ref = refs[N_HG_IN + N_SSD_IN:]
    hg_one = _hgrn_chunk(*hg_in, oa_ref, hst_ref)
    ssd_one, ssd_end = _ssd_chunk(*ssd_in, ob_ref, sst_ref, ext_ref, tl=tl)

    def chunk_step(c, carry):
        r0 = pl.multiple_of(c * CHUNK, CHUNK)
        for b in range(oa_ref.shape[0]):
            hg_one(b, pl.ds(r0, CHUNK))
            ssd_one(b, r0)
        return carry

    lax.fori_loop(0, tl // CHUNK, chunk_step, 0, unroll=2)
    ssd_end()


def _mixers(qfig3, xbc3, z3, dt3, hg_consts, hg_params, ssd_consts, ssd_params, tl):
    b, l, _ = qfig3.shape
    tok = lambda w: pl.BlockSpec((b, tl, w), lambda i: (0, i, 0))
    hg_rest = tuple(hg_consts) + tuple(hg_params)
    ssd_rest = tuple(ssd_params) + tuple(ssd_consts)
    assert 1 + len(hg_rest) == N_HG_IN and 3 + len(ssd_rest) == N_SSD_IN
    return pl.pallas_call(
        functools.partial(_mixers_body, tl=tl),
        grid=(l // tl,),
        in_specs=[tok(4 * D_HG)] + [_full(a) for a in hg_rest]
        + [tok(D_XBC), tok(D_SSD), tok(LANE)] + [_full(a) for a in ssd_rest],
        out_specs=[tok(D_HG), tok(D_SSD)],
        out_shape=[jax.ShapeDtypeStruct((b, l, D_HG), BF16),
                   jax.ShapeDtypeStruct((b, l, D_SSD), BF16)],
        scratch_shapes=[
            pltpu.VMEM((b, HG_HEADS, HG_DK, HG_DK), F32),
            pltpu.VMEM((b, SSD_GROUPS, SSD_STATE, D_SSD // SSD_GROUPS), F32),
            pltpu.VMEM((b, tl + SUBLANE, D_XBC), F32),
        ],
        compiler_params=_params(("arbitrary",)),
        name="hgrn2_ssd",
    )(qfig3, *hg_rest, xbc3, z3, dt3, *ssd_rest)


def _s5_expand(r, rep, row_group, col_group):
    big = jnp.dot(r, rep, preferred_element_type=BF16)
    rows, cols = big.shape[-2:]
    rg = row_group(lax.broadcasted_iota(jnp.int32, (rows, cols), 0))
    cg = col_group(lax.broadcasted_iota(jnp.int32, (rows, cols), 1))
    return jnp.where(rg == cg, big, jnp.zeros_like(big))


def _s5_consts(a_re, a_im, b_re, b_im, c_re, c_im, log_dt):
    t_, gc, gp = S5_T, S5_GROUP_CH, S5_STATE
    ar, ai = a_re.astype(F32), a_im.astype(F32)
    delta = jnp.exp(log_dt.astype(F32))[:, None]
    mag = jnp.exp(ar * delta)
    ab_re, ab_im = mag * jnp.cos(ai * delta), mag * jnp.sin(ai * delta)
    den = ar * ar + ai * ai
    nr, ni = ab_re - 1.0, ab_im
    fr = (nr * ar + ni * ai) / den
    fi = (ni * ar - nr * ai) / den
    br, bi = b_re.astype(F32).transpose(0, 2, 1), b_im.astype(F32).transpose(0, 2, 1)
    bb_re = fr[:, None, :] * br - fi[:, None, :] * bi
    bb_im = fr[:, None, :] * bi + fi[:, None, :] * br
    lam_r, lam_i = ar * delta, ai * delta
    tau = jnp.arange(t_ + 1, dtype=F32)
    pr = jnp.exp(tau[:, None, None] * lam_r) * jnp.cos(tau[:, None, None] * lam_i)
    pi = jnp.exp(tau[:, None, None] * lam_r) * jnp.sin(tau[:, None, None] * lam_i)
    qr = jnp.exp(lam_r[..., None] * tau) * jnp.cos(lam_i[..., None] * tau)
    qi = jnp.exp(lam_r[..., None] * tau) * jnp.sin(lam_i[..., None] * tau)
    abr = pr[:, :, None, :] * bb_re - pi[:, :, None, :] * bb_im
    abi = pr[:, :, None, :] * bb_im + pi[:, :, None, :] * bb_re
    cr, ci = c_re.astype(F32), c_im.astype(F32)
    hp = lax.Precision.HIGHEST
    km = (jnp.einsum('tgcp,gdp->gctd', abr[:t_], cr, precision=hp)
          - jnp.einsum('tgcp,gdp->gctd', abi[:t_], ci, precision=hp))
    tt, jj, ii = np.arange(t_)[:, None, None], np.arange(t_)[None, :, None], np.arange(t_)[None, None, :]
    shift = np.einsum('tji,de->jtdie', (tt == ii - jj).astype(np.float32), np.eye(gc, dtype=np.float32))
    shift = shift.reshape(t_, t_ * gc, t_ * gc)
    km2 = km.astype(BF16).reshape(S5_GROUPS * gc, t_ * gc)
    r_t = jnp.stack([jnp.dot(km2, jnp.asarray(shift[j], BF16), preferred_element_type=BF16) for j in range(t_)], 0)
    r_t = r_t.reshape(t_, S5_LT, LANE, t_ * gc).transpose(1, 0, 2, 3).reshape(S5_LT, t_ * LANE, t_ * gc)
    rev = np.arange(t_ - 1, -1, -1)
    r_w = jnp.concatenate([abr[rev], abi[rev]], axis=-1).astype(BF16)
    r_w = r_w.reshape(t_, S5_LT, S5_GPT, gc, 2 * gp).transpose(1, 0, 2, 3, 4).reshape(S5_LT, t_ * LANE, 2 * gp)
    crt, cit = cr.transpose(0, 2, 1)[:, :, None, :], ci.transpose(0, 2, 1)[:, :, None, :]
    q1r, q1i = qr[:, :, 1:, None], qi[:, :, 1:, None]
    vr = (crt * q1r - cit * q1i).astype(BF16).reshape(S5_LT, S5_SW, t_ * gc)
    vi = (-(crt * q1i + cit * q1r)).astype(BF16).reshape(S5_LT, S5_SW, t_ * gc)
    r_v = jnp.concatenate([vr, vi], axis=1)
    src, dst = np.arange(t_ * gc), np.arange(t_ * LANE)
    rep = jnp.asarray((src[:, None] // gc == dst[None, :] // LANE) & (src[:, None] % gc == dst[None, :] % gc), BF16)
    src, dst = np.arange(2 * gp), np.arange(2 * S5_SW)
    repw = jnp.asarray((src[:, None] // gp == dst[None, :] // S5_SW) & (src[:, None] % gp == dst[None, :] % gp), BF16)
    io_group = lambda x: (x // gc) % S5_GPT
    st_group = lambda x: (x % S5_SW) // gp
    toep = _s5_expand(r_t, rep, io_group, io_group)
    wz = _s5_expand(r_w, repw, io_group, st_group)
    vv = _s5_expand(r_v, rep, st_group, io_group)
    a_t = jnp.stack([pr[t_].reshape(S5_LT, S5_SW), pi[t_].reshape(S5_LT, S5_SW)], axis=1)
    return toep, wz, vv, a_t


def _s5_body(u_ref, toep_ref, wz_ref, vv_ref, at_ref, o_ref, z_ref, xp_ref, st_ref, *, n_rows):
    @pl.when(pl.program_id(2) == 0)
    def _():
        st_ref[...] = jnp.zeros_like(st_ref)

    fold = lambda j: pl.ds(j, n_rows, stride=S5_T)
    ub = jnp.concatenate([u_ref[fold(j), :] for j in range(S5_T)], axis=-1).astype(BF16)
    z_ref[...] = _dot(ub, wz_ref[...])
    ar, ai = at_ref[0:1, :], at_ref[1:2, :]
    re, im = slice(0, S5_SW), slice(S5_SW, 2 * S5_SW)

    def step(c, carry):
        xr, xi = carry
        row = pl.ds(c, 1)
        xp_ref[row, re] = xr
        xp_ref[row, im] = xi
        return (xr * ar - xi * ai + z_ref[row, re], xr * ai + xi * ar + z_ref[row, im])

    xr, xi = lax.fori_loop(0, n_rows, step, (st_ref[0:1, re], st_ref[0:1, im]), unroll=8)
    st_ref[0:1, re] = xr
    st_ref[0:1, im] = xi
    half = (S5_T // 2) * LANE
    y = jnp.concatenate([_dot(ub[:, 0:half], toep_ref[0:half, 0:half]), _dot(ub, toep_ref[:, half:])], axis=-1)
    y = y + _dot(xp_ref[...].astype(BF16), vv_ref[...])
    for i in range(S5_T):
        o_ref[fold(i), :] = y[:, i * LANE:(i + 1) * LANE]


def _s5(u4, consts, layer, batch, n_rows):
    toep, wz, vv, a_t = consts
    _, n, w = u4.shape
    per_batch = n // batch // (n_rows * S5_T)
    blk = pl.BlockSpec((None, n_rows * S5_T, w), lambda k, bi, i: (k, bi * per_batch + i, 0))
    wspec = lambda a: pl.BlockSpec((None, None) + a.shape[2:], lambda k, bi, i: (layer, k, 0, 0))
    return pl.pallas_call(
        functools.partial(_s5_body, n_rows=n_rows),
        grid=(S5_LT, batch, per_batch),
        in_specs=[blk, wspec(toep), wspec(wz), wspec(vv), wspec(a_t)],
        out_specs=blk,
        out_shape=jax.ShapeDtypeStruct(u4.shape, F32),
        scratch_shapes=[pltpu.VMEM((n_rows, 2 * S5_SW), F32), pltpu.VMEM((n_rows, 2 * S5_SW), F32),
                        pltpu.VMEM((SUBLANE, 2 * S5_SW), F32)],
        compiler_params=_params(("arbitrary", "arbitrary", "arbitrary")),
        name="s5",
    )(u4, toep, wz, vv, a_t)


def _outproj_ffn_body(h_ref, oa_ref, ob_ref, *rest):
    y_refs, u_refs = rest[0:S5_LT], rest[S5_LT:2 * S5_LT]
    d_ref, gw_ref, gb_ref, wo_ref, g_ref, gpre_ref, wg_ref, wu_ref, wd_ref, gpost_ref, o_ref = rest[2 * S5_LT:]
    yc = jnp.concatenate([r[...] for r in y_refs], axis=-1)
    u = jnp.concatenate([r[...] for r in u_refs], axis=-1)
    y = jax.nn.gelu(yc + d_ref[...] * u)
    oc = y * jax.nn.sigmoid(_dot(y.astype(BF16), gw_ref[...]) + gb_ref[...])
    mix = (_dot(oa_ref[...], wo_ref[0:D_HG, :])
           + _dot(ob_ref[...], wo_ref[D_HG:D_HG + D_SSD, :])
           + _dot(oc.astype(BF16), wo_ref[D_HG + D_SSD:D_MIX, :]))
    x = h_ref[...] + _rms(mix, g_ref[...])
    o_ref[...] = _ffn_apply(x, gpre_ref[...], wg_ref, wu_ref, wd_ref, gpost_ref[...])


def _outproj_ffn(h, oa, ob, y4, u4, s5_d, glu_w, glu_b, w_out, ng, wg, wu, wd, layer, tm):
    n, d = h.shape
    tok = lambda w: pl.BlockSpec((tm, w), lambda i: (i, 0))
    tile = lambda k: pl.BlockSpec((None, tm, LANE), lambda i: (k, i, 0))
    tiles = [tile(k) for k in range(S5_LT)]
    once = dict(pipeline_mode=pl.Buffered(1))
    norm = lambda k: pl.BlockSpec((None, 1, d), lambda i: (layer * 6 + k, 0, 0))
    weight = lambda r, c: pl.BlockSpec((None, None, r, c), lambda i: (layer, 1, 0, 0), **once)
    return pl.pallas_call(
        _outproj_ffn_body,
        grid=(n // tm,),
        in_specs=[tok(d), tok(D_HG), tok(D_SSD)] + tiles + tiles + [
            pl.BlockSpec((None, 1, D_S5), lambda i: (layer, 0, 0)),
            pl.BlockSpec((None, D_S5, D_S5), lambda i: (layer, 0, 0), **once),
            pl.BlockSpec((None, 1, D_S5), lambda i: (layer, 0, 0)),
            pl.BlockSpec((None, D_MIX, d), lambda i: (layer, 0, 0), **once),
            norm(3), norm(4), weight(d, D_FF), weight(d, D_FF), weight(D_FF, d), norm(5),
        ],
        out_specs=tok(d),
        out_shape=jax.ShapeDtypeStruct((n, d), F32),
        compiler_params=_params(("arbitrary",)),
        name="outproj_ffn",
    )(h, oa, ob, *([y4] * S5_LT), *([u4] * S5_LT), s5_d, glu_w, glu_b, w_out, ng, ng, wg, wu, wd, ng)


def _tile(n, pref):
    return pref if n % pref == 0 else n


def kernel(x, norm_g, ffn_w_gate, ffn_w_up, ffn_w_down, w_in, w_out, hg_lb_logits, hg_gnorm, ssd_conv_w, ssd_conv_b,
           ssd_dt_bias, ssd_A_log, ssd_D, ssd_norm, s5_A_re, s5_A_im, s5_B_re, s5_B_im, s5_C_re, s5_C_im, s5_D,
           s5_log_dt, s5_glu_w, s5_glu_b):
    bsz, seq, d = x.shape
    depth = norm_g.shape[0]
    n = bsz * seq
    assert d == D_MODEL and seq % (CHUNK * S5_T) == 0 and w_in.shape[-1] == COL_TAIL + SSD_HEADS + D_S5

    ng = norm_g.astype(F32).reshape(depth * 6, 1, d)
    wg, wu, wd = ffn_w_gate.astype(BF16), ffn_w_up.astype(BF16), ffn_w_down.astype(BF16)
    w_in_b = w_in.astype(BF16)
    w_tail = jnp.concatenate([
        w_in_b[..., COL_TAIL + SSD_HEADS:], w_in_b[..., COL_TAIL:COL_TAIL + SSD_HEADS],
        jnp.zeros(w_in.shape[:-1] + (LANE - SSD_HEADS,), BF16)], axis=-1)
    w_out_b = w_out.astype(BF16)
    glu_w_b = s5_glu_w.astype(BF16)
    lb = jnp.cumsum(jax.nn.softmax(hg_lb_logits.astype(F32), axis=0), axis=0)
    lb = (lb - lb[:1])[:, None, :]
    loglb, l1m, omlb = jnp.log(lb), jnp.log1p(-lb), 1.0 - lb
    gn = hg_gnorm.astype(F32)[:, None, :]
    a_rep = jnp.repeat(-jnp.exp(ssd_A_log.astype(F32)) * LOG2E, SSD_HEADDIM, axis=-1)[:, None, :]
    d_rep = jnp.repeat(ssd_D.astype(F32), SSD_HEADDIM, axis=-1)[:, None, :]
    dtb = jnp.pad(ssd_dt_bias.astype(F32), ((0, 0), (0, LANE - SSD_HEADS)))[:, None, :]
    cw, cb, nw = ssd_conv_w.astype(F32), ssd_conv_b.astype(F32)[:, None, :], ssd_norm.astype(F32)[:, None, :]
    s5_c = jax.vmap(_s5_consts)(s5_A_re, s5_A_im, s5_B_re, s5_B_im, s5_C_re, s5_C_im, s5_log_dt)
    s5_d, glu_b = s5_D.astype(F32)[:, None, :], s5_glu_b.astype(F32)[:, None, :]
    hg_c = _hgrn_consts()
    ssd_c = _ssd_consts()

    tm = _tile(n, 512)
    tl_mix = _tile(seq, 512)
    s5_rows = _tile(seq // S5_T, 512)

    h = x.astype(F32).reshape(n, d)
    for l in range(depth):
        h = _ffn(h, ng, wg, wu, wd, l, tm)
        qfig, xbc, z, u4, dt = _inproj(h, ng, w_in_b, w_tail, l, tm)
        tok3 = lambda a: a.reshape(bsz, seq, a.shape[-1])
        oa, ob = _mixers(tok3(qfig), tok3(xbc), tok3(z), tok3(dt), hg_c, (loglb[l], l1m[l], omlb[l], gn[l]),
                         ssd_c, (cw[l], cb[l], dtb[l], a_rep[l], d_rep[l], nw[l]), tl_mix)
        y4 = _s5(u4, s5_c, l, bsz, s5_rows)
        h = _outproj_ffn(h, oa.reshape(n, D_HG), ob.reshape(n, D_SSD), y4, u4, s5_d, glu_w_b, glu_b, w_out_b, ng,
                         wg, wu, wd, l, tm)
    return h.reshape(bsz, seq, d).astype(x.dtype)
```

```python
import functools

import numpy as np
import jax
import jax.numpy as jnp
from jax import lax
from jax.experimental import pallas as pl
from jax.experimental.pallas import tpu as pltpu

F32 = jnp.float32
BF16 = jnp.bfloat16
EPS = 1e-6
LOG2E = 1.4426950408889634

D_MODEL = 1024
D_FF = 2816
HG_HEADS, HG_DK = 4, 128
D_HG = HG_HEADS * HG_DK
SSD_HEADS, SSD_HEADDIM, SSD_GROUPS, SSD_STATE, SSD_CONV = 8, 64, 2, 128, 4
D_SSD = SSD_HEADS * SSD_HEADDIM
D_XBC = D_SSD + 2 * SSD_GROUPS * SSD_STATE
S5_GROUPS, S5_GROUP_CH, S5_STATE = 32, 16, 64
D_S5 = S5_GROUPS * S5_GROUP_CH
D_MIX = D_HG + D_SSD + D_S5
CHUNK = 64
LANE = 128
SUBLANE = 8
S5_T = 8
S5_LT = D_S5 // LANE
S5_GPT = LANE // S5_GROUP_CH
S5_SW = S5_GPT * S5_STATE

COL_Z, COL_XBC, COL_TAIL = 4 * D_HG, 4 * D_HG + D_SSD, 4 * D_HG + D_SSD + D_XBC
D_TAIL = D_S5 + LANE
VMEM_LIMIT = 56 * 1024 * 1024


def _dot(a, b):
    return jnp.dot(a, b, preferred_element_type=F32)


def _dot_nt(a, b):
    return lax.dot_general(a, b, (((1,), (1,)), ((), ())), preferred_element_type=F32)


def _dot_tn(a, b):
    return lax.dot_general(a, b, (((0,), (0,)), ((), ())), preferred_element_type=F32)


def _split(x):
    hi = x.astype(BF16)
    return hi, (x - hi.astype(F32)).astype(BF16)


def _dot_split(m2, x):
    hi, lo = _split(x)
    return _dot(m2, jnp.concatenate([hi, lo], axis=0))


def _dot_split_r(x, m2):
    hi, lo = _split(x)
    return _dot(jnp.concatenate([hi, lo], axis=1), m2)


def _rms(x, g):
    return x * lax.rsqrt(jnp.mean(x * x, axis=-1, keepdims=True) + EPS) * g


def _silu(x):
    return x * jax.nn.sigmoid(x)


def _params(sem):
    return pltpu.CompilerParams(dimension_semantics=sem, vmem_limit_bytes=VMEM_LIMIT)


def _full(a):
    return pl.BlockSpec(a.shape, lambda *_: (0,) * a.ndim)


def _ffn_apply(x, gpre, wg_ref, wu_ref, wd_ref, gpost):
    def half(xh):
        xn = _rms(xh, gpre).astype(BF16)
        act = (_silu(_dot(xn, wg_ref[...])) * _dot(xn, wu_ref[...])).astype(BF16)
        return xh + 0.5 * _rms(_dot(act, wd_ref[...]), gpost)

    rows = x.shape[0] // 2
    return jnp.concatenate([half(x[0:rows]), half(x[rows:])], axis=0)


def _ffn_body(x_ref, gpre_ref, wg_ref, wu_ref, wd_ref, gpost_ref, o_ref):
    o_ref[...] = _ffn_apply(x_ref[...], gpre_ref[...], wg_ref, wu_ref, wd_ref, gpost_ref[...])


def _ffn(h, ng, wg, wu, wd, layer, tm):
    n, d = h.shape
    weight = lambda r, c: pl.BlockSpec((None, None, r, c), lambda i: (layer, 0, 0, 0),
                                       pipeline_mode=pl.Buffered(1))
    return pl.pallas_call(
        _ffn_body,
        grid=(n // tm,),
        in_specs=[
            pl.BlockSpec((tm, d), lambda i: (i, 0)),
            pl.BlockSpec((None, 1, d), lambda i: (layer * 6, 0, 0)),
            weight(d, D_FF), weight(d, D_FF), weight(D_FF, d),
            pl.BlockSpec((None, 1, d), lambda i: (layer * 6 + 1, 0, 0)),
        ],
        out_specs=pl.BlockSpec((tm, d), lambda i: (i, 0)),
        out_shape=jax.ShapeDtypeStruct((n, d), F32),
        compiler_params=_params(("arbitrary",)),
        name="ffn",
    )(h, ng, wg, wu, wd, ng)


def _inproj_body(x_ref, g_ref, w_ref, wt_ref, qfig_ref, xbc_ref, z_ref, u_ref, dt_ref):
    xn = _rms(x_ref[...], g_ref[...]).astype(BF16)
    qfig_ref[...] = _dot(xn, w_ref[:, 0:COL_Z])
    z_ref[...] = _dot(xn, w_ref[:, COL_Z:COL_XBC])
    xbc_ref[...] = _dot(xn, w_ref[:, COL_XBC:COL_TAIL])
    u = _dot(xn, wt_ref[:, 0:D_S5])
    for k in range(S5_LT):
        u_ref[k] = u[:, k * LANE:(k + 1) * LANE]
    dt_ref[...] = _dot(xn, wt_ref[:, D_S5:D_TAIL])


def _inproj(h, ng, w_in_b, w_tail, layer, tm):
    n, d = h.shape
    tok = lambda w: pl.BlockSpec((tm, w), lambda i: (i, 0))
    out_w = (4 * D_HG, D_XBC, D_SSD)
    return pl.pallas_call(
        _inproj_body,
        grid=(n // tm,),
        in_specs=[
            tok(d),
            pl.BlockSpec((None, 1, d), lambda i: (layer * 6 + 2, 0, 0)),
            pl.BlockSpec((None, d, w_in_b.shape[-1]), lambda i: (layer, 0, 0)),
            pl.BlockSpec((None, d, D_TAIL), lambda i: (layer, 0, 0)),
        ],
        out_specs=[tok(w) for w in out_w] + [pl.BlockSpec((S5_LT, tm, LANE), lambda i: (0, i, 0)), tok(LANE)],
        out_shape=[jax.ShapeDtypeStruct((n, w), F32) for w in out_w]
        + [jax.ShapeDtypeStruct((S5_LT, n, LANE), F32), jax.ShapeDtypeStruct((n, LANE), F32)],
        compiler_params=_params(("arbitrary",)),
        name="inproj",
    )(h, ng, w_in_b, w_tail)


HG_LEVELS = (32, 16, 8, 4, 2)


def _block_tri_up(m, n=CHUNK):
    t = np.arange(n)
    same = (t[:, None] // m) == (t[None, :] // m)
    tri = same & (t[None, :] <= t[:, None])
    up = same & (t[None, :] > t[:, None])
    return tri.astype(np.float32), up.astype(np.float32)


def _hgrn_consts():
    mats, masks = [], []
    for m in (CHUNK,) + HG_LEVELS:
        mats += list(_block_tri_up(m))
    t = np.arange(CHUNK)
    for m in HG_LEVELS + (1,):
        bt, bs = t[:, None] // m, t[None, :] // m
        masks.append(((bt % 2 == 1) & (bs == bt - 1)).astype(np.float32))
    masks.append(np.eye(CHUNK, dtype=np.float32))
    cm = np.concatenate(mats, 0)
    return jnp.asarray(np.concatenate([cm, cm], 1), BF16), jnp.asarray(np.stack(masks, 0), F32)


def _hgrn_chunk(p_ref, cm_ref, mask_ref, loglb_ref, l1m_ref, omlb_ref, gn_ref, o_ref, st_ref):
    @pl.when(pl.program_id(0) == 0)
    def _():
        st_ref[...] = jnp.zeros_like(st_ref)

    cm = cm_ref[...]
    loglb, l1m, omlb, gn = loglb_ref[...], l1m_ref[...], omlb_ref[...], gn_ref[...]
    nl = len(HG_LEVELS)

    def chunk_one(b, rows):
        q = _silu(p_ref[b, rows, 0:D_HG])
        fr = p_ref[b, rows, D_HG:2 * D_HG]
        v = p_ref[b, rows, 2 * D_HG:3 * D_HG].astype(BF16)
        gr = p_ref[b, rows, 3 * D_HG:4 * D_HG]
        ls = jnp.minimum(fr, 0.0) - jnp.log1p(jnp.exp(-jnp.abs(fr)))
        bb = l1m + ls
        lf = jnp.maximum(loglb, bb) + jnp.log1p(jnp.exp(-jnp.abs(loglb - bb)))
        kk = omlb * jnp.exp(ls - fr)
        lf2 = lf * LOG2E
        cs = _dot_split(cm, lf2)
        kb = kk.astype(BF16)
        q01 = jnp.concatenate([q, q * jnp.exp2(lf2)], axis=0).astype(BF16)
        outs = []
        for h in range(HG_HEADS):
            hs = slice(h * HG_DK, (h + 1) * HG_DK)
            qh, kh = q[:, hs], kk[:, hs]
            st = st_ref[b, h]
            c64, r64 = cs[0:CHUNK, hs], cs[CHUNK:2 * CHUNK, hs]
            p01 = _dot_nt(q01[:, hs], kb[:, hs])
            scores = mask_ref[nl + 1] * p01[0:CHUNK] + mask_ref[nl] * p01[CHUNK:2 * CHUNK]
            for li in range(nl):
                base = 2 * CHUNK * (li + 1)
                cmm, rmm = cs[base:base + CHUNK, hs], cs[base + CHUNK:base + 2 * CHUNK, hs]
                p = _dot_nt((qh * jnp.exp2(cmm)).astype(BF16), (kh * jnp.exp2(rmm)).astype(BF16))
                scores = scores + mask_ref[li] * p
            o = _dot_nt((qh * jnp.exp2(c64)).astype(BF16), st.astype(BF16)) + _dot(scores.astype(BF16), v[:, hs])
            ke = (kh * jnp.exp2(r64)).astype(BF16)
            st_ref[b, h] = jnp.exp2(c64[CHUNK - 1:CHUNK, :]) * st + _dot_tn(v[:, hs], ke)
            outs.append(o * lax.rsqrt(jnp.mean(o * o, axis=-1, keepdims=True) + EPS) * gn)
        o_ref[b, rows, :] = (jnp.concatenate(outs, axis=-1) * _silu(gr)).astype(o_ref.dtype)

    return chunk_one


def _ssd_consts():
    tri, up = _block_tri_up(CHUNK)
    t = np.arange(CHUNK)
    su = (t[:, None] > t[None, :]).astype(np.float32)
    causal = (t[:, None] >= t[None, :]).astype(np.float32)
    expand = np.zeros((LANE, D_SSD), np.float32)
    for h in range(SSD_HEADS):
        expand[h, h * SSD_HEADDIM:(h + 1) * SSD_HEADDIM] = 1.0
    tu = np.concatenate([tri, up], 0)
    return (jnp.asarray(np.concatenate([tu, tu], 1), BF16), jnp.asarray(np.tile(su, (1, SSD_HEADS)), F32),
            jnp.asarray(causal, F32), jnp.asarray(np.concatenate([expand, expand], 0), BF16))


def _ssd_chunk(xbc_ref, z_ref, dt_ref, cw_ref, cb_ref, dtb_ref, a_ref, d_ref, nw_ref, tu_ref, su_ref, causal_ref,
               ex_ref, o_ref, st_ref, ext_ref, *, tl):
    pad = SUBLANE
    nb = xbc_ref.shape[0]

    @pl.when(pl.program_id(0) == 0)
    def _():
        st_ref[...] = jnp.zeros_like(st_ref)
        ext_ref[:, 0:pad, :] = jnp.zeros((nb, pad, D_XBC), F32)

    ext_ref[:, pad:pad + tl, :] = xbc_ref[...]

    tu, su, causal, expand = tu_ref[...], su_ref[...], causal_ref[...], ex_ref[...]
    a_rep, d_rep, nw, cb, dtb = a_ref[...], d_ref[...], nw_ref[...], cb_ref[...], dtb_ref[...]
    gw = D_SSD // SSD_GROUPS
    hpg = SSD_HEADS // SSD_GROUPS

    def chunk_one(b, r0):
        rows = pl.ds(r0, CHUNK)
        win = ext_ref[b, pl.ds(r0, CHUNK + pad), :]
        conv = cb + cw_ref[SSD_CONV - 1:SSD_CONV, :] * win[pad:pad + CHUNK, :]
        for j in range(SSD_CONV - 1):
            conv = conv + cw_ref[j:j + 1, :] * pltpu.roll(win, SSD_CONV - 1 - j, 0)[pad:pad + CHUNK, :]
        xs = _silu(conv)
        x = xs[:, 0:D_SSD]
        dtin = dt_ref[b, rows, :] + dtb
        sp = jnp.maximum(dtin, 0.0) + jnp.log1p(jnp.exp(-jnp.abs(dtin)))
        dtr = _dot_split_r(sp, expand)
        adt = dtr * a_rep
        r = _dot_split(tu, jnp.concatenate([adt, adt * su], axis=-1))
        acs, rs = r[0:CHUNK, 0:D_SSD], r[CHUNK:2 * CHUNK, 0:D_SSD]
        dm = r[0:CHUNK, D_SSD:2 * D_SSD]
        xdt = x * dtr
        xdt_b = xdt.astype(BF16)
        ys = []
        for g in range(SSD_GROUPS):
            gs = slice(g * gw, (g + 1) * gw)
            bg = xs[:, D_SSD + g * SSD_STATE:D_SSD + (g + 1) * SSD_STATE].astype(BF16)
            cg = xs[:, D_SSD + (SSD_GROUPS + g) * SSD_STATE:D_SSD + (SSD_GROUPS + g + 1) * SSD_STATE].astype(BF16)
            st = st_ref[b, g]
            scores = _dot_nt(cg, bg) * causal
            y_off = _dot(cg, st.astype(BF16)) * jnp.exp2(acs[:, gs])
            yd = []
            for hh in range(hpg):
                sl = slice((g * hpg + hh) * SSD_HEADDIM, (g * hpg + hh + 1) * SSD_HEADDIM)
                yd.append(_dot((scores * jnp.exp2(dm[:, sl])).astype(BF16), xdt_b[:, sl]))
            ys.append(jnp.concatenate(yd, axis=-1) + y_off)
            xd = (xdt[:, gs] * jnp.exp2(rs[:, gs])).astype(BF16)
            st_ref[b, g] = jnp.exp2(acs[CHUNK - 1:CHUNK, gs]) * st + _dot_tn(bg, xd)
        y = jnp.concatenate(ys, axis=-1) + x * d_rep
        y = y * _silu(z_ref[b, rows, :])
        yn = [y[:, g * gw:(g + 1) * gw] * lax.rsqrt(
            jnp.mean(y[:, g * gw:(g + 1) * gw] * y[:, g * gw:(g + 1) * gw], axis=-1, keepdims=True) + EPS)
            for g in range(SSD_GROUPS)]
        o_ref[b, rows, :] = (jnp.concatenate(yn, axis=-1) * nw).astype(o_ref.dtype)

    def end_of_tile():
        ext_ref[:, 0:pad, :] = xbc_ref[:, tl - pad:tl, :]

    return chunk_one, end_of_tile


N_HG_IN, N_SSD_IN = 7, 13


def _mixers_body(*refs, tl):
    hg_in, ssd_in = refs[0:N_HG_IN], refs[N_HG_IN:N_HG_IN + N_SSD_IN]
    oa_ref, ob_ref, hst_ref, sst_ref, ext_ref = refs[N_HG_IN + N_SSD_IN:]
    hg_one = _hgrn_chunk(*hg_in, oa_ref, hst_ref)
    ssd_one, ssd_end = _ssd_chunk(*ssd_in, ob_ref, sst_ref, ext_ref, tl=tl)

    def chunk_step(c, carry):
        r0 = pl.multiple_of(c * CHUNK, CHUNK)
        for b in range(oa_ref.shape[0]):
            hg_one(b, pl.ds(r0, CHUNK))
            ssd_one(b, r0)
        return carry

    lax.fori_loop(0, tl // CHUNK, chunk_step, 0, unroll=2)
    ssd_end()


def _mixers(qfig3, xbc3, z3, dt3, hg_consts, hg_params, ssd_consts, ssd_params, tl):
    b, l, _ = qfig3.shape
    tok = lambda w: pl.BlockSpec((b, tl, w), lambda i: (0, i, 0))
    hg_rest = tuple(hg_consts) + tuple(hg_params)
    ssd_rest = tuple(ssd_params) + tuple(ssd_consts)
    assert 1 + len(hg_rest) == N_HG_IN and 3 + len(ssd_rest) == N_SSD_IN
    return pl.pallas_call(
        functools.partial(_mixers_body, tl=tl),
        grid=(l // tl,),
        in_specs=[tok(4 * D_HG)] + [_full(a) for a in hg_rest]
        + [tok(D_XBC), tok(D_SSD), tok(LANE)] + [_full(a) for a in ssd_rest],
        out_specs=[tok(D_HG), tok(D_SSD)],
        out_shape=[jax.ShapeDtypeStruct((b, l, D_HG), BF16),
                   jax.ShapeDtypeStruct((b, l, D_SSD), BF16)],
        scratch_shapes=[
            pltpu.VMEM((b, HG_HEADS, HG_DK, HG_DK), F32),
            pltpu.VMEM((b, SSD_GROUPS, SSD_STATE, D_SSD // SSD_GROUPS), F32),
            pltpu.VMEM((b, tl + SUBLANE, D_XBC), F32),
        ],
        compiler_params=_params(("arbitrary",)),
        name="hgrn2_ssd",
    )(qfig3, *hg_rest, xbc3, z3, dt3, *ssd_rest)


def _s5_expand(r, rep, row_group, col_group):
    big = jnp.dot(r, rep, preferred_element_type=BF16)
    rows, cols = big.shape[-2:]
    rg = row_group(lax.broadcasted_iota(jnp.int32, (rows, cols), 0))
    cg = col_group(lax.broadcasted_iota(jnp.int32, (rows, cols), 1))
    return jnp.where(rg == cg, big, jnp.zeros_like(big))


def _s5_consts(a_re, a_im, b_re, b_im, c_re, c_im, log_dt):
    t_, gc, gp = S5_T, S5_GROUP_CH, S5_STATE
    ar, ai = a_re.astype(F32), a_im.astype(F32)
    delta = jnp.exp(log_dt.astype(F32))[:, None]
    mag = jnp.exp(ar * delta)
    ab_re, ab_im = mag * jnp.cos(ai * delta), mag * jnp.sin(ai * delta)
    den = ar * ar + ai * ai
    nr, ni = ab_re - 1.0, ab_im
    fr = (nr * ar + ni * ai) / den
    fi = (ni * ar - nr * ai) / den
    br, bi = b_re.astype(F32).transpose(0, 2, 1), b_im.astype(F32).transpose(0, 2, 1)
    bb_re = fr[:, None, :] * br - fi[:, None, :] * bi
    bb_im = fr[:, None, :] * bi + fi[:, None, :] * br
    lam_r, lam_i = ar * delta, ai * delta
    tau = jnp.arange(t_ + 1, dtype=F32)
    pr = jnp.exp(tau[:, None, None] * lam_r) * jnp.cos(tau[:, None, None] * lam_i)
    pi = jnp.exp(tau[:, None, None] * lam_r) * jnp.sin(tau[:, None, None] * lam_i)
    qr = jnp.exp(lam_r[..., None] * tau) * jnp.cos(lam_i[..., None] * tau)
    qi = jnp.exp(lam_r[..., None] * tau) * jnp.sin(lam_i[..., None] * tau)
    abr = pr[:, :, None, :] * bb_re - pi[:, :, None, :] * bb_im
    abi = pr[:, :, None, :] * bb_im + pi[:, :, None, :] * bb_re
    cr, ci = c_re.astype(F32), c_im.astype(F32)
    hp = lax.Precision.HIGHEST
    km = (jnp.einsum('tgcp,gdp->gctd', abr[:t_], cr, precision=hp)
          - jnp.einsum('tgcp,gdp->gctd', abi[:t_], ci, precision=hp))
    tt, jj, ii = np.arange(t_)[:, None, None], np.arange(t_)[None, :, None], np.arange(t_)[None, None, :]
    shift = np.einsum('tji,de->jtdie', (tt == ii - jj).astype(np.float32), np.eye(gc, dtype=np.float32))
    shift = shift.reshape(t_, t_ * gc, t_ * gc)
    km2 = km.astype(BF16).reshape(S5_GROUPS * gc, t_ * gc)
    r_t = jnp.stack([jnp.dot(km2, jnp.asarray(shift[j], BF16), preferred_element_type=BF16) for j in range(t_)], 0)
    r_t = r_t.reshape(t_, S5_LT, LANE, t_ * gc).transpose(1, 0, 2, 3).reshape(S5_LT, t_ * LANE, t_ * gc)
    rev = np.arange(t_ - 1, -1, -1)
    r_w = jnp.concatenate([abr[rev], abi[rev]], axis=-1).astype(BF16)
    r_w = r_w.reshape(t_, S5_LT, S5_GPT, gc, 2 * gp).transpose(1, 0, 2, 3, 4).reshape(S5_LT, t_ * LANE, 2 * gp)
    crt, cit = cr.transpose(0, 2, 1)[:, :, None, :], ci.transpose(0, 2, 1)[:, :, None, :]
    q1r, q1i = qr[:, :, 1:, None], qi[:, :, 1:, None]
    vr = (crt * q1r - cit * q1i).astype(BF16).reshape(S5_LT, S5_SW, t_ * gc)
    vi = (-(crt * q1i + cit * q1r)).astype(BF16).reshape(S5_LT, S5_SW, t_ * gc)
    r_v = jnp.concatenate([vr, vi], axis=1)
    src, dst = np.arange(t_ * gc), np.arange(t_ * LANE)
    rep = jnp.asarray((src[:, None] // gc == dst[None, :] // LANE) & (src[:, None] % gc == dst[None, :] % gc), BF16)
    src, dst = np.arange(2 * gp), np.arange(2 * S5_SW)
    repw = jnp.asarray((src[:, None] // gp == dst[None, :] // S5_SW) & (src[:, None] % gp == dst[None, :] % gp), BF16)
    io_group = lambda x: (x // gc) % S5_GPT
    st_group = lambda x: (x % S5_SW) // gp
    toep = _s5_expand(r_t, rep, io_group, io_group)
    wz = _s5_expand(r_w, repw, io_group, st_group)
    vv = _s5_expand(r_v, rep, st_group, io_group)
    a_t = jnp.stack([pr[t_].reshape(S5_LT, S5_SW), pi[t_].reshape(S5_LT, S5_SW)], axis=1)
    return toep, wz, vv, a_t


def _s5_body(u_ref, toep_ref, wz_ref, vv_ref, at_ref, o_ref, z_ref, xp_ref, st_ref, *, n_rows):
    @pl.when(pl.program_id(2) == 0)
    def _():
        st_ref[...] = jnp.zeros_like(st_ref)

    fold = lambda j: pl.ds(j, n_rows, stride=S5_T)
    ub = jnp.concatenate([u_ref[fold(j), :] for j in range(S5_T)], axis=-1).astype(BF16)
    z_ref[...] = _dot(ub, wz_ref[...])
    ar, ai = at_ref[0:1, :], at_ref[1:2, :]
    re, im = slice(0, S5_SW), slice(S5_SW, 2 * S5_SW)

    def step(c, carry):
        xr, xi = carry
        row = pl.ds(c, 1)
        xp_ref[row, re] = xr
        xp_ref[row, im] = xi
        return (xr * ar - xi * ai + z_ref[row, re], xr * ai + xi * ar + z_ref[row, im])

    xr, xi = lax.fori_loop(0, n_rows, step, (st_ref[0:1, re], st_ref[0:1, im]), unroll=8)
    st_ref[0:1, re] = xr
    st_ref[0:1, im] = xi
    half = (S5_T // 2) * LANE
    y = jnp.concatenate([_dot(ub[:, 0:half], toep_ref[0:half, 0:half]), _dot(ub, toep_ref[:, half:])], axis=-1)
    y = y + _dot(xp_ref[...].astype(BF16), vv_ref[...])
    for i in range(S5_T):
        o_ref[fold(i), :] = y[:, i * LANE:(i + 1) * LANE]


def _s5(u4, consts, layer, batch, n_rows):
    toep, wz, vv, a_t = consts
    _, n, w = u4.shape
    per_batch = n // batch // (n_rows * S5_T)
    blk = pl.BlockSpec((None, n_rows * S5_T, w), lambda k, bi, i: (k, bi * per_batch + i, 0))
    wspec = lambda a: pl.BlockSpec((None, None) + a.shape[2:], lambda k, bi, i: (layer, k, 0, 0))
    return pl.pallas_call(
        functools.partial(_s5_body, n_rows=n_rows),
        grid=(S5_LT, batch, per_batch),
        in_specs=[blk, wspec(toep), wspec(wz), wspec(vv), wspec(a_t)],
        out_specs=blk,
        out_shape=jax.ShapeDtypeStruct(u4.shape, F32),
        scratch_shapes=[pltpu.VMEM((n_rows, 2 * S5_SW), F32), pltpu.VMEM((n_rows, 2 * S5_SW), F32),
                        pltpu.VMEM((SUBLANE, 2 * S5_SW), F32)],
        compiler_params=_params(("arbitrary", "arbitrary", "arbitrary")),
        name="s5",
    )(u4, toep, wz, vv, a_t)


def _outproj_ffn_body(h_ref, oa_ref, ob_ref, *rest):
    y_refs, u_refs = rest[0:S5_LT], rest[S5_LT:2 * S5_LT]
    d_ref, gw_ref, gb_ref, wo_ref, g_ref, gpre_ref, wg_ref, wu_ref, wd_ref, gpost_ref, o_ref = rest[2 * S5_LT:]
    yc = jnp.concatenate([r[...] for r in y_refs], axis=-1)
    u = jnp.concatenate([r[...] for r in u_refs], axis=-1)
    y = jax.nn.gelu(yc + d_ref[...] * u)
    oc = y * jax.nn.sigmoid(_dot(y.astype(BF16), gw_ref[...]) + gb_ref[...])
    mix = (_dot(oa_ref[...], wo_ref[0:D_HG, :])
           + _dot(ob_ref[...], wo_ref[D_HG:D_HG + D_SSD, :])
           + _dot(oc.astype(BF16), wo_ref[D_HG + D_SSD:D_MIX, :]))
    x = h_ref[...] + _rms(mix, g_ref[...])
    o_ref[...] = _ffn_apply(x, gpre_ref[...], wg_ref, wu_ref, wd_ref, gpost_ref[...])


def _outproj_ffn(h, oa, ob, y4, u4, s5_d, glu_w, glu_b, w_out, ng, wg, wu, wd, layer, tm):
    n, d = h.shape
    tok = lambda w: pl.BlockSpec((tm, w), lambda i: (i, 0))
    tile = lambda k: pl.BlockSpec((None, tm, LANE), lambda i: (k, i, 0))
    tiles = [tile(k) for k in range(S5_LT)]
    once = dict(pipeline_mode=pl.Buffered(1))
    norm = lambda k: pl.BlockSpec((None, 1, d), lambda i: (layer * 6 + k, 0, 0))
    weight = lambda r, c: pl.BlockSpec((None, None, r, c), lambda i: (layer, 1, 0, 0), **once)
    return pl.pallas_call(
        _outproj_ffn_body,
        grid=(n // tm,),
        in_specs=[tok(d), tok(D_HG), tok(D_SSD)] + tiles + tiles + [
            pl.BlockSpec((None, 1, D_S5), lambda i: (layer, 0, 0)),
            pl.BlockSpec((None, D_S5, D_S5), lambda i: (layer, 0, 0), **once),
            pl.BlockSpec((None, 1, D_S5), lambda i: (layer, 0, 0)),
            pl.BlockSpec((None, D_MIX, d), lambda i: (layer, 0, 0), **once),
            norm(3), norm(4), weight(d, D_FF), weight(d, D_FF), weight(D_FF, d), norm(5),
        ],
        out_specs=tok(d),
        out_shape=jax.ShapeDtypeStruct((n, d), F32),
        compiler_params=_params(("arbitrary",)),
        name="outproj_ffn",
    )(h, oa, ob, *([y4] * S5_LT), *([u4] * S5_LT), s5_d, glu_w, glu_b, w_out, ng, ng, wg, wu, wd, ng)


def _tile(n, pref):
    return pref if n % pref == 0 else n


def kernel(x, norm_g, ffn_w_gate, ffn_w_up, ffn_w_down, w_in, w_out, hg_lb_logits, hg_gnorm, ssd_conv_w, ssd_conv_b,
           ssd_dt_bias, ssd_A_log, ssd_D, ssd_norm, s5_A_re, s5_A_im, s5_B_re, s5_B_im, s5_C_re, s5_C_im, s5_D,
           s5_log_dt, s5_glu_w, s5_glu_b):
    bsz, seq, d = x.shape
    depth = norm_g.shape[0]
    n = bsz * seq
    assert d == D_MODEL and seq % (CHUNK * S5_T) == 0 and w_in.shape[-1] == COL_TAIL + SSD_HEADS + D_S5

    ng = norm_g.astype(F32).reshape(depth * 6, 1, d)
    wg, wu, wd = ffn_w_gate.astype(BF16), ffn_w_up.astype(BF16), ffn_w_down.astype(BF16)
    w_in_b = w_in.astype(BF16)
    w_tail = jnp.concatenate([
        w_in_b[..., COL_TAIL + SSD_HEADS:], w_in_b[..., COL_TAIL:COL_TAIL + SSD_HEADS],
        jnp.zeros(w_in.shape[:-1] + (LANE - SSD_HEADS,), BF16)], axis=-1)
    w_out_b = w_out.astype(BF16)
    glu_w_b = s5_glu_w.astype(BF16)
    lb = jnp.cumsum(jax.nn.softmax(hg_lb_logits.astype(F32), axis=0), axis=0)
    lb = (lb - lb[:1])[:, None, :]
    loglb, l1m, omlb = jnp.log(lb), jnp.log1p(-lb), 1.0 - lb
    gn = hg_gnorm.astype(F32)[:, None, :]
    a_rep = jnp.repeat(-jnp.exp(ssd_A_log.astype(F32)) * LOG2E, SSD_HEADDIM, axis=-1)[:, None, :]
    d_rep = jnp.repeat(ssd_D.astype(F32), SSD_HEADDIM, axis=-1)[:, None, :]
    dtb = jnp.pad(ssd_dt_bias.astype(F32), ((0, 0), (0, LANE - SSD_HEADS)))[:, None, :]
    cw, cb, nw = ssd_conv_w.astype(F32), ssd_conv_b.astype(F32)[:, None, :], ssd_norm.astype(F32)[:, None, :]
    s5_c = jax.vmap(_s5_consts)(s5_A_re, s5_A_im, s5_B_re, s5_B_im, s5_C_re, s5_C_im, s5_log_dt)
    s5_d, glu_b = s5_D.astype(F32)[:, None, :], s5_glu_b.astype(F32)[:, None, :]
    hg_c = _hgrn_consts()
    ssd_c = _ssd_consts()

    tm = _tile(n, 512)
    tl_mix = _tile(seq, 512)
    s5_rows = _tile(seq // S5_T, 512)

    h = x.astype(F32).reshape(n, d)
    for l in range(depth):
        h = _ffn(h, ng, wg, wu, wd, l, tm)
        qfig, xbc, z, u4, dt = _inproj(h, ng, w_in_b, w_tail, l, tm)
        tok3 = lambda a: a.reshape(bsz, seq, a.shape[-1])
        oa, ob = _mixers(tok3(qfig), tok3(xbc), tok3(z), tok3(dt), hg_c, (loglb[l], l1m[l], omlb[l], gn[l]),
                         ssd_c, (cw[l], cb[l], dtb[l], a_rep[l], d_rep[l], nw[l]), tl_mix)
        y4 = _s5(u4, s5_c, l, bsz, s5_rows)
        h = _outproj_ffn(h, oa.reshape(n, D_HG), ob.reshape(n, D_SSD), y4, u4, s5_d, glu_w_b, glu_b, w_out_b, ng,
                         wg, wu, wd, l, tm)
    return h.reshape(bsz, seq, d).astype(x.dtype)
```
